```python
import math
import jax
import jax.numpy as jnp
from jax import lax
import numpy as np

D_MODEL = 1024
BATCH = 4
SEQ = 4096
DEPTH = 1
DEC_BATCH = 8
DEC_SEQ = 32
PAST_LEN = 2048

CHUNK = 64
SB_HEADS = 8
SB_HEAD_DIM = 64
SB_WIDTH = SB_HEADS * SB_HEAD_DIM
SB_BLOCK = 128
RW_HEADS = 8
RW_HEAD_DIM = 64
RW_WIDTH = RW_HEADS * RW_HEAD_DIM
DECAY_LORA = 64
AAA_LORA = 64
GATE_LORA = 160
RW_SHIFT_WIDTH = 3 * RW_WIDTH + DECAY_LORA + AAA_LORA + GATE_LORA
PROJ_WIDTH = 3 * SB_WIDTH + RW_SHIFT_WIDTH
FFN_HIDDEN = -(-8 * D_MODEL // (3 * 256)) * 256
RMS_EPS = 1e-6
GN_EPS = 64e-5
DECAY_SCALE = math.exp(-0.5)

kernel_name = 'hybrid_stickbreak_rwkv7_stream_step'


def _rmsnorm(x, g):
    xf = x.astype(jnp.float32)
    y = xf * lax.rsqrt(jnp.mean(xf * xf, axis=-1, keepdims=True) + RMS_EPS)
    return (y * g.astype(jnp.float32)).astype(x.dtype)


def _sb_block(q, k, v, q_pos, k_pos):
    z = jnp.einsum('bqhd,bkhd->bhqk', q.astype(jnp.float32), k.astype(jnp.float32)) / math.sqrt(SB_HEAD_DIM)
    mask = k_pos[None, :] < q_pos[:, None]
    log_1mb = jnp.where(mask, jax.nn.log_sigmoid(-z), 0.0)
    after = lax.cumsum(log_1mb, axis=3, reverse=True) - log_1mb
    weight = jnp.where(mask, jnp.exp(jax.nn.log_sigmoid(z) + after), 0.0)
    o = jnp.einsum('bhqk,bkhd->bqhd', weight, v.astype(jnp.float32))
    return o.astype(q.dtype)


def _sb_prompt(q, k, v):
    B, T, H, D = q.shape
    nb = T // SB_BLOCK
    qb = jnp.moveaxis(q.reshape(B, nb, SB_BLOCK, H, D), 1, 0)
    pos = jnp.arange(T, dtype=jnp.int32)
    qpos = pos.reshape(nb, SB_BLOCK)
    ob = lax.map(lambda blk: _sb_block(blk[0], k, v, blk[1], pos), (qb, qpos))
    return jnp.moveaxis(ob, 0, 1).reshape(B, T, H, D)


def _rwkv7_scan(wkv0, r, log_w, k, v, kk, a):
    def step(S, inp):
        r_t, lw_t, k_t, v_t, kk_t, a_t = inp
        sa = jnp.einsum('bhvk,bhk->bhv', S, kk_t)
        S = (S * jnp.exp(lw_t)[:, :, None, :]
             - sa[..., :, None] * (kk_t * a_t)[..., None, :]
             + v_t[..., :, None] * k_t[..., None, :])
        return S, jnp.einsum('bhvk,bhk->bhv', S, r_t)
    xs = tuple(jnp.moveaxis(t, 1, 0) for t in (r, log_w, k, v, kk, a))
    S, ys = lax.scan(step, wkv0, xs)
    return S, jnp.moveaxis(ys, 0, 1)


def _layer(x, k_past, v_past, wkv0, shift0, p):
    B, T, _ = x.shape
    f32 = lambda t: t.astype(jnp.float32)
    h = _rmsnorm(x, p['norm_mix_g'])
    proj = h @ p['w_in']

    q, k, v = jnp.split(proj[..., :3 * SB_WIDTH], 3, axis=-1)
    q = q.reshape(B, T, SB_HEADS, SB_HEAD_DIM)
    k = k.reshape(B, T, SB_HEADS, SB_HEAD_DIM)
    v = v.reshape(B, T, SB_HEADS, SB_HEAD_DIM)
    if k_past is None:
        o_sb = _sb_prompt(q, k, v)
    else:
        past = k_past.shape[1]
        k_all = jnp.concatenate([k_past.astype(k.dtype), k], axis=1)
        v_all = jnp.concatenate([v_past.astype(v.dtype), v], axis=1)
        q_pos = past + jnp.arange(T, dtype=jnp.int32)
        k_pos = jnp.arange(past + T, dtype=jnp.int32)
        o_sb = _sb_block(q, k_all, v_all, q_pos, k_pos)

    xp = proj[..., 3 * SB_WIDTH:]
    prev = jnp.concatenate([shift0.astype(xp.dtype), xp[:, :-1]], axis=1)
    xs = f32(xp + (prev - xp) * p['mu_shift'])
    o1 = RW_WIDTH
    o2 = 2 * RW_WIDTH
    o3 = 3 * RW_WIDTH
    o4 = o3 + DECAY_LORA
    o5 = o4 + AAA_LORA
    r, kr, vr, dw, da, dg = jnp.split(xs, [o1, o2, o3, o4, o5], axis=-1)
    log_w = -DECAY_SCALE * jax.nn.sigmoid(f32(p['w0']) + jnp.tanh(dw) @ f32(p['w_decay_up']))
    a = jax.nn.sigmoid(f32(p['a0']) + da @ f32(p['w_aaa_up']))
    g = jax.nn.sigmoid(dg) @ f32(p['w_gate_up'])
    heads = lambda t: t.reshape(B, T, RW_HEADS, RW_HEAD_DIM)
    kk = heads(kr * f32(p['k_k']))
    kk = kk / jnp.maximum(jnp.linalg.norm(kk, axis=-1, keepdims=True), 1e-12)
    kr = kr * (1.0 + (a - 1.0) * f32(p['k_a']))
    r_h, k_h, v_h, a_h, lw_h = heads(r), heads(kr), heads(vr), heads(a), heads(log_w)
    S, o = _rwkv7_scan(f32(wkv0), r_h, lw_h, k_h, v_h, kk, a_h)
    mu = jnp.mean(o, axis=-1, keepdims=True)
    var = jnp.mean(jnp.square(o - mu), axis=-1, keepdims=True)
    o = ((o - mu) * lax.rsqrt(var + GN_EPS)).reshape(B, T, RW_WIDTH) * f32(p['ln_x_w']) + f32(p['ln_x_b'])
    bonus = jnp.sum(r_h * k_h * f32(p['r_k']), axis=-1, keepdims=True) * v_h
    o_rw = (o + bonus.reshape(B, T, RW_WIDTH)) * g

    mix = jnp.concatenate([o_sb.reshape(B, T, SB_WIDTH).astype(x.dtype), o_rw.astype(x.dtype)], axis=-1)
    x = x + mix @ p['w_out']
    h2 = _rmsnorm(x, p['norm_ffn_g'])
    x = x + (jax.nn.silu(h2 @ p['w_gate']) * (h2 @ p['w_up'])) @ p['w_down']
    return x, k, v, S.astype(x.dtype), xp[:, -1:]


def setup_inputs(seed: int = 0) -> dict:
    key = jax.random.key(seed)
    ks = jax.random.split(key, 26)
    nrm = lambda i, shape, scale: scale * jax.random.normal(ks[i], shape, jnp.float32)
    L = DEPTH
    return {
        'x_prompt': nrm(0, (BATCH, SEQ, D_MODEL), 1.0),
        'x_sample': nrm(1, (DEC_BATCH, DEC_SEQ, D_MODEL), 1.0),
        'cache_k': nrm(2, (L, DEC_BATCH, PAST_LEN, SB_HEADS, SB_HEAD_DIM), 1.0),
        'cache_v': nrm(3, (L, DEC_BATCH, PAST_LEN, SB_HEADS, SB_HEAD_DIM), 1.0),
        'state_wkv': nrm(4, (L, DEC_BATCH, RW_HEADS, RW_HEAD_DIM, RW_HEAD_DIM), 0.3),
        'state_shift': nrm(5, (L, DEC_BATCH, 1, RW_SHIFT_WIDTH), 1.0),
        'norm_mix_g': 1.0 + nrm(6, (L, D_MODEL), 0.02),
        'w_in': nrm(7, (L, D_MODEL, PROJ_WIDTH), D_MODEL ** -0.5),
        'mu_shift': jax.random.uniform(ks[8], (L, RW_SHIFT_WIDTH), jnp.float32),
        'w0': nrm(9, (L, RW_WIDTH), 0.5),
        'w_decay_up': nrm(10, (L, DECAY_LORA, RW_WIDTH), 0.1 * DECAY_LORA ** -0.5),
        'a0': nrm(11, (L, RW_WIDTH), 0.1),
        'w_aaa_up': nrm(12, (L, AAA_LORA, RW_WIDTH), 0.1 * AAA_LORA ** -0.5),
        'w_gate_up': nrm(13, (L, GATE_LORA, RW_WIDTH), GATE_LORA ** -0.5),
        'k_k': 0.85 + nrm(14, (L, RW_WIDTH), 0.02),
        'k_a': 1.0 + nrm(15, (L, RW_WIDTH), 0.02),
        'r_k': nrm(16, (L, RW_HEADS, RW_HEAD_DIM), 0.1),
        'ln_x_w': 1.0 + nrm(17, (L, RW_WIDTH), 0.02),
        'ln_x_b': nrm(18, (L, RW_WIDTH), 0.02),
        'w_out': nrm(19, (L, D_MODEL, D_MODEL), D_MODEL ** -0.5),
        'norm_ffn_g': 1.0 + nrm(20, (L, D_MODEL), 0.02),
        'w_gate': nrm(21, (L, D_MODEL, FFN_HIDDEN), D_MODEL ** -0.5),
        'w_up': nrm(22, (L, D_MODEL, FFN_HIDDEN), D_MODEL ** -0.5),
        'w_down': nrm(23, (L, FFN_HIDDEN, D_MODEL), FFN_HIDDEN ** -0.5),
        'norm_final_g': 1.0 + nrm(24, (D_MODEL,), 0.02),
    }


def reference(x_prompt, x_sample, cache_k, cache_v, state_wkv, state_shift,
              norm_mix_g, w_in, mu_shift, w0, w_decay_up, a0, w_aaa_up, w_gate_up,
              k_k, k_a, r_k, ln_x_w, ln_x_b, w_out, norm_ffn_g, w_gate, w_up, w_down,
              norm_final_g):
    B = x_prompt.shape[0]
    xp_, xs_ = x_prompt, x_sample
    pk, pv, pS, psh = [], [], [], []
    sk, sv, sS, ssh = [], [], [], []
    for l in range(DEPTH):
        p = {
            'norm_mix_g': norm_mix_g[l], 'w_in': w_in[l], 'mu_shift': mu_shift[l],
            'w0': w0[l], 'w_decay_up': w_decay_up[l], 'a0': a0[l], 'w_aaa_up': w_aaa_up[l],
            'w_gate_up': w_gate_up[l], 'k_k': k_k[l], 'k_a': k_a[l], 'r_k': r_k[l],
            'ln_x_w': ln_x_w[l], 'ln_x_b': ln_x_b[l], 'w_out': w_out[l],
            'norm_ffn_g': norm_ffn_g[l], 'w_gate': w_gate[l], 'w_up': w_up[l], 'w_down': w_down[l],
        }
        wkv_zero = jnp.zeros((B, RW_HEADS, RW_HEAD_DIM, RW_HEAD_DIM), x_prompt.dtype)
        shift_zero = jnp.zeros((B, 1, RW_SHIFT_WIDTH), x_prompt.dtype)
        xp_, k1, v1, S1, sh1 = _layer(xp_, None, None, wkv_zero, shift_zero, p)
        xs_, k2, v2, S2, sh2 = _layer(xs_, cache_k[l], cache_v[l], state_wkv[l], state_shift[l], p)
        pk.append(k1)
        pv.append(v1)
        pS.append(S1)
        psh.append(sh1)
        sk.append(k2)
        sv.append(v2)
        sS.append(S2)
        ssh.append(sh2)
    y_prompt = _rmsnorm(xp_, norm_final_g)
    y_sample = _rmsnorm(xs_, norm_final_g)
    return (y_prompt, y_sample,
            jnp.stack(pk), jnp.stack(pv), jnp.stack(pS), jnp.stack(psh),
            jnp.stack(sk), jnp.stack(sv), jnp.stack(sS), jnp.stack(ssh))
```

```python
import functools
import math

import jax
import jax.numpy as jnp
from jax import lax
from jax.experimental import pallas as pl
from jax.experimental.pallas import tpu as pltpu

F32 = jnp.float32
BF16 = jnp.bfloat16

LANES = 128
HEAD_DIM = 64
SB_WIDTH = 512
RW_WIDTH = 512
LORA_DECAY = 64
LORA_AAA = 64
LORA_GATE = 160
LORA_WIDTH = LORA_DECAY + LORA_AAA + LORA_GATE
LORA_PAD = 384
RMS_EPS = 1e-6
GN_EPS = 64e-5
DECAY_SCALE = math.exp(-0.5)
KEY_BLOCK = 128
CHUNK = 64
VMEM_LIMIT = 56 * 1024 * 1024


def _dot(a, b):
    return jnp.dot(a, b, preferred_element_type=F32)


def _dot_nt(a, b):
    return lax.dot_general(a, b, (((1,), (1,)), ((), ())), preferred_element_type=F32)


def _split2(x):
    hi = x.astype(BF16)
    lo = (x - hi.astype(F32)).astype(BF16)
    return hi, lo


def _split3(x):
    hi = x.astype(BF16)
    r1 = x - hi.astype(F32)
    mid = r1.astype(BF16)
    lo = (r1 - mid.astype(F32)).astype(BF16)
    return hi, mid, lo


def _mm3(a, b):
    ah, al = _split2(a)
    bh, bl = _split2(b)
    return _dot(ah, bh) + (_dot(ah, bl) + _dot(al, bh))


def _mm3_nt(a, b):
    ah, al = _split2(a)
    bh, bl = _split2(b)
    return _dot_nt(ah, bh) + (_dot_nt(ah, bl) + _dot_nt(al, bh))


def _mm_exact_rhs(x, m):
    hi, mid, lo = _split3(x)
    return _dot(hi, m) + (_dot(mid, m) + _dot(lo, m))


def _mm_exact_lhs(m, x):
    hi, mid, lo = _split3(x)
    return _dot(m, hi) + (_dot(m, mid) + _dot(m, lo))


def _sigmoid(x):
    return 1.0 / (1.0 + jnp.exp(-x))


def _rmsnorm(x, g):
    return x * lax.rsqrt(jnp.mean(x * x, axis=-1, keepdims=True) + RMS_EPS) * g


def _const_spec(shape):
    return pl.BlockSpec(shape, lambda *_: (0,) * len(shape), pipeline_mode=pl.Buffered(1))


def _norm_proj_kernel(x_ref, g_ref, wqkv_ref, wrkv_ref, wl_ref, q_ref, k_ref, v_ref, rkv_ref, l_ref):
    hb = _rmsnorm(x_ref[...], g_ref[...]).astype(BF16)
    qkv = _dot(hb, wqkv_ref[...])
    q_ref[...] = qkv[:, :SB_WIDTH]
    k_ref[...] = qkv[:, SB_WIDTH:2 * SB_WIDTH]
    v_ref[...] = qkv[:, 2 * SB_WIDTH:]
    rkv_ref[...] = _dot(hb, wrkv_ref[...])
    l_ref[...] = _dot(hb, wl_ref[...])


def _norm_proj(x2d, g, wqkv, wrkv, wl, tm):
    m, d = x2d.shape
    row = lambda w: pl.BlockSpec((tm, w), lambda i: (i, 0))
    return pl.pallas_call(
        _norm_proj_kernel,
        grid=(m // tm,),
        in_specs=[row(d), _const_spec((1, d)), _const_spec(wqkv.shape), _const_spec(wrkv.shape),
                  _const_spec(wl.shape)],
        out_specs=[row(SB_WIDTH), row(SB_WIDTH), row(SB_WIDTH), row(3 * RW_WIDTH), row(LORA_PAD)],
        out_shape=[jax.ShapeDtypeStruct((m, w), F32)
                   for w in (SB_WIDTH, SB_WIDTH, SB_WIDTH, 3 * RW_WIDTH, LORA_PAD)],
        compiler_params=pltpu.CompilerParams(dimension_semantics=("arbitrary",),
                                             vmem_limit_bytes=VMEM_LIMIT),
        name="norm_proj",
    )(x2d, g, wqkv, wrkv, wl)


def _sb_kernel(q_ref, kd_ref, vd_ref, kp_ref, vp_ref, o_ref, *, tq, causal_prefix, n_prefix):
    q = q_ref[0] * (1.0 / math.sqrt(HEAD_DIM))
    first = lax.broadcasted_iota(jnp.int32, (tq, LANES), 1) < HEAD_DIM
    qe = jnp.concatenate([jnp.where(first, q, 0.0), jnp.where(first, 0.0, q)], axis=0).astype(BF16)

    row = lax.broadcasted_iota(jnp.int32, (2 * tq, KEY_BLOCK), 0)
    col = lax.broadcasted_iota(jnp.int32, (2 * tq, KEY_BLOCK), 1)
    causal = col < jnp.where(row >= tq, row - tq, row)

    jj = lax.broadcasted_iota(jnp.int32, (KEY_BLOCK, 2 * KEY_BLOCK), 0)
    ss = lax.broadcasted_iota(jnp.int32, (KEY_BLOCK, 2 * KEY_BLOCK), 1)
    suffix = jnp.where((jj > ss) | (ss >= KEY_BLOCK), 1.0, 0.0).astype(BF16)

    def block(kb, vb, c, o, mask):
        z = _dot_nt(qe, kb.astype(BF16))
        log1mb = -(jnp.maximum(z, 0.0) + jnp.log(1.0 + jnp.exp(-jnp.abs(z))))
        if mask is not None:
            log1mb = jnp.where(mask, log1mb, 0.0)
        hi, lo = _split2(log1mb)
        sums = _dot(hi, suffix) + _dot(lo, suffix)
        w = jnp.exp(z + log1mb + (sums[:, :KEY_BLOCK] + c))
        if mask is not None:
            w = jnp.where(mask, w, 0.0)
        o = o + _dot(w.astype(BF16), vb.astype(BF16))
        return c + sums[:, KEY_BLOCK:], o

    zeros = jnp.zeros((2 * tq, LANES), F32)
    c, o = block(kd_ref[0], vd_ref[0], zeros, zeros, causal)

    n = pl.program_id(2) if causal_prefix else n_prefix

    def body(it, carry):
        start = pl.multiple_of((n - 1 - it) * KEY_BLOCK, KEY_BLOCK)
        return block(kp_ref[0, pl.ds(start, KEY_BLOCK), :], vp_ref[0, pl.ds(start, KEY_BLOCK), :],
                     carry[0], carry[1], None)

    c, o = lax.fori_loop(0, n, body, (c, o))
    o_ref[0] = jnp.where(first, o[:tq], o[tq:])


def _sb_attention(q, kd, vd, kp, vp, *, tq, causal_prefix):
    b, t, _ = q.shape
    nq = t // tq
    tp = kp.shape[1]
    blk = lambda rows: pl.BlockSpec((1, rows, LANES), lambda bi, p, i: (bi, i, p))
    pre = pl.BlockSpec((1, tp, LANES), lambda bi, p, i: (bi, 0, p))
    return pl.pallas_call(
        functools.partial(_sb_kernel, tq=tq, causal_prefix=causal_prefix, n_prefix=tp // KEY_BLOCK),
        grid=(b, SB_WIDTH // LANES, nq),
        in_specs=[blk(tq), blk(KEY_BLOCK), blk(KEY_BLOCK), pre, pre],
        out_specs=blk(tq),
        out_shape=jax.ShapeDtypeStruct(q.shape, F32),
        compiler_params=pltpu.CompilerParams(dimension_semantics=("arbitrary",) * 3,
                                             vmem_limit_bytes=VMEM_LIMIT),
        name="stickbreak",
    )(q, kd, vd, kp, vp)


def _expand(x, first):
    return jnp.concatenate([jnp.where(first, x, 0.0), jnp.where(first, 0.0, x)], axis=0)


def _rwkv_kernel(r_ref, k_ref, v_ref, l_ref, sr_ref, sk_ref, sv_ref, sl_ref, z0_ref,
                 mur_ref, muk_ref, muv_ref, mul_ref, w0_ref, wd_ref, a0_ref, wa_ref, wg_ref,
                 kkw_ref, kaw_ref, rkw_ref, lnw_ref, lnb_ref,
                 o_ref, zout_ref,
                 pr_s, pk_s, pv_s, pl_s, r_s, k_s, v_s, kk_s, b_s, lw_s, g_s, y_s, z_s,
                 *, tb, t_valid):
    t = pl.program_id(2)

    @pl.when(t == 0)
    def _():
        pr_s[...] = sr_ref[0]
        pk_s[...] = sk_ref[0]
        pv_s[...] = sv_ref[0]
        pl_s[...] = sl_ref[0]
        z_s[...] = z0_ref[0, 0]

    def shift(x_ref, carry_s, mu_ref):
        x = x_ref[0]
        prev = pltpu.roll(x, 1, 0)
        prev = jnp.where(lax.broadcasted_iota(jnp.int32, x.shape, 0) == 0, carry_s[...], prev)
        carry_s[...] = x[tb - 1:tb, :]
        return x + (prev - x) * mu_ref[...]

    xr = shift(r_ref, pr_s, mur_ref)
    xk = shift(k_ref, pk_s, muk_ref)
    xv = shift(v_ref, pv_s, muv_ref)
    xl = shift(l_ref, pl_s, mul_ref)

    hr = lax.broadcasted_iota(jnp.int32, (LANES, LANES), 0) // HEAD_DIM
    hc = lax.broadcasted_iota(jnp.int32, (LANES, LANES), 1) // HEAD_DIM
    head_ones = jnp.where(hr == hc, 1.0, 0.0).astype(BF16)
    head_sum = lambda x: _mm_exact_rhs(x, head_ones)

    log_w = -DECAY_SCALE * _sigmoid(w0_ref[...] + _mm3(jnp.tanh(xl), wd_ref[...]))
    a = _sigmoid(a0_ref[...] + _mm3(xl, wa_ref[...]))
    g_s[...] = _mm3(_sigmoid(xl), wg_ref[...])
    kk = xk * kkw_ref[...]
    kk = kk / jnp.maximum(jnp.sqrt(head_sum(kk * kk)), 1e-12)
    k2 = xk * (1.0 + (a - 1.0) * kaw_ref[...])
    b = kk * a
    if t_valid < tb:
        valid = lax.broadcasted_iota(jnp.int32, (tb, LANES), 0) < t_valid
        log_w = jnp.where(valid, log_w, 0.0)
        kk = jnp.where(valid, kk, 0.0)
        b = jnp.where(valid, b, 0.0)
        k2 = jnp.where(valid, k2, 0.0)
        xv = jnp.where(valid, xv, 0.0)
    r_s[...] = xr
    k_s[...] = k2
    v_s[...] = xv
    kk_s[...] = kk
    b_s[...] = b
    lw_s[...] = log_w

    e2 = 2 * CHUNK
    first = lax.broadcasted_iota(jnp.int32, (CHUNK, LANES), 1) < HEAD_DIM
    ci = lax.broadcasted_iota(jnp.int32, (CHUNK, CHUNK), 0)
    cj = lax.broadcasted_iota(jnp.int32, (CHUNK, CHUNK), 1)
    tri = jnp.where(cj <= ci, 1.0, 0.0).astype(BF16)
    ei = lax.broadcasted_iota(jnp.int32, (e2, e2), 0)
    ej = lax.broadcasted_iota(jnp.int32, (e2, e2), 1)
    strict = ej < ei
    incl = ej <= ei
    eye = ei == ej

    def chunk(c, _):
        rows = pl.ds(pl.multiple_of(c * CHUNK, CHUNK), CHUNK)
        lw_c = lw_s[rows, :]
        cum = _mm_exact_lhs(tri, lw_c)
        tot = cum[CHUNK - 1:CHUNK, :]
        e_neg = jnp.exp(-cum)
        e_tail = jnp.exp(tot - cum)
        kk_c, b_c, k_c = kk_s[rows, :], b_s[rows, :], k_s[rows, :]
        kkt = _expand(kk_c * jnp.exp(cum - lw_c), first)
        rt = _expand(r_s[rows, :] * jnp.exp(cum), first)
        kh = _expand(k_c * e_neg, first)
        bh = _expand(b_c * e_neg, first)
        kp = _expand(k_c * e_tail, first)
        bp = _expand(b_c * e_tail, first)
        ve = _expand(v_s[rows, :], first)

        gram = _mm3_nt(jnp.concatenate([kkt, rt], axis=0), jnp.concatenate([kh, bh], axis=0))
        ak = jnp.where(strict, gram[:e2, :e2], 0.0)
        ab = jnp.where(strict, gram[:e2, e2:], 0.0)
        rk = jnp.where(incl, gram[e2:, :e2], 0.0)
        rb = jnp.where(incl, gram[e2:, e2:], 0.0)

        tinv = jnp.where(eye, 1.0, 0.0) - ab
        pw = _mm3(ab, ab)
        n_sq = CHUNK.bit_length() - 2
        for s in range(n_sq):
            tinv = tinv + _mm3(tinv, pw)
            if s + 1 < n_sq:
                pw = _mm3(pw, pw)

        wu = _mm3(tinv, jnp.concatenate([kkt, _mm3(ak, ve)], axis=1))
        rwu = _mm3(rb, wu)
        bwu = _mm3(bp.T, wu)
        q_eff = rt - rwu[:, :LANES]
        y_loc = _mm3(rk, ve) - rwu[:, LANES:]
        phi = jnp.where(eye, jnp.exp(tot), 0.0) - bwu[:, :LANES]
        psi = _mm3(kp.T, ve) - bwu[:, LANES:]

        z = z_s[...]
        ye = _mm3(q_eff, z) + y_loc
        y_s[rows, :] = ye[:CHUNK] + ye[CHUNK:]
        z_s[...] = _mm3(phi, z) + psi
        return 0

    lax.fori_loop(0, tb // CHUNK, chunk, 0)

    y = y_s[...]
    d = y - head_sum(y) * (1.0 / HEAD_DIM)
    var = head_sum(d * d) * (1.0 / HEAD_DIM)
    o = d * lax.rsqrt(var + GN_EPS) * lnw_ref[...] + lnb_ref[...]
    bonus = head_sum(r_s[...] * k_s[...] * rkw_ref[...]) * v_s[...]
    o_ref[0] = (o + bonus) * g_s[...]

    @pl.when(t == pl.num_programs(2) - 1)
    def _():
        zout_ref[0, 0] = z_s[...]


def _rwkv(rkv, lora, shift_rkv, shift_lora, z0, prm, *, tb, t_valid):
    b, t, _ = rkv.shape
    nt = t // tb
    pairs = RW_WIDTH // LANES
    seg = lambda off: pl.BlockSpec((1, tb, LANES), lambda bi, p, ti, off=off: (bi, ti, off + p))
    seg0 = lambda off: pl.BlockSpec((1, 1, LANES), lambda bi, p, ti, off=off: (bi, 0, off + p))
    vec = lambda off=0: pl.BlockSpec((1, LANES), lambda bi, p, ti, off=off: (0, off + p))
    mat = pl.BlockSpec((LORA_PAD, LANES), lambda bi, p, ti: (0, p))
    state = pl.BlockSpec((1, 1, LANES, LANES), lambda bi, p, ti: (bi, p, 0, 0))
    rows = lambda w: pltpu.VMEM((tb, w), F32)
    return pl.pallas_call(
        functools.partial(_rwkv_kernel, tb=tb, t_valid=t_valid),
        grid=(b, pairs, nt),
        in_specs=[seg(0), seg(pairs), seg(2 * pairs),
                  pl.BlockSpec((1, tb, LORA_PAD), lambda bi, p, ti: (bi, ti, 0)),
                  seg0(0), seg0(pairs), seg0(2 * pairs),
                  pl.BlockSpec((1, 1, LORA_PAD), lambda bi, p, ti: (bi, 0, 0)),
                  state,
                  vec(0), vec(pairs), vec(2 * pairs),
                  pl.BlockSpec((1, LORA_PAD), lambda bi, p, ti: (0, 0)),
                  vec(), mat, vec(), mat, mat, vec(), vec(), vec(), vec(), vec()],
        out_specs=[pl.BlockSpec((1, tb, LANES), lambda bi, p, ti: (bi, ti, p)), state],
        out_shape=[jax.ShapeDtypeStruct((b, t, RW_WIDTH), F32),
                   jax.ShapeDtypeStruct((b, pairs, LANES, LANES), F32)],
        scratch_shapes=[pltpu.VMEM((1, LANES), F32), pltpu.VMEM((1, LANES), F32),
                        pltpu.VMEM((1, LANES), F32), pltpu.VMEM((1, LORA_PAD), F32),
                        rows(LANES), rows(LANES), rows(LANES), rows(LANES), rows(LANES), rows(LANES),
                        rows(LANES), rows(LANES), pltpu.VMEM((LANES, LANES), F32)],
        compiler_params=pltpu.CompilerParams(dimension_semantics=("arbitrary",) * 3,
                                             vmem_limit_bytes=VMEM_LIMIT),
        name="rwkv7",
    )(rkv, rkv, rkv, lora, shift_rkv, shift_rkv, shift_rkv, shift_lora, z0,
      prm["mu_rkv"], prm["mu_rkv"], prm["mu_rkv"], prm["mu_lora"],
      prm["w0"], prm["wd"], prm["a0"], prm["wa"], prm["wg"],
      prm["k_k"], prm["k_a"], prm["r_k"], prm["ln_w"], prm["ln_b"])


def _out_ffn_kernel(x_ref, sb_ref, rw_ref, woa_ref, wob_ref, gf_ref, wg_ref, wu_ref, wd_ref, gl_ref,
                    y_ref):
    x1 = x_ref[...] + _dot(sb_ref[...].astype(BF16), woa_ref[...]) \
        + _dot(rw_ref[...].astype(BF16), wob_ref[...])
    h2 = _rmsnorm(x1, gf_ref[...]).astype(BF16)
    gate = _dot(h2, wg_ref[...])
    act = gate * _sigmoid(gate) * _dot(h2, wu_ref[...])
    x2 = x1 + _dot(act.astype(BF16), wd_ref[...])
    y_ref[...] = _rmsnorm(x2, gl_ref[...])


def _out_ffn(x2d, o_sb, o_rw, woa, wob, gf, wg, wu, wd, gl, tm):
    m, d = x2d.shape
    row = lambda w: pl.BlockSpec((tm, w), lambda i: (i, 0))
    return pl.pallas_call(
        _out_ffn_kernel,
        grid=(m // tm,),
        in_specs=[row(d), row(SB_WIDTH), row(RW_WIDTH), _const_spec(woa.shape), _const_spec(wob.shape),
                  _const_spec((1, d)), _const_spec(wg.shape), _const_spec(wu.shape),
                  _const_spec(wd.shape), _const_spec((1, d))],
        out_specs=row(d),
        out_shape=jax.ShapeDtypeStruct((m, d), F32),
        compiler_params=pltpu.CompilerParams(dimension_semantics=("arbitrary",),
                                             vmem_limit_bytes=VMEM_LIMIT),
        name="out_ffn",
    )(x2d, o_sb, o_rw, woa, wob, gf, wg, wu, wd, gl)


def _pad_rows(x, rows):
    return jnp.pad(x, ((0, 0), (0, rows - x.shape[1]), (0, 0)))


def _pad_lanes(x, lanes):
    return jnp.pad(x, [(0, 0)] * (x.ndim - 1) + [(0, lanes - x.shape[-1])])


def _state_to_blockdiag(wkv):
    b, h = wkv.shape[:2]
    zt = jnp.swapaxes(wkv, -1, -2).reshape(b, h // 2, 2, HEAD_DIM, HEAD_DIM)
    z = jnp.einsum("bpikv,ij->bpikjv", zt, jnp.eye(2, dtype=wkv.dtype))
    return z.reshape(b, h // 2, LANES, LANES)


def _blockdiag_to_state(z):
    b, p = z.shape[:2]
    z6 = z.reshape(b, p, 2, HEAD_DIM, 2, HEAD_DIM)
    zd = jnp.stack([z6[:, :, 0, :, 0, :], z6[:, :, 1, :, 1, :]], axis=2)
    return jnp.swapaxes(zd, -1, -2).reshape(b, 2 * p, HEAD_DIM, HEAD_DIM)


def _layer(x, k_past, v_past, wkv0, shift0, w, *, tm, tq, tb):
    b, t, d = x.shape
    x2d = x.reshape(b * t, d)
    q, k, v, rkv, lora = _norm_proj(x2d, w["norm_mix_g"], w["w_qkv"], w["w_rkv"], w["w_lora"], tm)
    q3, k3, v3 = (a.reshape(b, t, SB_WIDTH) for a in (q, k, v))
    if k_past is None:
        o_sb = _sb_attention(q3, k3, v3, k3, v3, tq=tq, causal_prefix=True)
    else:
        kd, vd = _pad_rows(k3, KEY_BLOCK), _pad_rows(v3, KEY_BLOCK)
        past = k_past.shape[1]
        o_sb = _sb_attention(q3, kd, vd, k_past.reshape(b, past, SB_WIDTH),
                             v_past.reshape(b, past, SB_WIDTH), tq=tq, causal_prefix=False)

    rkv3 = rkv.reshape(b, t, 3 * RW_WIDTH)
    lora3 = lora.reshape(b, t, LORA_PAD)
    t_pad = -(-t // tb) * tb
    o_rw, z_out = _rwkv(_pad_rows(rkv3, t_pad), _pad_rows(lora3, t_pad),
                        shift0[..., :3 * RW_WIDTH], _pad_lanes(shift0[..., 3 * RW_WIDTH:], LORA_PAD),
                        _state_to_blockdiag(wkv0), w, tb=tb, t_valid=min(t, tb))
    o_rw = o_rw[:, :t]
    shift_last = jnp.concatenate([rkv3[:, t - 1:, :], lora3[:, t - 1:, :LORA_WIDTH]], axis=-1)

    y = _out_ffn(x2d, o_sb.reshape(b * t, SB_WIDTH), o_rw.reshape(b * t, RW_WIDTH),
                 w["w_out_sb"], w["w_out_rw"], w["norm_ffn_g"], w["w_gate"], w["w_up"], w["w_down"],
                 w["norm_final_g"], tm)
    heads = SB_WIDTH // HEAD_DIM
    return (y.reshape(b, t, d), k3.reshape(b, t, heads, HEAD_DIM), v3.reshape(b, t, heads, HEAD_DIM),
            _blockdiag_to_state(z_out), shift_last)


def kernel(x_prompt, x_sample, cache_k, cache_v, state_wkv, state_shift, norm_mix_g, w_in, mu_shift, w0,
           w_decay_up, a0, w_aaa_up, w_gate_up, k_k, k_a, r_k, ln_x_w, ln_x_b, w_out, norm_ffn_g, w_gate,
           w_up, w_down, norm_final_g):
    assert w_in.shape[0] == 1, "single-layer trunk"
    l = 0
    rw3 = 3 * RW_WIDTH
    lora_rows = lambda m, lo, hi: jnp.pad(m, ((lo, LORA_PAD - hi), (0, 0)))
    w = {
        "norm_mix_g": norm_mix_g[l][None, :],
        "w_qkv": w_in[l][:, :3 * SB_WIDTH].astype(BF16),
        "w_rkv": w_in[l][:, 3 * SB_WIDTH:3 * SB_WIDTH + rw3].astype(BF16),
        "w_lora": _pad_lanes(w_in[l][:, 3 * SB_WIDTH + rw3:], LORA_PAD).astype(BF16),
        "mu_rkv": mu_shift[l][None, :rw3],
        "mu_lora": _pad_lanes(mu_shift[l][None, rw3:], LORA_PAD),
        "w0": w0[l][None, :],
        "wd": lora_rows(w_decay_up[l], 0, LORA_DECAY),
        "a0": a0[l][None, :],
        "wa": lora_rows(w_aaa_up[l], LORA_DECAY, LORA_DECAY + LORA_AAA),
        "wg": lora_rows(w_gate_up[l], LORA_DECAY + LORA_AAA, LORA_WIDTH),
        "k_k": k_k[l][None, :],
        "k_a": k_a[l][None, :],
        "r_k": r_k[l].reshape(1, RW_WIDTH),
        "ln_w": ln_x_w[l][None, :],
        "ln_b": ln_x_b[l][None, :],
        "w_out_sb": w_out[l][:SB_WIDTH].astype(BF16),
        "w_out_rw": w_out[l][SB_WIDTH:].astype(BF16),
        "norm_ffn_g": norm_ffn_g[l][None, :],
        "w_gate": w_gate[l].astype(BF16),
        "w_up": w_up[l].astype(BF16),
        "w_down": w_down[l].astype(BF16),
        "norm_final_g": norm_final_g[None, :],
    }
    bp = x_prompt.shape[0]
    heads = RW_WIDTH // HEAD_DIM
    wkv_zero = jnp.zeros((bp, heads, HEAD_DIM, HEAD_DIM), x_prompt.dtype)
    shift_zero = jnp.zeros((bp, 1, rw3 + LORA_WIDTH), x_prompt.dtype)
    yp, k1, v1, s1, sh1 = _layer(x_prompt, None, None, wkv_zero, shift_zero, w, tm=512, tq=128, tb=512)
    ys, k2, v2, s2, sh2 = _layer(x_sample, cache_k[l], cache_v[l], state_wkv[l], state_shift[l], w,
                                 tm=256, tq=x_sample.shape[1], tb=CHUNK)
    return (yp, ys, k1[None], v1[None], s1[None], sh1[None], k2[None], v2[None], s2[None], sh2[None])
```

```python
import functools
import math

import jax
import jax.numpy as jnp
from jax import lax
from jax.experimental import pallas as pl
from jax.experimental.pallas import tpu as pltpu

F32 = jnp.float32
BF16 = jnp.bfloat16

LANES = 128
HEAD_DIM = 64
SB_WIDTH = 512
RW_WIDTH = 512
LORA_DECAY = 64
LORA_AAA = 64
LORA_GATE = 160
LORA_WIDTH = LORA_DECAY + LORA_AAA + LORA_GATE
LORA_PAD = 384
RMS_EPS = 1e-6
GN_EPS = 64e-5
DECAY_SCALE = math.exp(-0.5)
KEY_BLOCK = 128
SB_DEAD_LOG = -104.0
CHUNK = 64
VMEM_LIMIT = 56 * 1024 * 1024


def _dot(a, b):
    return jnp.dot(a, b, preferred_element_type=F32)


def _dot_nt(a, b):
    return lax.dot_general(a, b, (((1,), (1,)), ((), ())), preferred_element_type=F32)


def _split2(x):
    hi = x.astype(BF16)
    lo = (x - hi.astype(F32)).astype(BF16)
    return hi, lo


def _split3(x):
    hi = x.astype(BF16)
    r1 = x - hi.astype(F32)
    mid = r1.astype(BF16)
    lo = (r1 - mid.astype(F32)).astype(BF16)
    return hi, mid, lo


def _mm3(a, b):
    ah, al = _split2(a)
    bh, bl = _split2(b)
    return _dot(ah, bh) + (_dot(ah, bl) + _dot(al, bh))


def _mm3_nt(a, b):
    ah, al = _split2(a)
    bh, bl = _split2(b)
    return _dot_nt(ah, bh) + (_dot_nt(ah, bl) + _dot_nt(al, bh))


def _bdot(a, b):
    return jnp.einsum("bij,bjk->bik", a, b, preferred_element_type=F32)


def _bdot_nt(a, b):
    return jnp.einsum("bik,bjk->bij", a, b, preferred_element_type=F32)


def _bmm3(a, b):
    ah, al = _split2(a)
    bh, bl = _split2(b)
    return _bdot(ah, bh) + (_bdot(ah, bl) + _bdot(al, bh))


def _bmm3_nt(a, b):
    ah, al = _split2(a)
    bh, bl = _split2(b)
    return _bdot_nt(ah, bh) + (_bdot_nt(ah, bl) + _bdot_nt(al, bh))


def _mm_exact_rhs(x, m):
    hi, mid, lo = _split3(x)
    return _dot(hi, m) + (_dot(mid, m) + _dot(lo, m))


def _mm_exact_lhs(m, x):
    hi, mid, lo = _split3(x)
    return _dot(m, hi) + (_dot(m, mid) + _dot(m, lo))


def _sigmoid(x):
    return 1.0 / (1.0 + jnp.exp(-x))


def _rmsnorm(x, g):
    return x * lax.rsqrt(jnp.mean(x * x, axis=-1, keepdims=True) + RMS_EPS) * g


def _const_spec(shape):
    return pl.BlockSpec(shape, lambda *_: (0,) * len(shape), pipeline_mode=pl.Buffered(1))


def _norm_proj_kernel(x_ref, g_ref, wqkv_ref, wrkv_ref, wl_ref, q_ref, k_ref, v_ref, rkv_ref, l_ref):
    hb = _rmsnorm(x_ref[...], g_ref[...]).astype(BF16)
    qkv = _dot(hb, wqkv_ref[...])
    q_ref[...] = qkv[:, :SB_WIDTH]
    k_ref[...] = qkv[:, SB_WIDTH:2 * SB_WIDTH]
    v_ref[...] = qkv[:, 2 * SB_WIDTH:]
    rkv_ref[...] = _dot(hb, wrkv_ref[...])
    l_ref[...] = _dot(hb, wl_ref[...])


def _norm_proj(x2d, g, wqkv, wrkv, wl, tm):
    m, d = x2d.shape
    row = lambda w: pl.BlockSpec((tm, w), lambda i: (i, 0))
    return pl.pallas_call(
        _norm_proj_kernel,
        grid=(m // tm,),
        in_specs=[row(d), _const_spec((1, d)), _const_spec(wqkv.shape), _const_spec(wrkv.shape),
                  _const_spec(wl.shape)],
        out_specs=[row(SB_WIDTH), row(SB_WIDTH), row(SB_WIDTH), row(3 * RW_WIDTH), row(LORA_PAD)],
        out_shape=[jax.ShapeDtypeStruct((m, w), F32)
                   for w in (SB_WIDTH, SB_WIDTH, SB_WIDTH, 3 * RW_WIDTH, LORA_PAD)],
        compiler_params=pltpu.CompilerParams(dimension_semantics=("arbitrary",),
                                             vmem_limit_bytes=VMEM_LIMIT),
        name="norm_proj",
    )(x2d, g, wqkv, wrkv, wl)


def _sb_kernel(q_ref, kd_ref, vd_ref, kp_ref, vp_ref, o_ref, *, tq, causal_prefix, n_prefix):
    q = q_ref[0] * (1.0 / math.sqrt(HEAD_DIM))
    first = lax.broadcasted_iota(jnp.int32, (tq, LANES), 1) < HEAD_DIM
    qe = jnp.concatenate([jnp.where(first, q, 0.0), jnp.where(first, 0.0, q)], axis=0).astype(BF16)

    row = lax.broadcasted_iota(jnp.int32, (2 * tq, KEY_BLOCK), 0)
    col = lax.broadcasted_iota(jnp.int32, (2 * tq, KEY_BLOCK), 1)
    causal = col < jnp.where(row >= tq, row - tq, row)

    jj = lax.broadcasted_iota(jnp.int32, (KEY_BLOCK, 2 * KEY_BLOCK), 0)
    ss = lax.broadcasted_iota(jnp.int32, (KEY_BLOCK, 2 * KEY_BLOCK), 1)
    suffix = jnp.where((jj > ss) | (ss >= KEY_BLOCK), 1.0, 0.0).astype(BF16)

    def block(kb, vb, c, o, mask):
        z = _dot_nt(qe, kb.astype(BF16))
        log1mb = -(jnp.maximum(z, 0.0) + jnp.log(1.0 + jnp.exp(-jnp.abs(z))))
        if mask is not None:
            log1mb = jnp.where(mask, log1mb, 0.0)
        hi, lo = _split2(log1mb)
        sums = _dot(hi, suffix) + _dot(lo, suffix)
        w = jnp.exp(z + log1mb + (sums[:, :KEY_BLOCK] + c))
        if mask is not None:
            w = jnp.where(mask, w, 0.0)
        o = o + _dot(w.astype(BF16), vb.astype(BF16))
        return c + sums[:, KEY_BLOCK:], o

    zeros = jnp.zeros((2 * tq, LANES), F32)
    c, o = block(kd_ref[0], vd_ref[0], zeros, zeros, causal)

    n = pl.program_id(2) if causal_prefix else n_prefix

    def live(carry):
        return (carry[0] < n) & (jnp.max(carry[1]) >= SB_DEAD_LOG)

    def body(carry):
        it, c, o = carry
        start = pl.multiple_of((n - 1 - it) * KEY_BLOCK, KEY_BLOCK)
        c, o = block(kp_ref[0, pl.ds(start, KEY_BLOCK), :], vp_ref[0, pl.ds(start, KEY_BLOCK), :],
                     c, o, None)
        return it + 1, c, o

    _, c, o = lax.while_loop(live, body, (jnp.int32(0), c, o))
    o_ref[0] = jnp.where(first, o[:tq], o[tq:])


def _sb_attention(q, kd, vd, kp, vp, *, tq, causal_prefix):
    b, t, _ = q.shape
    nq = t // tq
    tp = kp.shape[1]
    blk = lambda rows: pl.BlockSpec((1, rows, LANES), lambda bi, p, i: (bi, i, p))
    pre = pl.BlockSpec((1, tp, LANES), lambda bi, p, i: (bi, 0, p))
    return pl.pallas_call(
        functools.partial(_sb_kernel, tq=tq, causal_prefix=causal_prefix, n_prefix=tp // KEY_BLOCK),
        grid=(b, SB_WIDTH // LANES, nq),
        in_specs=[blk(tq), blk(KEY_BLOCK), blk(KEY_BLOCK), pre, pre],
        out_specs=blk(tq),
        out_shape=jax.ShapeDtypeStruct(q.shape, F32),
        compiler_params=pltpu.CompilerParams(dimension_semantics=("arbitrary",) * 3,
                                             vmem_limit_bytes=VMEM_LIMIT),
        name="stickbreak",
    )(q, kd, vd, kp, vp)


def _rwkv_kernel(r_ref, k_ref, v_ref, l_ref, sr_ref, sk_ref, sv_ref, sl_ref, z0_ref,
                 mur_ref, muk_ref, muv_ref, mul_ref, w0_ref, wd_ref, a0_ref, wa_ref, wg_ref,
                 kkw_ref, kaw_ref, rkw_ref, lnw_ref, lnb_ref,
                 o_ref, zout_ref,
                 pr_s, pk_s, pv_s, pl_s, y_s, z_s, qe_s, yl_s, phi_s, psi_s,
                 *, tb, t_valid):
    t = pl.program_id(2)

    @pl.when(t == 0)
    def _():
        pr_s[...] = sr_ref[0]
        pk_s[...] = sk_ref[0]
        pv_s[...] = sv_ref[0]
        pl_s[...] = sl_ref[0]
        z_s[...] = z0_ref[0, 0]

    def shift(x_ref, carry_s, mu_ref):
        x = x_ref[0]
        prev = pltpu.roll(x, 1, 0)
        prev = jnp.where(lax.broadcasted_iota(jnp.int32, x.shape, 0) == 0, carry_s[...], prev)
        carry_s[...] = x[tb - 1:tb, :]
        return x + (prev - x) * mu_ref[...]

    xr = shift(r_ref, pr_s, mur_ref)
    xk = shift(k_ref, pk_s, muk_ref)
    xv = shift(v_ref, pv_s, muv_ref)
    xl = shift(l_ref, pl_s, mul_ref)

    hr = lax.broadcasted_iota(jnp.int32, (LANES, LANES), 0) // HEAD_DIM
    hc = lax.broadcasted_iota(jnp.int32, (LANES, LANES), 1) // HEAD_DIM
    head_ones = jnp.where(hr == hc, 1.0, 0.0).astype(BF16)
    head_sum = lambda x: _mm_exact_rhs(x, head_ones)

    log_w = -DECAY_SCALE * _sigmoid(w0_ref[...] + _mm3(jnp.tanh(xl), wd_ref[...]))
    a = _sigmoid(a0_ref[...] + _mm3(xl, wa_ref[...]))
    kk = xk * kkw_ref[...]
    kk = kk / jnp.maximum(jnp.sqrt(head_sum(kk * kk)), 1e-12)
    k2 = xk * (1.0 + (a - 1.0) * kaw_ref[...])
    b = kk * a
    if t_valid < tb:
        valid = lax.broadcasted_iota(jnp.int32, (tb, LANES), 0) < t_valid
        log_w = jnp.where(valid, log_w, 0.0)
        kk = jnp.where(valid, kk, 0.0)
        b = jnp.where(valid, b, 0.0)
        k2 = jnp.where(valid, k2, 0.0)
        xv = jnp.where(valid, xv, 0.0)
    nb = tb // CHUNK
    e2 = 2 * CHUNK
    ti = lax.broadcasted_iota(jnp.int32, (tb, tb), 0)
    tj = lax.broadcasted_iota(jnp.int32, (tb, tb), 1)
    chunk_tri = jnp.where((tj <= ti) & (ti // CHUNK == tj // CHUNK), 1.0, 0.0).astype(BF16)
    cum = _mm_exact_lhs(chunk_tri, log_w)
    by_chunk = lambda x: x.reshape(nb, CHUNK, LANES)
    cum3, lw3 = by_chunk(cum), by_chunk(log_w)
    tot = cum3[:, CHUNK - 1:CHUNK, :]
    first = lax.broadcasted_iota(jnp.int32, (nb, CHUNK, LANES), 2) < HEAD_DIM
    expand = lambda x: jnp.concatenate([jnp.where(first, x, 0.0), jnp.where(first, 0.0, x)], axis=1)
    e_neg = jnp.exp(-cum3)
    e_tail = jnp.exp(tot - cum3)
    kk3, b3, k3 = by_chunk(kk), by_chunk(b), by_chunk(k2)
    kkt = expand(kk3 * jnp.exp(cum3 - lw3))
    rt = expand(by_chunk(xr) * jnp.exp(cum3))
    kh = expand(k3 * e_neg)
    bh = expand(b3 * e_neg)
    kp = expand(k3 * e_tail)
    bp = expand(b3 * e_tail)
    ve = expand(by_chunk(xv))

    ei = lax.broadcasted_iota(jnp.int32, (nb, e2, e2), 1)
    ej = lax.broadcasted_iota(jnp.int32, (nb, e2, e2), 2)
    strict = ej < ei
    incl = ej <= ei
    eye = ei == ej

    gram = _bmm3_nt(jnp.concatenate([kkt, rt], axis=1), jnp.concatenate([kh, bh], axis=1))
    ak = jnp.where(strict, gram[:, :e2, :e2], 0.0)
    ab = jnp.where(strict, gram[:, :e2, e2:], 0.0)
    rk = jnp.where(incl, gram[:, e2:, :e2], 0.0)
    rb = jnp.where(incl, gram[:, e2:, e2:], 0.0)

    tinv = jnp.where(eye, 1.0, 0.0) - ab
    pw = _bmm3(ab, ab)
    n_sq = CHUNK.bit_length() - 2
    for s in range(n_sq):
        tinv = tinv + _bmm3(tinv, pw)
        if s + 1 < n_sq:
            pw = _bmm3(pw, pw)

    wu = _bmm3(tinv, jnp.concatenate([kkt, _bmm3(ak, ve)], axis=2))
    rwu = _bmm3(rb, wu)
    bwu = _bmm3(jnp.swapaxes(bp, 1, 2), wu)
    qe_s[...] = rt - rwu[:, :, :LANES]
    yl_s[...] = _bmm3(rk, ve) - rwu[:, :, LANES:]
    phi_s[...] = jnp.where(eye, jnp.exp(tot), 0.0) - bwu[:, :, :LANES]
    psi_s[...] = _bmm3(jnp.swapaxes(kp, 1, 2), ve) - bwu[:, :, LANES:]

    def advance(c, _):
        rows = pl.ds(pl.multiple_of(c * CHUNK, CHUNK), CHUNK)
        z = z_s[...]
        ye = _mm3(qe_s[c], z) + yl_s[c]
        y_s[rows, :] = ye[:CHUNK] + ye[CHUNK:]
        z_s[...] = _mm3(phi_s[c], z) + psi_s[c]
        return 0

    lax.fori_loop(0, nb, advance, 0, unroll=min(nb, 2))

    y = y_s[...]
    d = y - head_sum(y) * (1.0 / HEAD_DIM)
    var = head_sum(d * d) * (1.0 / HEAD_DIM)
    o = d * lax.rsqrt(var + GN_EPS) * lnw_ref[...] + lnb_ref[...]
    bonus = head_sum(xr * k2 * rkw_ref[...]) * xv
    o_ref[0] = (o + bonus) * _mm3(_sigmoid(xl), wg_ref[...])

    @pl.when(t == pl.num_programs(2) - 1)
    def _():
        zout_ref[0, 0] = z_s[...]


def _rwkv(rkv, lora, shift_rkv, shift_lora, z0, prm, *, tb, t_valid):
    b, t, _ = rkv.shape
    nt = t // tb
    pairs = RW_WIDTH // LANES
    seg = lambda off: pl.BlockSpec((1, tb, LANES), lambda bi, p, ti, off=off: (bi, ti, off + p))
    seg0 = lambda off: pl.BlockSpec((1, 1, LANES), lambda bi, p, ti, off=off: (bi, 0, off + p))
    vec = lambda off=0: pl.BlockSpec((1, LANES), lambda bi, p, ti, off=off: (0, off + p))
    mat = pl.BlockSpec((LORA_PAD, LANES), lambda bi, p, ti: (0, p))
    state = pl.BlockSpec((1, 1, LANES, LANES), lambda bi, p, ti: (bi, p, 0, 0))
    rows = lambda w: pltpu.VMEM((tb, w), F32)
    return pl.pallas_call(
        functools.partial(_rwkv_kernel, tb=tb, t_valid=t_valid),
        grid=(b, pairs, nt),
        in_specs=[seg(0), seg(pairs), seg(2 * pairs),
                  pl.BlockSpec((1, tb, LORA_PAD), lambda bi, p, ti: (bi, ti, 0)),
                  seg0(0), seg0(pairs), seg0(2 * pairs),
                  pl.BlockSpec((1, 1, LORA_PAD), lambda bi, p, ti: (bi, 0, 0)),
                  state,
                  vec(0), vec(pairs), vec(2 * pairs),
                  pl.BlockSpec((1, LORA_PAD), lambda bi, p, ti: (0, 0)),
                  vec(), mat, vec(), mat, mat, vec(), vec(), vec(), vec(), vec()],
        out_specs=[pl.BlockSpec((1, tb, LANES), lambda bi, p, ti: (bi, ti, p)), state],
        out_shape=[jax.ShapeDtypeStruct((b, t, RW_WIDTH), F32),
                   jax.ShapeDtypeStruct((b, pairs, LANES, LANES), F32)],
        scratch_shapes=[pltpu.VMEM((1, LANES), F32), pltpu.VMEM((1, LANES), F32),
                        pltpu.VMEM((1, LANES), F32), pltpu.VMEM((1, LORA_PAD), F32),
                        pltpu.VMEM((tb, LANES), F32), pltpu.VMEM((LANES, LANES), F32)]
        + [pltpu.VMEM((tb // CHUNK, LANES, LANES), F32)] * 4,
        compiler_params=pltpu.CompilerParams(dimension_semantics=("arbitrary",) * 3,
                                             vmem_limit_bytes=VMEM_LIMIT),
        name="rwkv7",
    )(rkv, rkv, rkv, lora, shift_rkv, shift_rkv, shift_rkv, shift_lora, z0,
      prm["mu_rkv"], prm["mu_rkv"], prm["mu_rkv"], prm["mu_lora"],
      prm["w0"], prm["wd"], prm["a0"], prm["wa"], prm["wg"],
      prm["k_k"], prm["k_a"], prm["r_k"], prm["ln_w"], prm["ln_b"])


def _out_ffn_kernel(x_ref, sb_ref, rw_ref, woa_ref, wob_ref, gf_ref, wg_ref, wu_ref, wd_ref, gl_ref,
                    y_ref):
    x1 = x_ref[...] + _dot(sb_ref[...].astype(BF16), woa_ref[...]) \
        + _dot(rw_ref[...].astype(BF16), wob_ref[...])
    h2 = _rmsnorm(x1, gf_ref[...]).astype(BF16)
    gate = _dot(h2, wg_ref[...])
    act = gate * _sigmoid(gate) * _dot(h2, wu_ref[...])
    x2 = x1 + _dot(act.astype(BF16), wd_ref[...])
    y_ref[...] = _rmsnorm(x2, gl_ref[...])


def _out_ffn(x2d, o_sb, o_rw, woa, wob, gf, wg, wu, wd, gl, tm):
    m, d = x2d.shape
    row = lambda w: pl.BlockSpec((tm, w), lambda i: (i, 0))
    return pl.pallas_call(
        _out_ffn_kernel,
        grid=(m // tm,),
        in_specs=[row(d), row(SB_WIDTH), row(RW_WIDTH), _const_spec(woa.shape), _const_spec(wob.shape),
                  _const_spec((1, d)), _const_spec(wg.shape), _const_spec(wu.shape),
                  _const_spec(wd.shape), _const_spec((1, d))],
        out_specs=row(d),
        out_shape=jax.ShapeDtypeStruct((m, d), F32),
        compiler_params=pltpu.CompilerParams(dimension_semantics=("arbitrary",),
                                             vmem_limit_bytes=VMEM_LIMIT),
        name="out_ffn",
    )(x2d, o_sb, o_rw, woa, wob, gf, wg, wu, wd, gl)


def _pad_rows(x, rows):
    return jnp.pad(x, ((0, 0), (0, rows - x.shape[1]), (0, 0)))


def _pad_lanes(x, lanes):
    return jnp.pad(x, [(0, 0)] * (x.ndim - 1) + [(0, lanes - x.shape[-1])])


def _state_to_blockdiag(wkv):
    b, h = wkv.shape[:2]
    zt = jnp.swapaxes(wkv, -1, -2).reshape(b, h // 2, 2, HEAD_DIM, HEAD_DIM)
    z = jnp.einsum("bpikv,ij->bpikjv", zt, jnp.eye(2, dtype=wkv.dtype))
    return z.reshape(b, h // 2, LANES, LANES)


def _blockdiag_to_state(z):
    b, p = z.shape[:2]
    z6 = z.reshape(b, p, 2, HEAD_DIM, 2, HEAD_DIM)
    zd = jnp.stack([z6[:, :, 0, :, 0, :], z6[:, :, 1, :, 1, :]], axis=2)
    return jnp.swapaxes(zd, -1, -2).reshape(b, 2 * p, HEAD_DIM, HEAD_DIM)


def _layer(x, k_past, v_past, wkv0, shift0, w, *, tm, tq, tb):
    b, t, d = x.shape
    x2d = x.reshape(b * t, d)
    q, k, v, rkv, lora = _norm_proj(x2d, w["norm_mix_g"], w["w_qkv"], w["w_rkv"], w["w_lora"], tm)
    q3, k3, v3 = (a.reshape(b, t, SB_WIDTH) for a in (q, k, v))
    if k_past is None:
        o_sb = _sb_attention(q3, k3, v3, k3, v3, tq=tq, causal_prefix=True)
    else:
        kd, vd = _pad_rows(k3, KEY_BLOCK), _pad_rows(v3, KEY_BLOCK)
        past = k_past.shape[1]
        o_sb = _sb_attention(q3, kd, vd, k_past.reshape(b, past, SB_WIDTH),
                             v_past.reshape(b, past, SB_WIDTH), tq=tq, causal_prefix=False)

    rkv3 = rkv.reshape(b, t, 3 * RW_WIDTH)
    lora3 = lora.reshape(b, t, LORA_PAD)
    t_pad = -(-t // tb) * tb
    o_rw, z_out = _rwkv(_pad_rows(rkv3, t_pad), _pad_rows(lora3, t_pad),
                        shift0[..., :3 * RW_WIDTH], _pad_lanes(shift0[..., 3 * RW_WIDTH:], LORA_PAD),
                        _state_to_blockdiag(wkv0), w, tb=tb, t_valid=min(t, tb))
    o_rw = o_rw[:, :t]
    shift_last = jnp.concatenate([rkv3[:, t - 1:, :], lora3[:, t - 1:, :LORA_WIDTH]], axis=-1)

    y = _out_ffn(x2d, o_sb.reshape(b * t, SB_WIDTH), o_rw.reshape(b * t, RW_WIDTH),
                 w["w_out_sb"], w["w_out_rw"], w["norm_ffn_g"], w["w_gate"], w["w_up"], w["w_down"],
                 w["norm_final_g"], tm)
    heads = SB_WIDTH // HEAD_DIM
    return (y.reshape(b, t, d), k3.reshape(b, t, heads, HEAD_DIM), v3.reshape(b, t, heads, HEAD_DIM),
            _blockdiag_to_state(z_out), shift_last)


def kernel(x_prompt, x_sample, cache_k, cache_v, state_wkv, state_shift, norm_mix_g, w_in, mu_shift, w0,
           w_decay_up, a0, w_aaa_up, w_gate_up, k_k, k_a, r_k, ln_x_w, ln_x_b, w_out, norm_ffn_g, w_gate,
           w_up, w_down, norm_final_g):
    assert w_in.shape[0] == 1, "single-layer trunk"
    l = 0
    rw3 = 3 * RW_WIDTH
    lora_rows = lambda m, lo, hi: jnp.pad(m, ((lo, LORA_PAD - hi), (0, 0)))
    w = {
        "norm_mix_g": norm_mix_g[l][None, :],
        "w_qkv": w_in[l][:, :3 * SB_WIDTH].astype(BF16),
        "w_rkv": w_in[l][:, 3 * SB_WIDTH:3 * SB_WIDTH + rw3].astype(BF16),
        "w_lora": _pad_lanes(w_in[l][:, 3 * SB_WIDTH + rw3:], LORA_PAD).astype(BF16),
        "mu_rkv": mu_shift[l][None, :rw3],
        "mu_lora": _pad_lanes(mu_shift[l][None, rw3:], LORA_PAD),
        "w0": w0[l][None, :],
        "wd": lora_rows(w_decay_up[l], 0, LORA_DECAY),
        "a0": a0[l][None, :],
        "wa": lora_rows(w_aaa_up[l], LORA_DECAY, LORA_DECAY + LORA_AAA),
        "wg": lora_rows(w_gate_up[l], LORA_DECAY + LORA_AAA, LORA_WIDTH),
        "k_k": k_k[l][None, :],
        "k_a": k_a[l][None, :],
        "r_k": r_k[l].reshape(1, RW_WIDTH),
        "ln_w": ln_x_w[l][None, :],
        "ln_b": ln_x_b[l][None, :],
        "w_out_sb": w_out[l][:SB_WIDTH].astype(BF16),
        "w_out_rw": w_out[l][SB_WIDTH:].astype(BF16),
        "norm_ffn_g": norm_ffn_g[l][None, :],
        "w_gate": w_gate[l].astype(BF16),
        "w_up": w_up[l].astype(BF16),
        "w_down": w_down[l].astype(BF16),
        "norm_final_g": norm_final_g[None, :],
    }
    bp = x_prompt.shape[0]
    heads = RW_WIDTH // HEAD_DIM
    wkv_zero = jnp.zeros((bp, heads, HEAD_DIM, HEAD_DIM), x_prompt.dtype)
    shift_zero = jnp.zeros((bp, 1, rw3 + LORA_WIDTH), x_prompt.dtype)
    yp, k1, v1, s1, sh1 = _layer(x_prompt, None, None, wkv_zero, shift_zero, w, tm=512, tq=128, tb=512)
    ys, k2, v2, s2, sh2 = _layer(x_sample, cache_k[l], cache_v[l], state_wkv[l], state_shift[l], w,
                                 tm=256, tq=x_sample.shape[1], tb=CHUNK)
    return (yp, ys, k1[None], v1[None], s1[None], sh1[None], k2[None], v2[None], s2[None], sh2[None])
```

```python
import functools
import math

import jax
import jax.numpy as jnp
from jax import lax
from jax.experimental import pallas as pl
from jax.experimental.pallas import tpu as pltpu

F32 = jnp.float32
BF16 = jnp.bfloat16

LANES = 128
HEAD_DIM = 64
SB_WIDTH = 512
RW_WIDTH = 512
LORA_DECAY = 64
LORA_AAA = 64
LORA_GATE = 160
LORA_WIDTH = LORA_DECAY + LORA_AAA + LORA_GATE
LORA_PAD = 384
RMS_EPS = 1e-6
GN_EPS = 64e-5
DECAY_SCALE = math.exp(-0.5)
KEY_BLOCK = 128
SB_DEAD_LOG = -104.0
CHUNK = 64
VMEM_LIMIT = 56 * 1024 * 1024


def _dot(a, b):
    return jnp.dot(a, b, preferred_element_type=F32)


def _dot_nt(a, b):
    return lax.dot_general(a, b, (((1,), (1,)), ((), ())), preferred_element_type=F32)


def _split2(x):
    hi = x.astype(BF16)
    lo = (x - hi.astype(F32)).astype(BF16)
    return hi, lo


def _split3(x):
    hi = x.astype(BF16)
    r1 = x - hi.astype(F32)
    mid = r1.astype(BF16)
    lo = (r1 - mid.astype(F32)).astype(BF16)
    return hi, mid, lo


def _mm3(a, b):
    ah, al = _split2(a)
    bh, bl = _split2(b)
    return _dot(ah, bh) + (_dot(ah, bl) + _dot(al, bh))


def _mm3_nt(a, b):
    ah, al = _split2(a)
    bh, bl = _split2(b)
    return _dot_nt(ah, bh) + (_dot_nt(ah, bl) + _dot_nt(al, bh))


def _bdot(a, b):
    return jnp.einsum("bij,bjk->bik", a, b, preferred_element_type=F32)


def _bdot_nt(a, b):
    return jnp.einsum("bik,bjk->bij", a, b, preferred_element_type=F32)


def _bmm3(a, b):
    ah, al = _split2(a)
    bh, bl = _split2(b)
    return _bdot(ah, bh) + (_bdot(ah, bl) + _bdot(al, bh))


def _bmm3_nt(a, b):
    ah, al = _split2(a)
    bh, bl = _split2(b)
    return _bdot_nt(ah, bh) + (_bdot_nt(ah, bl) + _bdot_nt(al, bh))


def _mm_exact_rhs(x, m):
    hi, mid, lo = _split3(x)
    return _dot(hi, m) + (_dot(mid, m) + _dot(lo, m))


def _mm_exact_lhs(m, x):
    hi, mid, lo = _split3(x)
    return _dot(m, hi) + (_dot(m, mid) + _dot(m, lo))


def _sigmoid(x):
    return 1.0 / (1.0 + jnp.exp(-x))


def _rmsnorm(x, g):
    return x * lax.rsqrt(jnp.mean(x * x, axis=-1, keepdims=True) + RMS_EPS) * g


def _const_spec(shape):
    return pl.BlockSpec(shape, lambda *_: (0,) * len(shape), pipeline_mode=pl.Buffered(1))


def _norm_proj_kernel(x_ref, g_ref, wqkv_ref, wrkv_ref, wl_ref, q_ref, k_ref, v_ref, rkv_ref, l_ref):
    hb = _rmsnorm(x_ref[...], g_ref[...]).astype(BF16)
    qkv = _dot(hb, wqkv_ref[...])
    q_ref[...] = qkv[:, :SB_WIDTH]
    k_ref[...] = qkv[:, SB_WIDTH:2 * SB_WIDTH]
    v_ref[...] = qkv[:, 2 * SB_WIDTH:]
    rkv_ref[...] = _dot(hb, wrkv_ref[...])
    l_ref[...] = _dot(hb, wl_ref[...])


def _norm_proj(x2d, g, wqkv, wrkv, wl, tm):
    m, d = x2d.shape
    row = lambda w: pl.BlockSpec((tm, w), lambda i: (i, 0))
    return pl.pallas_call(
        _norm_proj_kernel,
        grid=(m // tm,),
        in_specs=[row(d), _const_spec((1, d)), _const_spec(wqkv.shape), _const_spec(wrkv.shape),
                  _const_spec(wl.shape)],
        out_specs=[row(SB_WIDTH), row(SB_WIDTH), row(SB_WIDTH), row(3 * RW_WIDTH), row(LORA_PAD)],
        out_shape=[jax.ShapeDtypeStruct((m, w), F32)
                   for w in (SB_WIDTH, SB_WIDTH, SB_WIDTH, 3 * RW_WIDTH, LORA_PAD)],
        compiler_params=pltpu.CompilerParams(dimension_semantics=("arbitrary",),
                                             vmem_limit_bytes=VMEM_LIMIT),
        name="norm_proj",
    )(x2d, g, wqkv, wrkv, wl)


def _sb_kernel(q_ref, kd_ref, vd_ref, kp_ref, vp_ref, o_ref, *, tq, causal_prefix, n_prefix):
    pairs = SB_WIDTH // LANES
    by_pair = lambda x: jnp.stack([x[:, p * LANES:(p + 1) * LANES] for p in range(pairs)])
    q = by_pair(q_ref[0] * (1.0 / math.sqrt(HEAD_DIM)))
    first = lax.broadcasted_iota(jnp.int32, (pairs, tq, LANES), 2) < HEAD_DIM
    qe = jnp.concatenate([jnp.where(first, q, 0.0), jnp.where(first, 0.0, q)], axis=1).astype(BF16)

    row = lax.broadcasted_iota(jnp.int32, (pairs, 2 * tq, KEY_BLOCK), 1)
    col = lax.broadcasted_iota(jnp.int32, (pairs, 2 * tq, KEY_BLOCK), 2)
    causal = col < jnp.where(row >= tq, row - tq, row)

    jj = lax.broadcasted_iota(jnp.int32, (KEY_BLOCK, 2 * KEY_BLOCK), 0)
    ss = lax.broadcasted_iota(jnp.int32, (KEY_BLOCK, 2 * KEY_BLOCK), 1)
    suffix = jnp.where((jj > ss) | (ss >= KEY_BLOCK), 1.0, 0.0).astype(BF16)

    def block(kb, vb, c, o, mask):
        z = _bdot_nt(qe, by_pair(kb).astype(BF16))
        log1mb = -(jnp.maximum(z, 0.0) + jnp.log(1.0 + jnp.exp(-jnp.abs(z))))
        if mask is not None:
            log1mb = jnp.where(mask, log1mb, 0.0)
        hi, lo = _split2(log1mb.reshape(pairs * 2 * tq, KEY_BLOCK))
        sums = (_dot(hi, suffix) + _dot(lo, suffix)).reshape(pairs, 2 * tq, 2 * KEY_BLOCK)
        w = jnp.exp(z + log1mb + (sums[:, :, :KEY_BLOCK] + c))
        if mask is not None:
            w = jnp.where(mask, w, 0.0)
        o = o + _bdot(w.astype(BF16), by_pair(vb).astype(BF16))
        return c + sums[:, :, KEY_BLOCK:], o

    zeros = jnp.zeros((pairs, 2 * tq, LANES), F32)
    c, o = block(kd_ref[0], vd_ref[0], zeros, zeros, causal)

    n = pl.program_id(1) if causal_prefix else n_prefix

    def live(carry):
        return (carry[0] < n) & (jnp.max(carry[1]) >= SB_DEAD_LOG)

    def body(carry):
        it, c, o = carry
        start = pl.multiple_of((n - 1 - it) * KEY_BLOCK, KEY_BLOCK)
        c, o = block(kp_ref[0, pl.ds(start, KEY_BLOCK), :], vp_ref[0, pl.ds(start, KEY_BLOCK), :],
                     c, o, None)
        return it + 1, c, o

    _, c, o = lax.while_loop(live, body, (jnp.int32(0), c, o))
    o = jnp.where(first, o[:, :tq], o[:, tq:])
    o_ref[0] = jnp.concatenate([o[p] for p in range(pairs)], axis=1)


def _sb_attention(q, kd, vd, kp, vp, *, tq, causal_prefix):
    b, t, _ = q.shape
    nq = t // tq
    tp = kp.shape[1]
    blk = lambda rows: pl.BlockSpec((1, rows, SB_WIDTH), lambda bi, i: (bi, i, 0))
    pre = pl.BlockSpec((1, tp, SB_WIDTH), lambda bi, i: (bi, 0, 0), pipeline_mode=pl.Buffered(1))
    return pl.pallas_call(
        functools.partial(_sb_kernel, tq=tq, causal_prefix=causal_prefix, n_prefix=tp // KEY_BLOCK),
        grid=(b, nq),
        in_specs=[blk(tq), blk(KEY_BLOCK), blk(KEY_BLOCK), pre, pre],
        out_specs=blk(tq),
        out_shape=jax.ShapeDtypeStruct(q.shape, F32),
        compiler_params=pltpu.CompilerParams(dimension_semantics=("arbitrary",) * 2,
                                             vmem_limit_bytes=VMEM_LIMIT),
        name="stickbreak",
    )(q, kd, vd, kp, vp)


def _rwkv_kernel(r_ref, k_ref, v_ref, l_ref, sr_ref, sk_ref, sv_ref, sl_ref, z0_ref,
                 mur_ref, muk_ref, muv_ref, mul_ref, w0_ref, wd_ref, a0_ref, wa_ref, wg_ref,
                 kkw_ref, kaw_ref, rkw_ref, lnw_ref, lnb_ref,
                 o_ref, zout_ref,
                 pr_s, pk_s, pv_s, pl_s, y_s, z_s, qe_s, yl_s, phi_s, psi_s,
                 *, tb, t_valid):
    t = pl.program_id(2)

    @pl.when(t == 0)
    def _():
        pr_s[...] = sr_ref[0]
        pk_s[...] = sk_ref[0]
        pv_s[...] = sv_ref[0]
        pl_s[...] = sl_ref[0]
        z_s[...] = z0_ref[0, 0]

    def shift(x_ref, carry_s, mu_ref):
        x = x_ref[0]
        prev = pltpu.roll(x, 1, 0)
        prev = jnp.where(lax.broadcasted_iota(jnp.int32, x.shape, 0) == 0, carry_s[...], prev)
        carry_s[...] = x[tb - 1:tb, :]
        return x + (prev - x) * mu_ref[...]

    xr = shift(r_ref, pr_s, mur_ref)
    xk = shift(k_ref, pk_s, muk_ref)
    xv = shift(v_ref, pv_s, muv_ref)
    xl = shift(l_ref, pl_s, mul_ref)

    hr = lax.broadcasted_iota(jnp.int32, (LANES, LANES), 0) // HEAD_DIM
    hc = lax.broadcasted_iota(jnp.int32, (LANES, LANES), 1) // HEAD_DIM
    head_ones = jnp.where(hr == hc, 1.0, 0.0).astype(BF16)
    head_sum = lambda x: _mm_exact_rhs(x, head_ones)

    lora = lambda x, w_ref: _dot(x.astype(BF16), w_ref[...].astype(BF16))
    log_w = -DECAY_SCALE * _sigmoid(w0_ref[...] + lora(jnp.tanh(xl), wd_ref))
    a = _sigmoid(a0_ref[...] + lora(xl, wa_ref))
    kk = xk * kkw_ref[...]
    kk = kk / jnp.maximum(jnp.sqrt(head_sum(kk * kk)), 1e-12)
    k2 = xk * (1.0 + (a - 1.0) * kaw_ref[...])
    b = kk * a
    if t_valid < tb:
        valid = lax.broadcasted_iota(jnp.int32, (tb, LANES), 0) < t_valid
        log_w = jnp.where(valid, log_w, 0.0)
        kk = jnp.where(valid, kk, 0.0)
        b = jnp.where(valid, b, 0.0)
        k2 = jnp.where(valid, k2, 0.0)
        xv = jnp.where(valid, xv, 0.0)
    nb = tb // CHUNK
    e2 = 2 * CHUNK
    by_chunk = lambda x: x.reshape(nb, CHUNK, LANES)
    ti = lax.broadcasted_iota(jnp.int32, (nb, CHUNK, CHUNK), 1)
    tj = lax.broadcasted_iota(jnp.int32, (nb, CHUNK, CHUNK), 2)
    tri = jnp.where(tj <= ti, 1.0, 0.0).astype(BF16)
    lw3 = by_chunk(log_w)
    lw_hi, lw_mid, lw_lo = _split3(lw3)
    cum3 = _bdot(tri, lw_hi) + (_bdot(tri, lw_mid) + _bdot(tri, lw_lo))
    tot = cum3[:, CHUNK - 1:CHUNK, :]
    first = lax.broadcasted_iota(jnp.int32, (nb, CHUNK, LANES), 2) < HEAD_DIM
    expand = lambda x: jnp.concatenate([jnp.where(first, x, 0.0), jnp.where(first, 0.0, x)], axis=1)
    e_neg = jnp.exp(-cum3)
    e_tail = jnp.exp(tot - cum3)
    kk3, b3, k3, v3 = by_chunk(kk), by_chunk(b), by_chunk(k2), by_chunk(xv)
    kkt = kk3 * jnp.exp(cum3 - lw3)
    rt = by_chunk(xr) * jnp.exp(cum3)

    wr = lax.broadcasted_iota(jnp.int32, (nb, CHUNK, e2), 1)
    wc = lax.broadcasted_iota(jnp.int32, (nb, CHUNK, e2), 2)
    head0 = wc < CHUNK
    ws = jnp.where(head0, wc, wc - CHUNK)
    strict, incl, eye_w = ws < wr, ws <= wr, ws == wr
    diag2 = lambda x: jnp.concatenate([jnp.where(head0, x, 0.0), jnp.where(head0, 0.0, x)], axis=1)

    gram = _bmm3_nt(jnp.concatenate([kkt, rt], axis=1),
                    jnp.concatenate([expand(k3 * e_neg), expand(b3 * e_neg)], axis=1))
    ak = jnp.where(strict, gram[:, :CHUNK, :e2], 0.0)
    ab = jnp.where(strict, gram[:, :CHUNK, e2:], 0.0)
    rk = jnp.where(incl, gram[:, CHUNK:, :e2], 0.0)
    rb = jnp.where(incl, gram[:, CHUNK:, e2:], 0.0)

    tinv = jnp.where(eye_w, 1.0, 0.0) - ab
    pw = _bmm3(ab, diag2(ab))
    n_sq = CHUNK.bit_length() - 2
    for s in range(n_sq):
        if s + 1 < n_sq:
            both = _bmm3(pw, jnp.concatenate([diag2(tinv), diag2(pw)], axis=2))
            tinv = tinv + both[:, :, :e2]
            pw = both[:, :, e2:]
        else:
            tinv = tinv + _bmm3(pw, diag2(tinv))

    ve = expand(v3)
    akve = _bmm3(ak, ve)
    wu = _bmm3(tinv, jnp.concatenate([expand(kkt), expand(akve)], axis=2))
    w_, u_loc = wu[:, :, :LANES], wu[:, :, LANES:]
    yq = _bmm3(jnp.concatenate([rk, -rb], axis=2),
               jnp.concatenate([jnp.concatenate([ve, jnp.zeros_like(ve)], axis=2),
                                jnp.concatenate([expand(u_loc), expand(w_)], axis=2)], axis=1))
    qe_s[...] = rt + yq[:, :, LANES:]
    yl_s[...] = yq[:, :, :LANES]
    zp = _bmm3(jnp.swapaxes(jnp.concatenate([k3 * e_tail, -(b3 * e_tail)], axis=1), 1, 2),
               jnp.concatenate([jnp.concatenate([v3, jnp.zeros_like(v3)], axis=2),
                                jnp.concatenate([u_loc, w_], axis=2)], axis=1))
    zr = lax.broadcasted_iota(jnp.int32, (nb, LANES, LANES), 1)
    zc = lax.broadcasted_iota(jnp.int32, (nb, LANES, LANES), 2)
    same_head = (zr // HEAD_DIM) == (zc // HEAD_DIM)
    phi_s[...] = jnp.where(zr == zc, jnp.exp(tot), 0.0) + jnp.where(same_head, zp[:, :, LANES:], 0.0)
    psi_s[...] = jnp.where(same_head, zp[:, :, :LANES], 0.0)

    def advance(c, _):
        rows = pl.ds(pl.multiple_of(c * CHUNK, CHUNK), CHUNK)
        z = z_s[...]
        y_s[rows, :] = _mm3(qe_s[c], z) + yl_s[c]
        z_s[...] = _mm3(phi_s[c], z) + psi_s[c]
        return 0

    lax.fori_loop(0, nb, advance, 0, unroll=min(nb, 2))

    y = y_s[...]
    d = y - head_sum(y) * (1.0 / HEAD_DIM)
    var = head_sum(d * d) * (1.0 / HEAD_DIM)
    o = d * lax.rsqrt(var + GN_EPS) * lnw_ref[...] + lnb_ref[...]
    bonus = head_sum(xr * k2 * rkw_ref[...]) * xv
    o_ref[0] = (o + bonus) * lora(_sigmoid(xl), wg_ref)

    @pl.when(t == pl.num_programs(2) - 1)
    def _():
        zout_ref[0, 0] = z_s[...]


def _rwkv(rkv, lora, shift_rkv, shift_lora, z0, prm, *, tb, t_valid):
    b, t, _ = rkv.shape
    nt = t // tb
    pairs = RW_WIDTH // LANES
    seg = lambda off: pl.BlockSpec((1, tb, LANES), lambda bi, p, ti, off=off: (bi, ti, off + p))
    seg0 = lambda off: pl.BlockSpec((1, 1, LANES), lambda bi, p, ti, off=off: (bi, 0, off + p))
    vec = lambda off=0: pl.BlockSpec((1, LANES), lambda bi, p, ti, off=off: (0, off + p))
    mat = pl.BlockSpec((LORA_PAD, LANES), lambda bi, p, ti: (0, p))
    state = pl.BlockSpec((1, 1, LANES, LANES), lambda bi, p, ti: (bi, p, 0, 0))
    rows = lambda w: pltpu.VMEM((tb, w), F32)
    return pl.pallas_call(
        functools.partial(_rwkv_kernel, tb=tb, t_valid=t_valid),
        grid=(b, pairs, nt),
        in_specs=[seg(0), seg(pairs), seg(2 * pairs),
                  pl.BlockSpec((1, tb, LORA_PAD), lambda bi, p, ti: (bi, ti, 0)),
                  seg0(0), seg0(pairs), seg0(2 * pairs),
                  pl.BlockSpec((1, 1, LORA_PAD), lambda bi, p, ti: (bi, 0, 0)),
                  state,
                  vec(0), vec(pairs), vec(2 * pairs),
                  pl.BlockSpec((1, LORA_PAD), lambda bi, p, ti: (0, 0)),
                  vec(), mat, vec(), mat, mat, vec(), vec(), vec(), vec(), vec()],
        out_specs=[pl.BlockSpec((1, tb, LANES), lambda bi, p, ti: (bi, ti, p)), state],
        out_shape=[jax.ShapeDtypeStruct((b, t, RW_WIDTH), F32),
                   jax.ShapeDtypeStruct((b, pairs, LANES, LANES), F32)],
        scratch_shapes=[pltpu.VMEM((1, LANES), F32), pltpu.VMEM((1, LANES), F32),
                        pltpu.VMEM((1, LANES), F32), pltpu.VMEM((1, LORA_PAD), F32),
                        pltpu.VMEM((tb, LANES), F32), pltpu.VMEM((LANES, LANES), F32)]
        + [pltpu.VMEM((tb // CHUNK, CHUNK, LANES), F32)] * 2
        + [pltpu.VMEM((tb // CHUNK, LANES, LANES), F32)] * 2,
        compiler_params=pltpu.CompilerParams(dimension_semantics=("arbitrary",) * 3,
                                             vmem_limit_bytes=VMEM_LIMIT),
        name="rwkv7",
    )(rkv, rkv, rkv, lora, shift_rkv, shift_rkv, shift_rkv, shift_lora, z0,
      prm["mu_rkv"], prm["mu_rkv"], prm["mu_rkv"], prm["mu_lora"],
      prm["w0"], prm["wd"], prm["a0"], prm["wa"], prm["wg"],
      prm["k_k"], prm["k_a"], prm["r_k"], prm["ln_w"], prm["ln_b"])


def _out_ffn_kernel(x_ref, sb_ref, rw_ref, woa_ref, wob_ref, gf_ref, wg_ref, wu_ref, wd_ref, gl_ref,
                    y_ref):
    x1 = x_ref[...] + _dot(sb_ref[...].astype(BF16), woa_ref[...]) \
        + _dot(rw_ref[...].astype(BF16), wob_ref[...])
    h2 = _rmsnorm(x1, gf_ref[...]).astype(BF16)
    gate = _dot(h2, wg_ref[...])
    act = gate * _sigmoid(gate) * _dot(h2, wu_ref[...])
    x2 = x1 + _dot(act.astype(BF16), wd_ref[...])
    y_ref[...] = _rmsnorm(x2, gl_ref[...])


def _out_ffn(x2d, o_sb, o_rw, woa, wob, gf, wg, wu, wd, gl, tm):
    m, d = x2d.shape
    row = lambda w: pl.BlockSpec((tm, w), lambda i: (i, 0))
    return pl.pallas_call(
        _out_ffn_kernel,
        grid=(m // tm,),
        in_specs=[row(d), row(SB_WIDTH), row(RW_WIDTH), _const_spec(woa.shape), _const_spec(wob.shape),
                  _const_spec((1, d)), _const_spec(wg.shape), _const_spec(wu.shape),
                  _const_spec(wd.shape), _const_spec((1, d))],
        out_specs=row(d),
        out_shape=jax.ShapeDtypeStruct((m, d), F32),
        compiler_params=pltpu.CompilerParams(dimension_semantics=("arbitrary",),
                                             vmem_limit_bytes=VMEM_LIMIT),
        name="out_ffn",
    )(x2d, o_sb, o_rw, woa, wob, gf, wg, wu, wd, gl)


def _pad_rows(x, rows):
    return jnp.pad(x, ((0, 0), (0, rows - x.shape[1]), (0, 0)))


def _pad_lanes(x, lanes):
    return jnp.pad(x, [(0, 0)] * (x.ndim - 1) + [(0, lanes - x.shape[-1])])


def _state_to_blockdiag(wkv):
    b, h = wkv.shape[:2]
    zt = jnp.swapaxes(wkv, -1, -2).reshape(b, h // 2, 2, HEAD_DIM, HEAD_DIM)
    z = jnp.einsum("bpikv,ij->bpikjv", zt, jnp.eye(2, dtype=wkv.dtype))
    return z.reshape(b, h // 2, LANES, LANES)


def _blockdiag_to_state(z):
    b, p = z.shape[:2]
    z6 = z.reshape(b, p, 2, HEAD_DIM, 2, HEAD_DIM)
    zd = jnp.stack([z6[:, :, 0, :, 0, :], z6[:, :, 1, :, 1, :]], axis=2)
    return jnp.swapaxes(zd, -1, -2).reshape(b, 2 * p, HEAD_DIM, HEAD_DIM)


def _layer(x, k_past, v_past, wkv0, shift0, w, *, tm, tq, tb):
    b, t, d = x.shape
    x2d = x.reshape(b * t, d)
    q, k, v, rkv, lora = _norm_proj(x2d, w["norm_mix_g"], w["w_qkv"], w["w_rkv"], w["w_lora"], tm)
    q3, k3, v3 = (a.reshape(b, t, SB_WIDTH) for a in (q, k, v))
    if k_past is None:
        o_sb = _sb_attention(q3, k3, v3, k3, v3, tq=tq, causal_prefix=True)
    else:
        kd, vd = _pad_rows(k3, KEY_BLOCK), _pad_rows(v3, KEY_BLOCK)
        past = k_past.shape[1]
        o_sb = _sb_attention(q3, kd, vd, k_past.reshape(b, past, SB_WIDTH),
                             v_past.reshape(b, past, SB_WIDTH), tq=tq, causal_prefix=False)

    rkv3 = rkv.reshape(b, t, 3 * RW_WIDTH)
    lora3 = lora.reshape(b, t, LORA_PAD)
    t_pad = -(-t // tb) * tb
    o_rw, z_out = _rwkv(_pad_rows(rkv3, t_pad), _pad_rows(lora3, t_pad),
                        shift0[..., :3 * RW_WIDTH], _pad_lanes(shift0[..., 3 * RW_WIDTH:], LORA_PAD),
                        _state_to_blockdiag(wkv0), w, tb=tb, t_valid=min(t, tb))
    o_rw = o_rw[:, :t]
    shift_last = jnp.concatenate([rkv3[:, t - 1:, :], lora3[:, t - 1:, :LORA_WIDTH]], axis=-1)

    y = _out_ffn(x2d, o_sb.reshape(b * t, SB_WIDTH), o_rw.reshape(b * t, RW_WIDTH),
                 w["w_out_sb"], w["w_out_rw"], w["norm_ffn_g"], w["w_gate"], w["w_up"], w["w_down"],
                 w["norm_final_g"], tm)
    heads = SB_WIDTH // HEAD_DIM
    return (y.reshape(b, t, d), k3.reshape(b, t, heads, HEAD_DIM), v3.reshape(b, t, heads, HEAD_DIM),
            _blockdiag_to_state(z_out), shift_last)


def kernel(x_prompt, x_sample, cache_k, cache_v, state_wkv, state_shift, norm_mix_g, w_in, mu_shift, w0,
           w_decay_up, a0, w_aaa_up, w_gate_up, k_k, k_a, r_k, ln_x_w, ln_x_b, w_out, norm_ffn_g, w_gate,
           w_up, w_down, norm_final_g):
    assert w_in.shape[0] == 1, "single-layer trunk"
    l = 0
    rw3 = 3 * RW_WIDTH
    lora_rows = lambda m, lo, hi: jnp.pad(m, ((lo, LORA_PAD - hi), (0, 0)))
    w = {
        "norm_mix_g": norm_mix_g[l][None, :],
        "w_qkv": w_in[l][:, :3 * SB_WIDTH].astype(BF16),
        "w_rkv": w_in[l][:, 3 * SB_WIDTH:3 * SB_WIDTH + rw3].astype(BF16),
        "w_lora": _pad_lanes(w_in[l][:, 3 * SB_WIDTH + rw3:], LORA_PAD).astype(BF16),
        "mu_rkv": mu_shift[l][None, :rw3],
        "mu_lora": _pad_lanes(mu_shift[l][None, rw3:], LORA_PAD),
        "w0": w0[l][None, :],
        "wd": lora_rows(w_decay_up[l], 0, LORA_DECAY),
        "a0": a0[l][None, :],
        "wa": lora_rows(w_aaa_up[l], LORA_DECAY, LORA_DECAY + LORA_AAA),
        "wg": lora_rows(w_gate_up[l], LORA_DECAY + LORA_AAA, LORA_WIDTH),
        "k_k": k_k[l][None, :],
        "k_a": k_a[l][None, :],
        "r_k": r_k[l].reshape(1, RW_WIDTH),
        "ln_w": ln_x_w[l][None, :],
        "ln_b": ln_x_b[l][None, :],
        "w_out_sb": w_out[l][:SB_WIDTH].astype(BF16),
        "w_out_rw": w_out[l][SB_WIDTH:].astype(BF16),
        "norm_ffn_g": norm_ffn_g[l][None, :],
        "w_gate": w_gate[l].astype(BF16),
        "w_up": w_up[l].astype(BF16),
        "w_down": w_down[l].astype(BF16),
        "norm_final_g": norm_final_g[None, :],
    }
    bp = x_prompt.shape[0]
    heads = RW_WIDTH // HEAD_DIM
    wkv_zero = jnp.zeros((bp, heads, HEAD_DIM, HEAD_DIM), x_prompt.dtype)
    shift_zero = jnp.zeros((bp, 1, rw3 + LORA_WIDTH), x_prompt.dtype)
    yp, k1, v1, s1, sh1 = _layer(x_prompt, None, None, wkv_zero, shift_zero, w, tm=512, tq=128, tb=1024)
    ys, k2, v2, s2, sh2 = _layer(x_sample, cache_k[l], cache_v[l], state_wkv[l], state_shift[l], w,
                                 tm=256, tq=x_sample.shape[1], tb=CHUNK)
    return (yp, ys, k1[None], v1[None], s1[None], sh1[None], k2[None], v2[None], s2[None], sh2[None])
```

```python
import functools
import math

import jax
import jax.numpy as jnp
from jax import lax
from jax.experimental import pallas as pl
from jax.experimental.pallas import tpu as pltpu

F32 = jnp.float32
BF16 = jnp.bfloat16

LANES = 128
HEAD_DIM = 64
SB_WIDTH = 512
RW_WIDTH = 512
LORA_DECAY = 64
LORA_AAA = 64
LORA_GATE = 160
LORA_WIDTH = LORA_DECAY + LORA_AAA + LORA_GATE
LORA_PAD = 384
RMS_EPS = 1e-6
GN_EPS = 64e-5
DECAY_SCALE = math.exp(-0.5)
KEY_BLOCK = 128
SB_DEAD_LOG = -104.0
CHUNK = 64
VMEM_LIMIT = 56 * 1024 * 1024


def _dot(a, b):
    return jnp.dot(a, b, preferred_element_type=F32)


def _split2(x):
    hi = x.astype(BF16)
    lo = (x - hi.astype(F32)).astype(BF16)
    return hi, lo


def _bdot(a, b):
    return jnp.einsum("bij,bjk->bik", a, b, preferred_element_type=F32)


def _bdot_nt(a, b):
    return jnp.einsum("bik,bjk->bij", a, b, preferred_element_type=F32)


def _bmm1(a, b):
    return _bdot(a.astype(BF16), b.astype(BF16))


def _bmm3(a, b):
    ah, al = _split2(a)
    bh, bl = _split2(b)
    return _bdot(ah, bh) + (_bdot(ah, bl) + _bdot(al, bh))


def _mm_exact_rhs(x, m):
    hi, lo = _split2(x)
    return _dot(hi, m) + _dot(lo, m)


def _sigmoid(x):
    return 1.0 / (1.0 + jnp.exp(-x))


def _rmsnorm(x, g):
    return x * lax.rsqrt(jnp.mean(x * x, axis=-1, keepdims=True) + RMS_EPS) * g


def _const_spec(shape):
    return pl.BlockSpec(shape, lambda *_: (0,) * len(shape), pipeline_mode=pl.Buffered(1))


def _norm_proj_kernel(x_ref, g_ref, wqkv_ref, wrkv_ref, wl_ref, q_ref, k_ref, v_ref, rkv_ref, l_ref):
    hb = _rmsnorm(x_ref[...], g_ref[...]).astype(BF16)
    qkv = _dot(hb, wqkv_ref[...])
    q_ref[...] = qkv[:, :SB_WIDTH]
    k_ref[...] = qkv[:, SB_WIDTH:2 * SB_WIDTH]
    v_ref[...] = qkv[:, 2 * SB_WIDTH:]
    rkv_ref[...] = _dot(hb, wrkv_ref[...])
    l_ref[...] = _dot(hb, wl_ref[...])


def _norm_proj(x2d, g, wqkv, wrkv, wl, tm):
    m, d = x2d.shape
    row = lambda w: pl.BlockSpec((tm, w), lambda i: (i, 0))
    return pl.pallas_call(
        _norm_proj_kernel,
        grid=(m // tm,),
        in_specs=[row(d), _const_spec((1, d)), _const_spec(wqkv.shape), _const_spec(wrkv.shape),
                  _const_spec(wl.shape)],
        out_specs=[row(SB_WIDTH), row(SB_WIDTH), row(SB_WIDTH), row(3 * RW_WIDTH), row(LORA_PAD)],
        out_shape=[jax.ShapeDtypeStruct((m, w), F32)
                   for w in (SB_WIDTH, SB_WIDTH, SB_WIDTH, 3 * RW_WIDTH, LORA_PAD)],
        compiler_params=pltpu.CompilerParams(dimension_semantics=("arbitrary",),
                                             vmem_limit_bytes=VMEM_LIMIT),
        name="norm_proj",
    )(x2d, g, wqkv, wrkv, wl)


def _sb_kernel(q_ref, kd_ref, vd_ref, kp_ref, vp_ref, o_ref, *, tq, causal_prefix, n_prefix):
    pairs = SB_WIDTH // LANES
    by_pair = lambda x: jnp.stack([x[:, p * LANES:(p + 1) * LANES] for p in range(pairs)])
    q = by_pair(q_ref[0] * (1.0 / math.sqrt(HEAD_DIM)))
    first = lax.broadcasted_iota(jnp.int32, (pairs, tq, LANES), 2) < HEAD_DIM
    qe = jnp.concatenate([jnp.where(first, q, 0.0), jnp.where(first, 0.0, q)], axis=1).astype(BF16)

    row = lax.broadcasted_iota(jnp.int32, (pairs, 2 * tq, KEY_BLOCK), 1)
    col = lax.broadcasted_iota(jnp.int32, (pairs, 2 * tq, KEY_BLOCK), 2)
    causal = col < jnp.where(row >= tq, row - tq, row)

    jj = lax.broadcasted_iota(jnp.int32, (KEY_BLOCK, 2 * KEY_BLOCK), 0)
    ss = lax.broadcasted_iota(jnp.int32, (KEY_BLOCK, 2 * KEY_BLOCK), 1)
    suffix = jnp.where((jj > ss) | (ss >= KEY_BLOCK), 1.0, 0.0).astype(BF16)

    def block(kb, vb, c, o, mask):
        z = _bdot_nt(qe, by_pair(kb).astype(BF16))
        log1mb = -(jnp.maximum(z, 0.0) + jnp.log(1.0 + jnp.exp(-jnp.abs(z))))
        if mask is not None:
            log1mb = jnp.where(mask, log1mb, 0.0)
        hi, lo = _split2(log1mb.reshape(pairs * 2 * tq, KEY_BLOCK))
        sums = (_dot(hi, suffix) + _dot(lo, suffix)).reshape(pairs, 2 * tq, 2 * KEY_BLOCK)
        w = jnp.exp(z + log1mb + (sums[:, :, :KEY_BLOCK] + c))
        if mask is not None:
            w = jnp.where(mask, w, 0.0)
        o = o + _bdot(w.astype(BF16), by_pair(vb).astype(BF16))
        return c + sums[:, :, KEY_BLOCK:], o

    zeros = jnp.zeros((pairs, 2 * tq, LANES), F32)
    c, o = block(kd_ref[0], vd_ref[0], zeros, zeros, causal)

    n = pl.program_id(1) if causal_prefix else n_prefix

    def live(carry):
        return (carry[0] < n) & (jnp.max(carry[1]) >= SB_DEAD_LOG)

    def body(carry):
        it, c, o = carry
        start = pl.multiple_of((n - 1 - it) * KEY_BLOCK, KEY_BLOCK)
        c, o = block(kp_ref[0, pl.ds(start, KEY_BLOCK), :], vp_ref[0, pl.ds(start, KEY_BLOCK), :],
                     c, o, None)
        return it + 1, c, o

    _, c, o = lax.while_loop(live, body, (jnp.int32(0), c, o))
    o = jnp.where(first, o[:, :tq], o[:, tq:])
    o_ref[0] = jnp.concatenate([o[p] for p in range(pairs)], axis=1)


def _sb_attention(q, kd, vd, kp, vp, *, tq, causal_prefix):
    b, t, _ = q.shape
    nq = t // tq
    tp = kp.shape[1]
    blk = lambda rows: pl.BlockSpec((1, rows, SB_WIDTH), lambda bi, i: (bi, i, 0))
    pre = pl.BlockSpec((1, tp, SB_WIDTH), lambda bi, i: (bi, 0, 0), pipeline_mode=pl.Buffered(1))
    return pl.pallas_call(
        functools.partial(_sb_kernel, tq=tq, causal_prefix=causal_prefix, n_prefix=tp // KEY_BLOCK),
        grid=(b, nq),
        in_specs=[blk(tq), blk(KEY_BLOCK), blk(KEY_BLOCK), pre, pre],
        out_specs=blk(tq),
        out_shape=jax.ShapeDtypeStruct(q.shape, F32),
        compiler_params=pltpu.CompilerParams(dimension_semantics=("arbitrary",) * 2,
                                             vmem_limit_bytes=VMEM_LIMIT),
        name="stickbreak",
    )(q, kd, vd, kp, vp)


def _rwkv_kernel(x_ref, l_ref, sx_ref, sl_ref, z0_ref, mux_ref, mul_ref, w0_ref, wd_ref, a0_ref, wa_ref,
                 wg_ref, kkw_ref, kaw_ref, rkw_ref, lnw_ref, lnb_ref,
                 o_ref, zout_ref,
                 px_s, pl_s, y_s, z_s, qe_s, yl_s, phi_s, psi_s,
                 *, tb, t_valid):
    t = pl.program_id(1)
    pairs = RW_WIDTH // LANES

    @pl.when(t == 0)
    def _():
        px_s[...] = sx_ref[0]
        pl_s[...] = sl_ref[0]
        z_s[...] = z0_ref[0]

    def shift(x_ref, carry_s, mu_ref):
        x = x_ref[0]
        prev = pltpu.roll(x, 1, 0)
        prev = jnp.where(lax.broadcasted_iota(jnp.int32, x.shape, 0) == 0, carry_s[...], prev)
        carry_s[...] = x[tb - 1:tb, :]
        return x + (prev - x) * mu_ref[...]

    xs = shift(x_ref, px_s, mux_ref)
    xr, xk, xv = xs[:, :RW_WIDTH], xs[:, RW_WIDTH:2 * RW_WIDTH], xs[:, 2 * RW_WIDTH:]
    xl = shift(l_ref, pl_s, mul_ref)

    hr = lax.broadcasted_iota(jnp.int32, (LANES, LANES), 0) // HEAD_DIM
    hc = lax.broadcasted_iota(jnp.int32, (LANES, LANES), 1) // HEAD_DIM
    head_ones = jnp.where(hr == hc, 1.0, 0.0).astype(BF16)
    head_sum = lambda x: jnp.concatenate(
        [_mm_exact_rhs(x[:, p * LANES:(p + 1) * LANES], head_ones) for p in range(pairs)], axis=1)

    lora = lambda x, w_ref: _dot(x.astype(BF16), w_ref[...].astype(BF16))
    log_w = -DECAY_SCALE * _sigmoid(w0_ref[...] + lora(jnp.tanh(xl), wd_ref))
    a = _sigmoid(a0_ref[...] + lora(xl, wa_ref))
    kk = xk * kkw_ref[...]
    kk = kk / jnp.maximum(jnp.sqrt(head_sum(kk * kk)), 1e-12)
    k2 = xk * (1.0 + (a - 1.0) * kaw_ref[...])
    b = kk * a
    if t_valid < tb:
        valid = lax.broadcasted_iota(jnp.int32, (tb, RW_WIDTH), 0) < t_valid
        log_w = jnp.where(valid, log_w, 0.0)
        kk = jnp.where(valid, kk, 0.0)
        b = jnp.where(valid, b, 0.0)
        k2 = jnp.where(valid, k2, 0.0)
        xv = jnp.where(valid, xv, 0.0)
    nc = tb // CHUNK
    nb = pairs * nc
    e2 = 2 * CHUNK
    by_chunk = lambda x: jnp.stack(
        [x[:, p * LANES:(p + 1) * LANES] for p in range(pairs)]).reshape(nb, CHUNK, LANES)
    ti = lax.broadcasted_iota(jnp.int32, (nb, CHUNK, CHUNK), 1)
    tj = lax.broadcasted_iota(jnp.int32, (nb, CHUNK, CHUNK), 2)
    tri = jnp.where(tj <= ti, 1.0, 0.0).astype(BF16)
    lw3 = by_chunk(log_w)
    lw_hi, lw_lo = _split2(lw3)
    cum3 = _bdot(tri, lw_hi) + _bdot(tri, lw_lo)
    tot = cum3[:, CHUNK - 1:CHUNK, :]
    first = lax.broadcasted_iota(jnp.int32, (nb, CHUNK, LANES), 2) < HEAD_DIM
    expand = lambda x: jnp.concatenate([jnp.where(first, x, 0.0), jnp.where(first, 0.0, x)], axis=1)
    e_neg = jnp.exp(-cum3)
    e_tail = jnp.exp(tot - cum3)
    kk3, b3, k3, v3 = by_chunk(kk), by_chunk(b), by_chunk(k2), by_chunk(xv)
    kkt = kk3 * jnp.exp(cum3 - lw3)
    rt = by_chunk(xr) * jnp.exp(cum3)

    wr = lax.broadcasted_iota(jnp.int32, (nb, CHUNK, e2), 1)
    wc = lax.broadcasted_iota(jnp.int32, (nb, CHUNK, e2), 2)
    head0 = wc < CHUNK
    ws = jnp.where(head0, wc, wc - CHUNK)
    strict, incl, eye_w = ws < wr, ws <= wr, ws == wr
    diag2 = lambda x: jnp.concatenate([jnp.where(head0, x, 0.0), jnp.where(head0, 0.0, x)], axis=1)

    gram = _bdot_nt(jnp.concatenate([kkt, rt], axis=1).astype(BF16),
                    jnp.concatenate([expand(k3 * e_neg), expand(b3 * e_neg)], axis=1).astype(BF16))
    ak = jnp.where(strict, gram[:, :CHUNK, :e2], 0.0)
    ab = jnp.where(strict, gram[:, :CHUNK, e2:], 0.0)
    rk = jnp.where(incl, gram[:, CHUNK:, :e2], 0.0)
    rb = jnp.where(incl, gram[:, CHUNK:, e2:], 0.0)

    eye2 = jnp.where(eye_w, 1.0, 0.0)
    tinv = eye2 - ab
    ab_d = diag2(ab)
    pw = _bmm1(ab, ab_d)
    n_sq = CHUNK.bit_length() - 2
    for s in range(n_sq):
        if s + 1 < n_sq:
            both = _bmm1(pw, jnp.concatenate([diag2(tinv), diag2(pw)], axis=2))
            tinv = tinv + both[:, :, :e2]
            pw = both[:, :, e2:]
        else:
            tinv = tinv + _bmm1(pw, diag2(tinv))
    resid = eye2 - tinv - _bmm3(ab, diag2(tinv))
    tinv = tinv + _bmm1(tinv, diag2(resid))

    ve = expand(v3)
    akve = _bmm1(ak, ve)
    wu = _bmm1(tinv, jnp.concatenate([expand(kkt), expand(akve)], axis=2))
    w_, u_loc = wu[:, :, :LANES], wu[:, :, LANES:]
    yq = _bmm1(jnp.concatenate([rk, -rb], axis=2),
               jnp.concatenate([jnp.concatenate([ve, jnp.zeros_like(ve)], axis=2),
                                jnp.concatenate([expand(u_loc), expand(w_)], axis=2)], axis=1))
    per_pair = lambda x: x.reshape((pairs, nc) + x.shape[1:])
    qe_s[...] = per_pair(rt + yq[:, :, LANES:])
    yl_s[...] = per_pair(yq[:, :, :LANES])
    zp = _bmm1(jnp.swapaxes(jnp.concatenate([k3 * e_tail, -(b3 * e_tail)], axis=1), 1, 2),
               jnp.concatenate([jnp.concatenate([v3, jnp.zeros_like(v3)], axis=2),
                                jnp.concatenate([u_loc, w_], axis=2)], axis=1))
    zr = lax.broadcasted_iota(jnp.int32, (nb, LANES, LANES), 1)
    zc = lax.broadcasted_iota(jnp.int32, (nb, LANES, LANES), 2)
    same_head = (zr // HEAD_DIM) == (zc // HEAD_DIM)
    phi_s[...] = per_pair(jnp.where(zr == zc, jnp.exp(tot), 0.0)
                          + jnp.where(same_head, zp[:, :, LANES:], 0.0))
    psi_s[...] = per_pair(jnp.where(same_head, zp[:, :, :LANES], 0.0))

    def advance(c, _):
        rows = pl.ds(pl.multiple_of(c * CHUNK, CHUNK), CHUNK)
        z = z_s[...]
        y_s[:, rows, :] = _bmm3(qe_s[:, c], z) + yl_s[:, c]
        z_s[...] = _bmm3(phi_s[:, c], z) + psi_s[:, c]
        return 0

    lax.fori_loop(0, nc, advance, 0, unroll=min(nc, 2))

    y = jnp.concatenate([y_s[p] for p in range(pairs)], axis=1)
    d = y - head_sum(y) * (1.0 / HEAD_DIM)
    var = head_sum(d * d) * (1.0 / HEAD_DIM)
    o = d * lax.rsqrt(var + GN_EPS) * lnw_ref[...] + lnb_ref[...]
    bonus = head_sum(xr * k2 * rkw_ref[...]) * xv
    o_ref[0] = (o + bonus) * lora(_sigmoid(xl), wg_ref)

    @pl.when(t == pl.num_programs(1) - 1)
    def _():
        zout_ref[0] = z_s[...]


def _rwkv(rkv, lora, shift_rkv, shift_lora, z0, prm, *, tb, t_valid):
    b, t, _ = rkv.shape
    nt = t // tb
    nc = tb // CHUNK
    pairs = RW_WIDTH // LANES
    rows = lambda w: pl.BlockSpec((1, tb, w), lambda bi, ti: (bi, ti, 0))
    first = lambda w: pl.BlockSpec((1, 1, w), lambda bi, ti: (bi, 0, 0))
    state = pl.BlockSpec((1, pairs, LANES, LANES), lambda bi, ti: (bi, 0, 0, 0))
    vec = _const_spec((1, RW_WIDTH))
    mat = _const_spec((LORA_PAD, RW_WIDTH))
    return pl.pallas_call(
        functools.partial(_rwkv_kernel, tb=tb, t_valid=t_valid),
        grid=(b, nt),
        in_specs=[rows(3 * RW_WIDTH), rows(LORA_PAD), first(3 * RW_WIDTH), first(LORA_PAD), state,
                  _const_spec((1, 3 * RW_WIDTH)), _const_spec((1, LORA_PAD)),
                  vec, mat, vec, mat, mat, vec, vec, vec, vec, vec],
        out_specs=[rows(RW_WIDTH), state],
        out_shape=[jax.ShapeDtypeStruct((b, t, RW_WIDTH), F32),
                   jax.ShapeDtypeStruct((b, pairs, LANES, LANES), F32)],
        scratch_shapes=[pltpu.VMEM((1, 3 * RW_WIDTH), F32), pltpu.VMEM((1, LORA_PAD), F32),
                        pltpu.VMEM((pairs, tb, LANES), F32), pltpu.VMEM((pairs, LANES, LANES), F32)]
        + [pltpu.VMEM((pairs, nc, CHUNK, LANES), F32)] * 2
        + [pltpu.VMEM((pairs, nc, LANES, LANES), F32)] * 2,
        compiler_params=pltpu.CompilerParams(dimension_semantics=("arbitrary",) * 2,
                                             vmem_limit_bytes=VMEM_LIMIT),
        name="rwkv7",
    )(rkv, lora, shift_rkv, shift_lora, z0, prm["mu_rkv"], prm["mu_lora"],
      prm["w0"], prm["wd"], prm["a0"], prm["wa"], prm["wg"],
      prm["k_k"], prm["k_a"], prm["r_k"], prm["ln_w"], prm["ln_b"])


def _out_ffn_kernel(x_ref, sb_ref, rw_ref, woa_ref, wob_ref, gf_ref, wg_ref, wu_ref, wd_ref, gl_ref,
                    y_ref):
    x1 = x_ref[...] + _dot(sb_ref[...].astype(BF16), woa_ref[...]) \
        + _dot(rw_ref[...].astype(BF16), wob_ref[...])
    h2 = _rmsnorm(x1, gf_ref[...]).astype(BF16)
    gate = _dot(h2, wg_ref[...])
    act = gate * _sigmoid(gate) * _dot(h2, wu_ref[...])
    x2 = x1 + _dot(act.astype(BF16), wd_ref[...])
    y_ref[...] = _rmsnorm(x2, gl_ref[...])


def _out_ffn(x2d, o_sb, o_rw, woa, wob, gf, wg, wu, wd, gl, tm):
    m, d = x2d.shape
    row = lambda w: pl.BlockSpec((tm, w), lambda i: (i, 0))
    return pl.pallas_call(
        _out_ffn_kernel,
        grid=(m // tm,),
        in_specs=[row(d), row(SB_WIDTH), row(RW_WIDTH), _const_spec(woa.shape), _const_spec(wob.shape),
                  _const_spec((1, d)), _const_spec(wg.shape), _const_spec(wu.shape),
                  _const_spec(wd.shape), _const_spec((1, d))],
        out_specs=row(d),
        out_shape=jax.ShapeDtypeStruct((m, d), F32),
        compiler_params=pltpu.CompilerParams(dimension_semantics=("arbitrary",),
                                             vmem_limit_bytes=VMEM_LIMIT),
        name="out_ffn",
    )(x2d, o_sb, o_rw, woa, wob, gf, wg, wu, wd, gl)


def _pad_rows(x, rows):
    return jnp.pad(x, ((0, 0), (0, rows - x.shape[1]), (0, 0)))


def _pad_lanes(x, lanes):
    return jnp.pad(x, [(0, 0)] * (x.ndim - 1) + [(0, lanes - x.shape[-1])])


def _state_to_blockdiag(wkv):
    b, h = wkv.shape[:2]
    zt = jnp.swapaxes(wkv, -1, -2).reshape(b, h // 2, 2, HEAD_DIM, HEAD_DIM)
    z = jnp.einsum("bpikv,ij->bpikjv", zt, jnp.eye(2, dtype=wkv.dtype))
    return z.reshape(b, h // 2, LANES, LANES)


def _blockdiag_to_state(z):
    b, p = z.shape[:2]
    z6 = z.reshape(b, p, 2, HEAD_DIM, 2, HEAD_DIM)
    zd = jnp.stack([z6[:, :, 0, :, 0, :], z6[:, :, 1, :, 1, :]], axis=2)
    return jnp.swapaxes(zd, -1, -2).reshape(b, 2 * p, HEAD_DIM, HEAD_DIM)


def _layer(x, k_past, v_past, wkv0, shift0, w, *, tm, tq, tb):
    b, t, d = x.shape
    x2d = x.reshape(b * t, d)
    q, k, v, rkv, lora = _norm_proj(x2d, w["norm_mix_g"], w["w_qkv"], w["w_rkv"], w["w_lora"], tm)
    q3, k3, v3 = (a.reshape(b, t, SB_WIDTH) for a in (q, k, v))
    if k_past is None:
        o_sb = _sb_attention(q3, k3, v3, k3, v3, tq=tq, causal_prefix=True)
    else:
        kd, vd = _pad_rows(k3, KEY_BLOCK), _pad_rows(v3, KEY_BLOCK)
        past = k_past.shape[1]
        o_sb = _sb_attention(q3, kd, vd, k_past.reshape(b, past, SB_WIDTH),
                             v_past.reshape(b, past, SB_WIDTH), tq=tq, causal_prefix=False)

    rkv3 = rkv.reshape(b, t, 3 * RW_WIDTH)
    lora3 = lora.reshape(b, t, LORA_PAD)
    t_pad = -(-t // tb) * tb
    o_rw, z_out = _rwkv(_pad_rows(rkv3, t_pad), _pad_rows(lora3, t_pad),
                        shift0[..., :3 * RW_WIDTH], _pad_lanes(shift0[..., 3 * RW_WIDTH:], LORA_PAD),
                        _state_to_blockdiag(wkv0), w, tb=tb, t_valid=min(t, tb))
    o_rw = o_rw[:, :t]
    shift_last = jnp.concatenate([rkv3[:, t - 1:, :], lora3[:, t - 1:, :LORA_WIDTH]], axis=-1)

    y = _out_ffn(x2d, o_sb.reshape(b * t, SB_WIDTH), o_rw.reshape(b * t, RW_WIDTH),
                 w["w_out_sb"], w["w_out_rw"], w["norm_ffn_g"], w["w_gate"], w["w_up"], w["w_down"],
                 w["norm_final_g"], tm)
    heads = SB_WIDTH // HEAD_DIM
    return (y.reshape(b, t, d), k3.reshape(b, t, heads, HEAD_DIM), v3.reshape(b, t, heads, HEAD_DIM),
            _blockdiag_to_state(z_out), shift_last)


def kernel(x_prompt, x_sample, cache_k, cache_v, state_wkv, state_shift, norm_mix_g, w_in, mu_shift, w0,
           w_decay_up, a0, w_aaa_up, w_gate_up, k_k, k_a, r_k, ln_x_w, ln_x_b, w_out, norm_ffn_g, w_gate,
           w_up, w_down, norm_final_g):
    assert w_in.shape[0] == 1, "single-layer trunk"
    l = 0
    rw3 = 3 * RW_WIDTH
    lora_rows = lambda m, lo, hi: jnp.pad(m, ((lo, LORA_PAD - hi), (0, 0)))
    w = {
        "norm_mix_g": norm_mix_g[l][None, :],
        "w_qkv": w_in[l][:, :3 * SB_WIDTH].astype(BF16),
        "w_rkv": w_in[l][:, 3 * SB_WIDTH:3 * SB_WIDTH + rw3].astype(BF16),
        "w_lora": _pad_lanes(w_in[l][:, 3 * SB_WIDTH + rw3:], LORA_PAD).astype(BF16),
        "mu_rkv": mu_shift[l][None, :rw3],
        "mu_lora": _pad_lanes(mu_shift[l][None, rw3:], LORA_PAD),
        "w0": w0[l][None, :],
        "wd": lora_rows(w_decay_up[l], 0, LORA_DECAY),
        "a0": a0[l][None, :],
        "wa": lora_rows(w_aaa_up[l], LORA_DECAY, LORA_DECAY + LORA_AAA),
        "wg": lora_rows(w_gate_up[l], LORA_DECAY + LORA_AAA, LORA_WIDTH),
        "k_k": k_k[l][None, :],
        "k_a": k_a[l][None, :],
        "r_k": r_k[l].reshape(1, RW_WIDTH),
        "ln_w": ln_x_w[l][None, :],
        "ln_b": ln_x_b[l][None, :],
        "w_out_sb": w_out[l][:SB_WIDTH].astype(BF16),
        "w_out_rw": w_out[l][SB_WIDTH:].astype(BF16),
        "norm_ffn_g": norm_ffn_g[l][None, :],
        "w_gate": w_gate[l].astype(BF16),
        "w_up": w_up[l].astype(BF16),
        "w_down": w_down[l].astype(BF16),
        "norm_final_g": norm_final_g[None, :],
    }
    bp = x_prompt.shape[0]
    heads = RW_WIDTH // HEAD_DIM
    wkv_zero = jnp.zeros((bp, heads, HEAD_DIM, HEAD_DIM), x_prompt.dtype)
    shift_zero = jnp.zeros((bp, 1, rw3 + LORA_WIDTH), x_prompt.dtype)
    yp, k1, v1, s1, sh1 = _layer(x_prompt, None, None, wkv_zero, shift_zero, w, tm=512, tq=128, tb=512)
    ys, k2, v2, s2, sh2 = _layer(x_sample, cache_k[l], cache_v[l], state_wkv[l], state_shift[l], w,
                                 tm=256, tq=x_sample.shape[1], tb=CHUNK)
    return (yp, ys, k1[None], v1[None], s1[None], sh1[None], k2[None], v2[None], s2[None], sh2[None])
```

```python
import functools
import math

import jax
import jax.numpy as jnp
from jax import lax
from jax.experimental import pallas as pl
from jax.experimental.pallas import tpu as pltpu

F32 = jnp.float32
BF16 = jnp.bfloat16

LANES = 128
HEAD_DIM = 64
SB_WIDTH = 512
SB_HEADS = SB_WIDTH // HEAD_DIM
RW_WIDTH = 512
LORA_DECAY = 64
LORA_AAA = 64
LORA_GATE = 160
LORA_WIDTH = LORA_DECAY + LORA_AAA + LORA_GATE
LORA_PAD = 384
RMS_EPS = 1e-6
GN_EPS = 64e-5
DECAY_SCALE = math.exp(-0.5)
KEY_BLOCK = 128
SB_DEAD_LOG = -104.0
CHUNK = 64
VMEM_LIMIT = 56 * 1024 * 1024


def _dot(a, b):
    return jnp.dot(a, b, preferred_element_type=F32)


def _split2(x):
    hi = x.astype(BF16)
    lo = (x - hi.astype(F32)).astype(BF16)
    return hi, lo


def _bdot(a, b):
    return jnp.einsum("bij,bjk->bik", a, b, preferred_element_type=F32)


def _bdot_nt(a, b):
    return jnp.einsum("bik,bjk->bij", a, b, preferred_element_type=F32)


def _bmm1(a, b):
    return _bdot(a.astype(BF16), b.astype(BF16))


def _bmm3(a, b):
    ah, al = _split2(a)
    bh, bl = _split2(b)
    return _bdot(ah, bh) + (_bdot(ah, bl) + _bdot(al, bh))


def _mm_exact_rhs(x, m):
    hi, lo = _split2(x)
    return _dot(hi, m) + _dot(lo, m)


def _sigmoid(x):
    return 1.0 / (1.0 + jnp.exp(-x))


def _rmsnorm(x, g):
    return x * lax.rsqrt(jnp.mean(x * x, axis=-1, keepdims=True) + RMS_EPS) * g


def _const_spec(shape):
    return pl.BlockSpec(shape, lambda *_: (0,) * len(shape), pipeline_mode=pl.Buffered(1))


def _norm_proj_kernel(x_ref, g_ref, wqkv_ref, wrkv_ref, wl_ref,
                      q_ref, k_ref, v_ref, rkv_ref, l_ref, kh_ref, vh_ref, *, tm):
    hb = _rmsnorm(x_ref[...], g_ref[...]).astype(BF16)
    qkv = _dot(hb, wqkv_ref[...])
    k, v = qkv[:, SB_WIDTH:2 * SB_WIDTH], qkv[:, 2 * SB_WIDTH:]
    q_ref[...] = qkv[:, :SB_WIDTH]
    k_ref[...] = k
    v_ref[...] = v
    for h in range(SB_HEADS):
        rows = pl.ds(h, tm, stride=SB_HEADS)
        kh_ref[rows, :] = k[:, h * HEAD_DIM:(h + 1) * HEAD_DIM]
        vh_ref[rows, :] = v[:, h * HEAD_DIM:(h + 1) * HEAD_DIM]
    rkv_ref[...] = _dot(hb, wrkv_ref[...])
    l_ref[...] = _dot(hb, wl_ref[...])


def _norm_proj(x2d, g, wqkv, wrkv, wl, tm):
    m, d = x2d.shape
    row = lambda w: pl.BlockSpec((tm, w), lambda i: (i, 0))
    by_head = pl.BlockSpec((tm * SB_HEADS, HEAD_DIM), lambda i: (i, 0))
    return pl.pallas_call(
        functools.partial(_norm_proj_kernel, tm=tm),
        grid=(m // tm,),
        in_specs=[row(d), _const_spec((1, d)), _const_spec(wqkv.shape), _const_spec(wrkv.shape),
                  _const_spec(wl.shape)],
        out_specs=[row(SB_WIDTH), row(SB_WIDTH), row(SB_WIDTH), row(3 * RW_WIDTH), row(LORA_PAD),
                   by_head, by_head],
        out_shape=[jax.ShapeDtypeStruct((m, w), F32)
                   for w in (SB_WIDTH, SB_WIDTH, SB_WIDTH, 3 * RW_WIDTH, LORA_PAD)]
        + [jax.ShapeDtypeStruct((m * SB_HEADS, HEAD_DIM), F32)] * 2,
        compiler_params=pltpu.CompilerParams(dimension_semantics=("arbitrary",),
                                             vmem_limit_bytes=VMEM_LIMIT),
        name="norm_proj",
    )(x2d, g, wqkv, wrkv, wl)


def _sb_kernel(q_ref, kd_ref, vd_ref, kp_ref, vp_ref, o_ref, *, tq, causal_prefix, prefix_by_head,
               n_prefix):
    pairs = SB_WIDTH // LANES
    by_pair = lambda x: jnp.stack([x[:, p * LANES:(p + 1) * LANES] for p in range(pairs)])
    q = by_pair(q_ref[0] * (1.0 / math.sqrt(HEAD_DIM)))
    first = lax.broadcasted_iota(jnp.int32, (pairs, tq, LANES), 2) < HEAD_DIM
    qe = jnp.concatenate([jnp.where(first, q, 0.0), jnp.where(first, 0.0, q)], axis=1).astype(BF16)

    row = lax.broadcasted_iota(jnp.int32, (pairs, 2 * tq, KEY_BLOCK), 1)
    col = lax.broadcasted_iota(jnp.int32, (pairs, 2 * tq, KEY_BLOCK), 2)
    causal = col < jnp.where(row >= tq, row - tq, row)

    jj = lax.broadcasted_iota(jnp.int32, (KEY_BLOCK, 2 * KEY_BLOCK), 0)
    ss = lax.broadcasted_iota(jnp.int32, (KEY_BLOCK, 2 * KEY_BLOCK), 1)
    suffix = jnp.where((jj > ss) | (ss >= KEY_BLOCK), 1.0, 0.0).astype(BF16)

    def prefix_block(ref, start):
        if not prefix_by_head:
            return by_pair(ref[0, pl.ds(start, KEY_BLOCK), :])
        heads = [ref[0, pl.ds(start * SB_HEADS + h, KEY_BLOCK, stride=SB_HEADS), :] for h in range(SB_HEADS)]
        return jnp.stack([jnp.concatenate(heads[2 * p:2 * p + 2], axis=1) for p in range(pairs)])

    def block(kb, vb, c, o, mask):
        z = _bdot_nt(qe, kb.astype(BF16))
        log1mb = -(jnp.maximum(z, 0.0) + jnp.log(1.0 + jnp.exp(-jnp.abs(z))))
        if mask is not None:
            log1mb = jnp.where(mask, log1mb, 0.0)
        hi, lo = _split2(log1mb.reshape(pairs * 2 * tq, KEY_BLOCK))
        sums = (_dot(hi, suffix) + _dot(lo, suffix)).reshape(pairs, 2 * tq, 2 * KEY_BLOCK)
        w = jnp.exp(z + log1mb + (sums[:, :, :KEY_BLOCK] + c))
        if mask is not None:
            w = jnp.where(mask, w, 0.0)
        o = o + _bdot(w.astype(BF16), vb.astype(BF16))
        return c + sums[:, :, KEY_BLOCK:], o

    zeros = jnp.zeros((pairs, 2 * tq, LANES), F32)
    c, o = block(by_pair(kd_ref[0]), by_pair(vd_ref[0]), zeros, zeros, causal)

    n = pl.program_id(1) if causal_prefix else n_prefix

    def live(carry):
        return (carry[0] < n) & (jnp.max(carry[1]) >= SB_DEAD_LOG)

    def body(carry):
        it, c, o = carry
        start = pl.multiple_of((n - 1 - it) * KEY_BLOCK, KEY_BLOCK)
        c, o = block(prefix_block(kp_ref, start), prefix_block(vp_ref, start), c, o, None)
        return it + 1, c, o

    _, c, o = lax.while_loop(live, body, (jnp.int32(0), c, o))
    o = jnp.where(first, o[:, :tq], o[:, tq:])
    o_ref[0] = jnp.concatenate([o[p] for p in range(pairs)], axis=1)


def _sb_attention(q, kd, vd, kp, vp, *, tq, causal_prefix):
    b, t, _ = q.shape
    nq = t // tq
    tp = kp.shape[1]
    prefix_by_head = kp.ndim == 4
    if prefix_by_head:
        kp, vp = (a.reshape(b, tp * SB_HEADS, HEAD_DIM) for a in (kp, vp))
    blk = lambda rows: pl.BlockSpec((1, rows, SB_WIDTH), lambda bi, i: (bi, i, 0))
    pre = pl.BlockSpec((1,) + kp.shape[1:], lambda bi, i: (bi, 0, 0), pipeline_mode=pl.Buffered(1))
    return pl.pallas_call(
        functools.partial(_sb_kernel, tq=tq, causal_prefix=causal_prefix, prefix_by_head=prefix_by_head,
                          n_prefix=tp // KEY_BLOCK),
        grid=(b, nq),
        in_specs=[blk(tq), blk(KEY_BLOCK), blk(KEY_BLOCK), pre, pre],
        out_specs=blk(tq),
        out_shape=jax.ShapeDtypeStruct(q.shape, F32),
        compiler_params=pltpu.CompilerParams(dimension_semantics=("arbitrary",) * 2,
                                             vmem_limit_bytes=VMEM_LIMIT),
        name="stickbreak",
    )(q, kd, vd, kp, vp)


def _rwkv_kernel(x_ref, l_ref, sx_ref, sl_ref, z0_ref, mux_ref, mul_ref, w0_ref, wd_ref, a0_ref, wa_ref,
                 wg_ref, kkw_ref, kaw_ref, rkw_ref, lnw_ref, lnb_ref,
                 o_ref, zout_ref,
                 px_s, pl_s, y_s, z_s, qe_s, yl_s, phi_s, psi_s,
                 *, tb, t_valid):
    t = pl.program_id(1)
    pairs = RW_WIDTH // LANES

    @pl.when(t == 0)
    def _():
        px_s[...] = sx_ref[0]
        pl_s[...] = sl_ref[0]
        z_s[...] = z0_ref[0]

    def shift(x_ref, carry_s, mu_ref):
        x = x_ref[0]
        prev = pltpu.roll(x, 1, 0)
        prev = jnp.where(lax.broadcasted_iota(jnp.int32, x.shape, 0) == 0, carry_s[...], prev)
        carry_s[...] = x[tb - 1:tb, :]
        return x + (prev - x) * mu_ref[...]

    xs = shift(x_ref, px_s, mux_ref)
    xr, xk, xv = xs[:, :RW_WIDTH], xs[:, RW_WIDTH:2 * RW_WIDTH], xs[:, 2 * RW_WIDTH:]
    xl = shift(l_ref, pl_s, mul_ref)

    hr = lax.broadcasted_iota(jnp.int32, (LANES, LANES), 0) // HEAD_DIM
    hc = lax.broadcasted_iota(jnp.int32, (LANES, LANES), 1) // HEAD_DIM
    head_ones = jnp.where(hr == hc, 1.0, 0.0).astype(BF16)
    head_sum = lambda x: jnp.concatenate(
        [_mm_exact_rhs(x[:, p * LANES:(p + 1) * LANES], head_ones) for p in range(pairs)], axis=1)

    lora = lambda x, w_ref: _dot(x.astype(BF16), w_ref[...].astype(BF16))
    log_w = -DECAY_SCALE * _sigmoid(w0_ref[...] + lora(jnp.tanh(xl), wd_ref))
    a = _sigmoid(a0_ref[...] + lora(xl, wa_ref))
    kk = xk * kkw_ref[...]
    kk = kk / jnp.maximum(jnp.sqrt(head_sum(kk * kk)), 1e-12)
    k2 = xk * (1.0 + (a - 1.0) * kaw_ref[...])
    b = kk * a
    if t_valid < tb:
        valid = lax.broadcasted_iota(jnp.int32, (tb, RW_WIDTH), 0) < t_valid
        log_w = jnp.where(valid, log_w, 0.0)
        kk = jnp.where(valid, kk, 0.0)
        b = jnp.where(valid, b, 0.0)
        k2 = jnp.where(valid, k2, 0.0)
        xv = jnp.where(valid, xv, 0.0)
    nc = tb // CHUNK
    nb = pairs * nc
    e2 = 2 * CHUNK
    by_chunk = lambda x: jnp.stack(
        [x[:, p * LANES:(p + 1) * LANES] for p in range(pairs)]).reshape(nb, CHUNK, LANES)
    ti = lax.broadcasted_iota(jnp.int32, (nb, CHUNK, CHUNK), 1)
    tj = lax.broadcasted_iota(jnp.int32, (nb, CHUNK, CHUNK), 2)
    tri = jnp.where(tj <= ti, 1.0, 0.0).astype(BF16)
    lw3 = by_chunk(log_w)
    lw_hi, lw_lo = _split2(lw3)
    cum3 = _bdot(tri, lw_hi) + _bdot(tri, lw_lo)
    tot = cum3[:, CHUNK - 1:CHUNK, :]
    first = lax.broadcasted_iota(jnp.int32, (nb, CHUNK, LANES), 2) < HEAD_DIM
    expand = lambda x: jnp.concatenate([jnp.where(first, x, 0.0), jnp.where(first, 0.0, x)], axis=1)
    e_neg = jnp.exp(-cum3)
    e_tail = jnp.exp(tot - cum3)
    kk3, b3, k3, v3 = by_chunk(kk), by_chunk(b), by_chunk(k2), by_chunk(xv)
    kkt = kk3 * jnp.exp(cum3 - lw3)
    rt = by_chunk(xr) * jnp.exp(cum3)

    wr = lax.broadcasted_iota(jnp.int32, (nb, CHUNK, e2), 1)
    wc = lax.broadcasted_iota(jnp.int32, (nb, CHUNK, e2), 2)
    head0 = wc < CHUNK
    ws = jnp.where(head0, wc, wc - CHUNK)
    strict, incl, eye_w = ws < wr, ws <= wr, ws == wr
    diag2 = lambda x: jnp.concatenate([jnp.where(head0, x, 0.0), jnp.where(head0, 0.0, x)], axis=1)

    gram = _bdot_nt(jnp.concatenate([kkt, rt], axis=1).astype(BF16),
                    jnp.concatenate([expand(k3 * e_neg), expand(b3 * e_neg)], axis=1).astype(BF16))
    ak = jnp.where(strict, gram[:, :CHUNK, :e2], 0.0)
    ab = jnp.where(strict, gram[:, :CHUNK, e2:], 0.0)
    rk = jnp.where(incl, gram[:, CHUNK:, :e2], 0.0)
    rb = jnp.where(incl, gram[:, CHUNK:, e2:], 0.0)

    eye2 = jnp.where(eye_w, 1.0, 0.0)
    tinv = eye2 - ab
    ab_d = diag2(ab)
    pw = _bmm1(ab, ab_d)
    n_sq = CHUNK.bit_length() - 2
    for s in range(n_sq):
        if s + 1 < n_sq:
            both = _bmm1(pw, jnp.concatenate([diag2(tinv), diag2(pw)], axis=2))
            tinv = tinv + both[:, :, :e2]
            pw = both[:, :, e2:]
        else:
            tinv = tinv + _bmm1(pw, diag2(tinv))
    resid = eye2 - tinv - _bmm3(ab, diag2(tinv))
    tinv = tinv + _bmm1(tinv, diag2(resid))

    ve = expand(v3)
    akve = _bmm1(ak, ve)
    wu = _bmm1(tinv, jnp.concatenate([expand(kkt), expand(akve)], axis=2))
    w_, u_loc = wu[:, :, :LANES], wu[:, :, LANES:]
    yq = _bmm1(jnp.concatenate([rk, -rb], axis=2),
               jnp.concatenate([jnp.concatenate([ve, jnp.zeros_like(ve)], axis=2),
                                jnp.concatenate([expand(u_loc), expand(w_)], axis=2)], axis=1))
    per_pair = lambda x: x.reshape((pairs, nc) + x.shape[1:])
    qe_s[...] = per_pair(rt + yq[:, :, LANES:])
    yl_s[...] = per_pair(yq[:, :, :LANES])
    zp = _bmm1(jnp.swapaxes(jnp.concatenate([k3 * e_tail, -(b3 * e_tail)], axis=1), 1, 2),
               jnp.concatenate([jnp.concatenate([v3, jnp.zeros_like(v3)], axis=2),
                                jnp.concatenate([u_loc, w_], axis=2)], axis=1))
    zr = lax.broadcasted_iota(jnp.int32, (nb, LANES, LANES), 1)
    zc = lax.broadcasted_iota(jnp.int32, (nb, LANES, LANES), 2)
    same_head = (zr // HEAD_DIM) == (zc // HEAD_DIM)
    phi_s[...] = per_pair(jnp.where(zr == zc, jnp.exp(tot), 0.0)
                          + jnp.where(same_head, zp[:, :, LANES:], 0.0))
    psi_s[...] = per_pair(jnp.where(same_head, zp[:, :, :LANES], 0.0))

    def advance(c, _):
        rows = pl.ds(pl.multiple_of(c * CHUNK, CHUNK), CHUNK)
        z = z_s[...]
        y_s[:, rows, :] = _bmm3(qe_s[:, c], z) + yl_s[:, c]
        z_s[...] = _bmm3(phi_s[:, c], z) + psi_s[:, c]
        return 0

    lax.fori_loop(0, nc, advance, 0, unroll=min(nc, 2))

    y = jnp.concatenate([y_s[p] for p in range(pairs)], axis=1)
    d = y - head_sum(y) * (1.0 / HEAD_DIM)
    var = head_sum(d * d) * (1.0 / HEAD_DIM)
    o = d * lax.rsqrt(var + GN_EPS) * lnw_ref[...] + lnb_ref[...]
    bonus = head_sum(xr * k2 * rkw_ref[...]) * xv
    o_ref[0] = (o + bonus) * lora(_sigmoid(xl), wg_ref)

    @pl.when(t == pl.num_programs(1) - 1)
    def _():
        zout_ref[0] = z_s[...]


def _rwkv(rkv, lora, shift_rkv, shift_lora, z0, prm, *, tb, t_valid):
    b, t, _ = rkv.shape
    nt = t // tb
    nc = tb // CHUNK
    pairs = RW_WIDTH // LANES
    rows = lambda w: pl.BlockSpec((1, tb, w), lambda bi, ti: (bi, ti, 0))
    first = lambda w: pl.BlockSpec((1, 1, w), lambda bi, ti: (bi, 0, 0))
    state = pl.BlockSpec((1, pairs, LANES, LANES), lambda bi, ti: (bi, 0, 0, 0))
    vec = _const_spec((1, RW_WIDTH))
    mat = _const_spec((LORA_PAD, RW_WIDTH))
    return pl.pallas_call(
        functools.partial(_rwkv_kernel, tb=tb, t_valid=t_valid),
        grid=(b, nt),
        in_specs=[rows(3 * RW_WIDTH), rows(LORA_PAD), first(3 * RW_WIDTH), first(LORA_PAD), state,
                  _const_spec((1, 3 * RW_WIDTH)), _const_spec((1, LORA_PAD)),
                  vec, mat, vec, mat, mat, vec, vec, vec, vec, vec],
        out_specs=[rows(RW_WIDTH), state],
        out_shape=[jax.ShapeDtypeStruct((b, t, RW_WIDTH), F32),
                   jax.ShapeDtypeStruct((b, pairs, LANES, LANES), F32)],
        scratch_shapes=[pltpu.VMEM((1, 3 * RW_WIDTH), F32), pltpu.VMEM((1, LORA_PAD), F32),
                        pltpu.VMEM((pairs, tb, LANES), F32), pltpu.VMEM((pairs, LANES, LANES), F32)]
        + [pltpu.VMEM((pairs, nc, CHUNK, LANES), F32)] * 2
        + [pltpu.VMEM((pairs, nc, LANES, LANES), F32)] * 2,
        compiler_params=pltpu.CompilerParams(dimension_semantics=("arbitrary",) * 2,
                                             vmem_limit_bytes=VMEM_LIMIT),
        name="rwkv7",
    )(rkv, lora, shift_rkv, shift_lora, z0, prm["mu_rkv"], prm["mu_lora"],
      prm["w0"], prm["wd"], prm["a0"], prm["wa"], prm["wg"],
      prm["k_k"], prm["k_a"], prm["r_k"], prm["ln_w"], prm["ln_b"])


def _out_ffn_kernel(x_ref, sb_ref, rw_ref, woa_ref, wob_ref, gf_ref, wg_ref, wu_ref, wd_ref, gl_ref,
                    y_ref):
    x1 = x_ref[...] + _dot(sb_ref[...].astype(BF16), woa_ref[...]) \
        + _dot(rw_ref[...].astype(BF16), wob_ref[...])
    h2 = _rmsnorm(x1, gf_ref[...]).astype(BF16)
    gate = _dot(h2, wg_ref[...])
    act = gate * _sigmoid(gate) * _dot(h2, wu_ref[...])
    x2 = x1 + _dot(act.astype(BF16), wd_ref[...])
    y_ref[...] = _rmsnorm(x2, gl_ref[...])


def _out_ffn(x2d, o_sb, o_rw, woa, wob, gf, wg, wu, wd, gl, tm):
    m, d = x2d.shape
    row = lambda w: pl.BlockSpec((tm, w), lambda i: (i, 0))
    return pl.pallas_call(
        _out_ffn_kernel,
        grid=(m // tm,),
        in_specs=[row(d), row(SB_WIDTH), row(RW_WIDTH), _const_spec(woa.shape), _const_spec(wob.shape),
                  _const_spec((1, d)), _const_spec(wg.shape), _const_spec(wu.shape),
                  _const_spec(wd.shape), _const_spec((1, d))],
        out_specs=row(d),
        out_shape=jax.ShapeDtypeStruct((m, d), F32),
        compiler_params=pltpu.CompilerParams(dimension_semantics=("arbitrary",),
                                             vmem_limit_bytes=VMEM_LIMIT),
        name="out_ffn",
    )(x2d, o_sb, o_rw, woa, wob, gf, wg, wu, wd, gl)


def _pad_rows(x, rows):
    return jnp.pad(x, ((0, 0), (0, rows - x.shape[1]), (0, 0)))


def _pad_lanes(x, lanes):
    return jnp.pad(x, [(0, 0)] * (x.ndim - 1) + [(0, lanes - x.shape[-1])])


def _state_to_blockdiag(wkv):
    b, h = wkv.shape[:2]
    zt = jnp.swapaxes(wkv, -1, -2).reshape(b, h // 2, 2, HEAD_DIM, HEAD_DIM)
    z = jnp.einsum("bpikv,ij->bpikjv", zt, jnp.eye(2, dtype=wkv.dtype))
    return z.reshape(b, h // 2, LANES, LANES)


def _blockdiag_to_state(z):
    b, p = z.shape[:2]
    z6 = z.reshape(b, p, 2, HEAD_DIM, 2, HEAD_DIM)
    zd = jnp.stack([z6[:, :, 0, :, 0, :], z6[:, :, 1, :, 1, :]], axis=2)
    return jnp.swapaxes(zd, -1, -2).reshape(b, 2 * p, HEAD_DIM, HEAD_DIM)


def _layer(x, k_past, v_past, wkv0, shift0, w, *, tm, tq, tb):
    b, t, d = x.shape
    x2d = x.reshape(b * t, d)
    q, k, v, rkv, lora, k_heads, v_heads = _norm_proj(
        x2d, w["norm_mix_g"], w["w_qkv"], w["w_rkv"], w["w_lora"], tm)
    q3, k3, v3 = (a.reshape(b, t, SB_WIDTH) for a in (q, k, v))
    if k_past is None:
        o_sb = _sb_attention(q3, k3, v3, k3, v3, tq=tq, causal_prefix=True)
    else:
        kd, vd = _pad_rows(k3, KEY_BLOCK), _pad_rows(v3, KEY_BLOCK)
        o_sb = _sb_attention(q3, kd, vd, k_past, v_past, tq=tq, causal_prefix=False)

    rkv3 = rkv.reshape(b, t, 3 * RW_WIDTH)
    lora3 = lora.reshape(b, t, LORA_PAD)
    t_pad = -(-t // tb) * tb
    o_rw, z_out = _rwkv(_pad_rows(rkv3, t_pad), _pad_rows(lora3, t_pad),
                        shift0[..., :3 * RW_WIDTH], _pad_lanes(shift0[..., 3 * RW_WIDTH:], LORA_PAD),
                        _state_to_blockdiag(wkv0), w, tb=tb, t_valid=min(t, tb))
    o_rw = o_rw[:, :t]
    shift_last = jnp.concatenate([rkv3[:, t - 1:, :], lora3[:, t - 1:, :LORA_WIDTH]], axis=-1)

    y = _out_ffn(x2d, o_sb.reshape(b * t, SB_WIDTH), o_rw.reshape(b * t, RW_WIDTH),
                 w["w_out_sb"], w["w_out_rw"], w["norm_ffn_g"], w["w_gate"], w["w_up"], w["w_down"],
                 w["norm_final_g"], tm)
    return (y.reshape(b, t, d), k_heads.reshape(b, t, SB_HEADS, HEAD_DIM),
            v_heads.reshape(b, t, SB_HEADS, HEAD_DIM), _blockdiag_to_state(z_out), shift_last)


def kernel(x_prompt, x_sample, cache_k, cache_v, state_wkv, state_shift, norm_mix_g, w_in, mu_shift, w0,
           w_decay_up, a0, w_aaa_up, w_gate_up, k_k, k_a, r_k, ln_x_w, ln_x_b, w_out, norm_ffn_g, w_gate,
           w_up, w_down, norm_final_g):
    assert w_in.shape[0] == 1, "single-layer trunk"
    l = 0
    rw3 = 3 * RW_WIDTH
    lora_rows = lambda m, lo, hi: jnp.pad(m, ((lo, LORA_PAD - hi), (0, 0)))
    w = {
        "norm_mix_g": norm_mix_g[l][None, :],
        "w_qkv": w_in[l][:, :3 * SB_WIDTH].astype(BF16),
        "w_rkv": w_in[l][:, 3 * SB_WIDTH:3 * SB_WIDTH + rw3].astype(BF16),
        "w_lora": _pad_lanes(w_in[l][:, 3 * SB_WIDTH + rw3:], LORA_PAD).astype(BF16),
        "mu_rkv": mu_shift[l][None, :rw3],
        "mu_lora": _pad_lanes(mu_shift[l][None, rw3:], LORA_PAD),
        "w0": w0[l][None, :],
        "wd": lora_rows(w_decay_up[l], 0, LORA_DECAY),
        "a0": a0[l][None, :],
        "wa": lora_rows(w_aaa_up[l], LORA_DECAY, LORA_DECAY + LORA_AAA),
        "wg": lora_rows(w_gate_up[l], LORA_DECAY + LORA_AAA, LORA_WIDTH),
        "k_k": k_k[l][None, :],
        "k_a": k_a[l][None, :],
        "r_k": r_k[l].reshape(1, RW_WIDTH),
        "ln_w": ln_x_w[l][None, :],
        "ln_b": ln_x_b[l][None, :],
        "w_out_sb": w_out[l][:SB_WIDTH].astype(BF16),
        "w_out_rw": w_out[l][SB_WIDTH:].astype(BF16),
        "norm_ffn_g": norm_ffn_g[l][None, :],
        "w_gate": w_gate[l].astype(BF16),
        "w_up": w_up[l].astype(BF16),
        "w_down": w_down[l].astype(BF16),
        "norm_final_g": norm_final_g[None, :],
    }
    bp = x_prompt.shape[0]
    heads = RW_WIDTH // HEAD_DIM
    wkv_zero = jnp.zeros((bp, heads, HEAD_DIM, HEAD_DIM), x_prompt.dtype)
    shift_zero = jnp.zeros((bp, 1, rw3 + LORA_WIDTH), x_prompt.dtype)
    yp, k1, v1, s1, sh1 = _layer(x_prompt, None, None, wkv_zero, shift_zero, w, tm=512, tq=128, tb=512)
    ys, k2, v2, s2, sh2 = _layer(x_sample, cache_k[l], cache_v[l], state_wkv[l], state_shift[l], w,
                                 tm=256, tq=x_sample.shape[1], tb=CHUNK)
    return (yp, ys, k1[None], v1[None], s1[None], sh1[None], k2[None], v2[None], s2[None], sh2[None])
```

```python
import functools
import math

import jax
import jax.numpy as jnp
from jax import lax
from jax.experimental import pallas as pl
from jax.experimental.pallas import tpu as pltpu

F32 = jnp.float32
BF16 = jnp.bfloat16

LANES = 128
HEAD_DIM = 64
SB_WIDTH = 512
SB_HEADS = SB_WIDTH // HEAD_DIM
RW_WIDTH = 512
LORA_DECAY = 64
LORA_AAA = 64
LORA_GATE = 160
LORA_WIDTH = LORA_DECAY + LORA_AAA + LORA_GATE
LORA_PAD = 384
RMS_EPS = 1e-6
GN_EPS = 64e-5
DECAY_SCALE = math.exp(-0.5)
KEY_BLOCK = 128
SB_DEAD_LOG = -104.0
CHUNK = 64
VMEM_LIMIT = 56 * 1024 * 1024


def _dot(a, b):
    return jnp.dot(a, b, preferred_element_type=F32)


def _split2(x):
    hi = x.astype(BF16)
    lo = (x - hi.astype(F32)).astype(BF16)
    return hi, lo


def _bdot(a, b):
    return jnp.einsum("bij,bjk->bik", a, b, preferred_element_type=F32)


def _bdot_nt(a, b):
    return jnp.einsum("bik,bjk->bij", a, b, preferred_element_type=F32)


def _bmm1(a, b):
    return _bdot(a.astype(BF16), b.astype(BF16))


def _bmm3(a, b):
    ah, al = _split2(a)
    bh, bl = _split2(b)
    return _bdot(ah, bh) + (_bdot(ah, bl) + _bdot(al, bh))


def _mm_exact_rhs(x, m):
    hi, lo = _split2(x)
    return _dot(hi, m) + _dot(lo, m)


def _sigmoid(x):
    return 1.0 / (1.0 + jnp.exp(-x))


def _rmsnorm(x, g):
    return x * lax.rsqrt(jnp.mean(x * x, axis=-1, keepdims=True) + RMS_EPS) * g


def _const_spec(shape):
    return pl.BlockSpec(shape, lambda *_: (0,) * len(shape), pipeline_mode=pl.Buffered(1))


def _norm_proj_kernel(x_ref, g_ref, wq_ref, wkv_ref, wrkv_ref, wl_ref,
                      q_ref, k_ref, v_ref, rkv_ref, l_ref, *, feature_major_kv):
    hb = _rmsnorm(x_ref[...], g_ref[...]).astype(BF16)
    q_ref[...] = _dot(hb, wq_ref[...])
    if feature_major_kv:
        kv = lax.dot_general(wkv_ref[...], hb, (((1,), (1,)), ((), ())), preferred_element_type=F32)
        k_ref[0] = kv[:SB_WIDTH]
        v_ref[0] = kv[SB_WIDTH:]
    else:
        kv = _dot(hb, wkv_ref[...])
        k_ref[...] = kv[:, :SB_WIDTH]
        v_ref[...] = kv[:, SB_WIDTH:]
    rkv_ref[...] = _dot(hb, wrkv_ref[...])
    l_ref[...] = _dot(hb, wl_ref[...])


def _norm_proj(x, g, wq, wkv, wrkv, wl, tm, feature_major_kv):
    b, t, d = x.shape
    m = b * t
    row = lambda w: pl.BlockSpec((tm, w), lambda i: (i, 0))
    if feature_major_kv:
        per_batch = t // tm
        kv_spec = pl.BlockSpec((1, SB_WIDTH, tm), lambda i: (i // per_batch, 0, i % per_batch))
        kv_shape = jax.ShapeDtypeStruct((b, SB_WIDTH, t), F32)
    else:
        kv_spec, kv_shape = row(SB_WIDTH), jax.ShapeDtypeStruct((m, SB_WIDTH), F32)
    return pl.pallas_call(
        functools.partial(_norm_proj_kernel, feature_major_kv=feature_major_kv),
        grid=(m // tm,),
        in_specs=[row(d), _const_spec((1, d)), _const_spec(wq.shape), _const_spec(wkv.shape),
                  _const_spec(wrkv.shape), _const_spec(wl.shape)],
        out_specs=[row(SB_WIDTH), kv_spec, kv_spec, row(3 * RW_WIDTH), row(LORA_PAD)],
        out_shape=[jax.ShapeDtypeStruct((m, SB_WIDTH), F32), kv_shape, kv_shape,
                   jax.ShapeDtypeStruct((m, 3 * RW_WIDTH), F32), jax.ShapeDtypeStruct((m, LORA_PAD), F32)],
        compiler_params=pltpu.CompilerParams(dimension_semantics=("arbitrary",),
                                             vmem_limit_bytes=VMEM_LIMIT),
        name="norm_proj",
    )(x.reshape(m, d), g, wq, wkv, wrkv, wl)


def _sb_kernel(q_ref, kd_ref, vd_ref, kp_ref, vp_ref, o_ref, *, tq, causal_prefix, own_transposed,
               prefix_transposed, n_prefix):
    pairs = SB_WIDTH // LANES
    by_pair = lambda x: jnp.stack([x[:, p * LANES:(p + 1) * LANES] for p in range(pairs)])
    q = by_pair(q_ref[0] * (1.0 / math.sqrt(HEAD_DIM)))
    first = lax.broadcasted_iota(jnp.int32, (pairs, tq, LANES), 2) < HEAD_DIM
    qe = jnp.concatenate([jnp.where(first, q, 0.0), jnp.where(first, 0.0, q)], axis=1).astype(BF16)

    row = lax.broadcasted_iota(jnp.int32, (pairs, 2 * tq, KEY_BLOCK), 1)
    col = lax.broadcasted_iota(jnp.int32, (pairs, 2 * tq, KEY_BLOCK), 2)
    causal = col < jnp.where(row >= tq, row - tq, row)

    jj = lax.broadcasted_iota(jnp.int32, (KEY_BLOCK, 2 * KEY_BLOCK), 0)
    ss = lax.broadcasted_iota(jnp.int32, (KEY_BLOCK, 2 * KEY_BLOCK), 1)
    suffix = jnp.where((jj > ss) | (ss >= KEY_BLOCK), 1.0, 0.0).astype(BF16)

    def prefix_block(ref, start):
        if not prefix_transposed:
            return by_pair(ref[0, pl.ds(start, KEY_BLOCK), :])
        return ref[0, :, pl.ds(start, KEY_BLOCK)].reshape(pairs, LANES, KEY_BLOCK)

    def block(kb, vb, c, o, mask, transposed=False):
        z = (_bdot if transposed else _bdot_nt)(qe, kb.astype(BF16))
        log1mb = -(jnp.maximum(z, 0.0) + jnp.log(1.0 + jnp.exp(-jnp.abs(z))))
        if mask is not None:
            log1mb = jnp.where(mask, log1mb, 0.0)
        hi, lo = _split2(log1mb.reshape(pairs * 2 * tq, KEY_BLOCK))
        sums = (_dot(hi, suffix) + _dot(lo, suffix)).reshape(pairs, 2 * tq, 2 * KEY_BLOCK)
        w = jnp.exp(z + log1mb + (sums[:, :, :KEY_BLOCK] + c))
        if mask is not None:
            w = jnp.where(mask, w, 0.0)
        o = o + (_bdot_nt if transposed else _bdot)(w.astype(BF16), vb.astype(BF16))
        return c + sums[:, :, KEY_BLOCK:], o

    zeros = jnp.zeros((pairs, 2 * tq, LANES), F32)
    own = (lambda ref: ref[0].reshape(pairs, LANES, KEY_BLOCK)) if own_transposed else (lambda ref: by_pair(ref[0]))
    c, o = block(own(kd_ref), own(vd_ref), zeros, zeros, causal, own_transposed)

    n = pl.program_id(1) if causal_prefix else n_prefix

    def live(carry):
        return (carry[0] < n) & (jnp.max(carry[1]) >= SB_DEAD_LOG)

    def body(carry):
        it, c, o = carry
        start = pl.multiple_of((n - 1 - it) * KEY_BLOCK, KEY_BLOCK)
        c, o = block(prefix_block(kp_ref, start), prefix_block(vp_ref, start), c, o, None,
                     prefix_transposed)
        return it + 1, c, o

    _, c, o = lax.while_loop(live, body, (jnp.int32(0), c, o))
    o = jnp.where(first, o[:, :tq], o[:, tq:])
    o_ref[0] = jnp.concatenate([o[p] for p in range(pairs)], axis=1)


def _sb_attention(q, kd, vd, kp, vp, *, tq, causal_prefix, own_transposed, prefix_transposed):
    b, t, _ = q.shape
    nq = t // tq
    tp = kp.shape[2] if prefix_transposed else kp.shape[1]
    blk = lambda rows: pl.BlockSpec((1, rows, SB_WIDTH), lambda bi, i: (bi, i, 0))
    own = pl.BlockSpec((1, SB_WIDTH, KEY_BLOCK), lambda bi, i: (bi, 0, i)) if own_transposed else blk(KEY_BLOCK)
    pre = pl.BlockSpec((1,) + kp.shape[1:], lambda bi, i: (bi, 0, 0), pipeline_mode=pl.Buffered(1))
    return pl.pallas_call(
        functools.partial(_sb_kernel, tq=tq, causal_prefix=causal_prefix, own_transposed=own_transposed,
                          prefix_transposed=prefix_transposed, n_prefix=tp // KEY_BLOCK),
        grid=(b, nq),
        in_specs=[blk(tq), own, own, pre, pre],
        out_specs=blk(tq),
        out_shape=jax.ShapeDtypeStruct(q.shape, F32),
        compiler_params=pltpu.CompilerParams(dimension_semantics=("arbitrary",) * 2,
                                             vmem_limit_bytes=VMEM_LIMIT),
        name="stickbreak",
    )(q, kd, vd, kp, vp)


def _rwkv_kernel(x_ref, l_ref, sx_ref, sl_ref, z0_ref, mux_ref, mul_ref, w0_ref, wd_ref, a0_ref, wa_ref,
                 wg_ref, kkw_ref, kaw_ref, rkw_ref, lnw_ref, lnb_ref,
                 o_ref, zout_ref,
                 px_s, pl_s, y_s, z_s, qe_s, yl_s, phi_s, psi_s,
                 *, tb, t_valid):
    t = pl.program_id(1)
    pairs = RW_WIDTH // LANES

    @pl.when(t == 0)
    def _():
        px_s[...] = sx_ref[0]
        pl_s[...] = sl_ref[0]
        z_s[...] = z0_ref[0]

    def shift(x_ref, carry_s, mu_ref):
        x = x_ref[0]
        prev = pltpu.roll(x, 1, 0)
        prev = jnp.where(lax.broadcasted_iota(jnp.int32, x.shape, 0) == 0, carry_s[...], prev)
        carry_s[...] = x[tb - 1:tb, :]
        return x + (prev - x) * mu_ref[...]

    xs = shift(x_ref, px_s, mux_ref)
    xr, xk, xv = xs[:, :RW_WIDTH], xs[:, RW_WIDTH:2 * RW_WIDTH], xs[:, 2 * RW_WIDTH:]
    xl = shift(l_ref, pl_s, mul_ref)

    hr = lax.broadcasted_iota(jnp.int32, (LANES, LANES), 0) // HEAD_DIM
    hc = lax.broadcasted_iota(jnp.int32, (LANES, LANES), 1) // HEAD_DIM
    head_ones = jnp.where(hr == hc, 1.0, 0.0).astype(BF16)
    head_sum = lambda x: jnp.concatenate(
        [_mm_exact_rhs(x[:, p * LANES:(p + 1) * LANES], head_ones) for p in range(pairs)], axis=1)

    lora = lambda x, w_ref: _dot(x.astype(BF16), w_ref[...].astype(BF16))
    log_w = -DECAY_SCALE * _sigmoid(w0_ref[...] + lora(jnp.tanh(xl), wd_ref))
    a = _sigmoid(a0_ref[...] + lora(xl, wa_ref))
    kk = xk * kkw_ref[...]
    kk = kk / jnp.maximum(jnp.sqrt(head_sum(kk * kk)), 1e-12)
    k2 = xk * (1.0 + (a - 1.0) * kaw_ref[...])
    b = kk * a
    if t_valid < tb:
        valid = lax.broadcasted_iota(jnp.int32, (tb, RW_WIDTH), 0) < t_valid
        log_w = jnp.where(valid, log_w, 0.0)
        kk = jnp.where(valid, kk, 0.0)
        b = jnp.where(valid, b, 0.0)
        k2 = jnp.where(valid, k2, 0.0)
        xv = jnp.where(valid, xv, 0.0)
    nc = tb // CHUNK
    nb = pairs * nc
    e2 = 2 * CHUNK
    by_chunk = lambda x: jnp.stack(
        [x[:, p * LANES:(p + 1) * LANES] for p in range(pairs)]).reshape(nb, CHUNK, LANES)
    ti = lax.broadcasted_iota(jnp.int32, (nb, CHUNK, CHUNK), 1)
    tj = lax.broadcasted_iota(jnp.int32, (nb, CHUNK, CHUNK), 2)
    tri = jnp.where(tj <= ti, 1.0, 0.0).astype(BF16)
    lw3 = by_chunk(log_w)
    lw_hi, lw_lo = _split2(lw3)
    cum3 = _bdot(tri, lw_hi) + _bdot(tri, lw_lo)
    tot = cum3[:, CHUNK - 1:CHUNK, :]
    first = lax.broadcasted_iota(jnp.int32, (nb, CHUNK, LANES), 2) < HEAD_DIM
    expand = lambda x: jnp.concatenate([jnp.where(first, x, 0.0), jnp.where(first, 0.0, x)], axis=1)
    e_neg = jnp.exp(-cum3)
    e_tail = jnp.exp(tot - cum3)
    kk3, b3, k3, v3 = by_chunk(kk), by_chunk(b), by_chunk(k2), by_chunk(xv)
    kkt = kk3 * jnp.exp(cum3 - lw3)
    rt = by_chunk(xr) * jnp.exp(cum3)

    wr = lax.broadcasted_iota(jnp.int32, (nb, CHUNK, e2), 1)
    wc = lax.broadcasted_iota(jnp.int32, (nb, CHUNK, e2), 2)
    head0 = wc < CHUNK
    ws = jnp.where(head0, wc, wc - CHUNK)
    strict, incl, eye_w = ws < wr, ws <= wr, ws == wr
    diag2 = lambda x: jnp.concatenate([jnp.where(head0, x, 0.0), jnp.where(head0, 0.0, x)], axis=1)

    gram = _bdot_nt(jnp.concatenate([kkt, rt], axis=1).astype(BF16),
                    jnp.concatenate([expand(k3 * e_neg), expand(b3 * e_neg)], axis=1).astype(BF16))
    ak = jnp.where(strict, gram[:, :CHUNK, :e2], 0.0)
    ab = jnp.where(strict, gram[:, :CHUNK, e2:], 0.0)
    rk = jnp.where(incl, gram[:, CHUNK:, :e2], 0.0)
    rb = jnp.where(incl, gram[:, CHUNK:, e2:], 0.0)

    eye2 = jnp.where(eye_w, 1.0, 0.0)
    tinv = eye2 - ab
    ab_d = diag2(ab)
    pw = _bmm1(ab, ab_d)
    n_sq = CHUNK.bit_length() - 2
    for s in range(n_sq):
        if s + 1 < n_sq:
            both = _bmm1(pw, jnp.concatenate([diag2(tinv), diag2(pw)], axis=2))
            tinv = tinv + both[:, :, :e2]
            pw = both[:, :, e2:]
        else:
            tinv = tinv + _bmm1(pw, diag2(tinv))
    resid = eye2 - tinv - _bmm3(ab, diag2(tinv))
    tinv = tinv + _bmm1(tinv, diag2(resid))

    ve = expand(v3)
    akve = _bmm1(ak, ve)
    wu = _bmm1(tinv, jnp.concatenate([expand(kkt), expand(akve)], axis=2))
    w_, u_loc = wu[:, :, :LANES], wu[:, :, LANES:]
    yq = _bmm1(jnp.concatenate([rk, -rb], axis=2),
               jnp.concatenate([jnp.concatenate([ve, jnp.zeros_like(ve)], axis=2),
                                jnp.concatenate([expand(u_loc), expand(w_)], axis=2)], axis=1))
    per_pair = lambda x: x.reshape((pairs, nc) + x.shape[1:])
    qe_s[...] = per_pair(rt + yq[:, :, LANES:])
    yl_s[...] = per_pair(yq[:, :, :LANES])
    zp = _bmm1(jnp.swapaxes(jnp.concatenate([k3 * e_tail, -(b3 * e_tail)], axis=1), 1, 2),
               jnp.concatenate([jnp.concatenate([v3, jnp.zeros_like(v3)], axis=2),
                                jnp.concatenate([u_loc, w_], axis=2)], axis=1))
    zr = lax.broadcasted_iota(jnp.int32, (nb, LANES, LANES), 1)
    zc = lax.broadcasted_iota(jnp.int32, (nb, LANES, LANES), 2)
    same_head = (zr // HEAD_DIM) == (zc // HEAD_DIM)
    phi_s[...] = per_pair(jnp.where(zr == zc, jnp.exp(tot), 0.0)
                          + jnp.where(same_head, zp[:, :, LANES:], 0.0))
    psi_s[...] = per_pair(jnp.where(same_head, zp[:, :, :LANES], 0.0))

    def advance(c, _):
        rows = pl.ds(pl.multiple_of(c * CHUNK, CHUNK), CHUNK)
        z = z_s[...]
        y_s[:, rows, :] = _bmm3(qe_s[:, c], z) + yl_s[:, c]
        z_s[...] = _bmm3(phi_s[:, c], z) + psi_s[:, c]
        return 0

    lax.fori_loop(0, nc, advance, 0, unroll=min(nc, 2))

    y = jnp.concatenate([y_s[p] for p in range(pairs)], axis=1)
    d = y - head_sum(y) * (1.0 / HEAD_DIM)
    var = head_sum(d * d) * (1.0 / HEAD_DIM)
    o = d * lax.rsqrt(var + GN_EPS) * lnw_ref[...] + lnb_ref[...]
    bonus = head_sum(xr * k2 * rkw_ref[...]) * xv
    o_ref[0] = (o + bonus) * lora(_sigmoid(xl), wg_ref)

    @pl.when(t == pl.num_programs(1) - 1)
    def _():
        zout_ref[0] = z_s[...]


def _rwkv(rkv, lora, shift_rkv, shift_lora, z0, prm, *, tb, t_valid):
    b, t, _ = rkv.shape
    nt = t // tb
    nc = tb // CHUNK
    pairs = RW_WIDTH // LANES
    rows = lambda w: pl.BlockSpec((1, tb, w), lambda bi, ti: (bi, ti, 0))
    first = lambda w: pl.BlockSpec((1, 1, w), lambda bi, ti: (bi, 0, 0))
    state = pl.BlockSpec((1, pairs, LANES, LANES), lambda bi, ti: (bi, 0, 0, 0))
    vec = _const_spec((1, RW_WIDTH))
    mat = _const_spec((LORA_PAD, RW_WIDTH))
    return pl.pallas_call(
        functools.partial(_rwkv_kernel, tb=tb, t_valid=t_valid),
        grid=(b, nt),
        in_specs=[rows(3 * RW_WIDTH), rows(LORA_PAD), first(3 * RW_WIDTH), first(LORA_PAD), state,
                  _const_spec((1, 3 * RW_WIDTH)), _const_spec((1, LORA_PAD)),
                  vec, mat, vec, mat, mat, vec, vec, vec, vec, vec],
        out_specs=[rows(RW_WIDTH), state],
        out_shape=[jax.ShapeDtypeStruct((b, t, RW_WIDTH), F32),
                   jax.ShapeDtypeStruct((b, pairs, LANES, LANES), F32)],
        scratch_shapes=[pltpu.VMEM((1, 3 * RW_WIDTH), F32), pltpu.VMEM((1, LORA_PAD), F32),
                        pltpu.VMEM((pairs, tb, LANES), F32), pltpu.VMEM((pairs, LANES, LANES), F32)]
        + [pltpu.VMEM((pairs, nc, CHUNK, LANES), F32)] * 2
        + [pltpu.VMEM((pairs, nc, LANES, LANES), F32)] * 2,
        compiler_params=pltpu.CompilerParams(dimension_semantics=("arbitrary",) * 2,
                                             vmem_limit_bytes=VMEM_LIMIT),
        name="rwkv7",
    )(rkv, lora, shift_rkv, shift_lora, z0, prm["mu_rkv"], prm["mu_lora"],
      prm["w0"], prm["wd"], prm["a0"], prm["wa"], prm["wg"],
      prm["k_k"], prm["k_a"], prm["r_k"], prm["ln_w"], prm["ln_b"])


def _out_ffn_kernel(x_ref, sb_ref, rw_ref, woa_ref, wob_ref, gf_ref, wg_ref, wu_ref, wd_ref, gl_ref,
                    y_ref):
    x1 = x_ref[...] + _dot(sb_ref[...].astype(BF16), woa_ref[...]) \
        + _dot(rw_ref[...].astype(BF16), wob_ref[...])
    h2 = _rmsnorm(x1, gf_ref[...]).astype(BF16)
    gate = _dot(h2, wg_ref[...])
    act = gate * _sigmoid(gate) * _dot(h2, wu_ref[...])
    x2 = x1 + _dot(act.astype(BF16), wd_ref[...])
    y_ref[...] = _rmsnorm(x2, gl_ref[...])


def _out_ffn(x2d, o_sb, o_rw, woa, wob, gf, wg, wu, wd, gl, tm):
    m, d = x2d.shape
    row = lambda w: pl.BlockSpec((tm, w), lambda i: (i, 0))
    return pl.pallas_call(
        _out_ffn_kernel,
        grid=(m // tm,),
        in_specs=[row(d), row(SB_WIDTH), row(RW_WIDTH), _const_spec(woa.shape), _const_spec(wob.shape),
                  _const_spec((1, d)), _const_spec(wg.shape), _const_spec(wu.shape),
                  _const_spec(wd.shape), _const_spec((1, d))],
        out_specs=row(d),
        out_shape=jax.ShapeDtypeStruct((m, d), F32),
        compiler_params=pltpu.CompilerParams(dimension_semantics=("arbitrary",),
                                             vmem_limit_bytes=VMEM_LIMIT),
        name="out_ffn",
    )(x2d, o_sb, o_rw, woa, wob, gf, wg, wu, wd, gl)


def _pad_rows(x, rows):
    return jnp.pad(x, ((0, 0), (0, rows - x.shape[1]), (0, 0)))


def _pad_lanes(x, lanes):
    return jnp.pad(x, [(0, 0)] * (x.ndim - 1) + [(0, lanes - x.shape[-1])])


def _state_to_blockdiag(wkv):
    b, h = wkv.shape[:2]
    zt = jnp.swapaxes(wkv, -1, -2).reshape(b, h // 2, 2, HEAD_DIM, HEAD_DIM)
    z = jnp.einsum("bpikv,ij->bpikjv", zt, jnp.eye(2, dtype=wkv.dtype))
    return z.reshape(b, h // 2, LANES, LANES)


def _blockdiag_to_state(z):
    b, p = z.shape[:2]
    z6 = z.reshape(b, p, 2, HEAD_DIM, 2, HEAD_DIM)
    zd = jnp.stack([z6[:, :, 0, :, 0, :], z6[:, :, 1, :, 1, :]], axis=2)
    return jnp.swapaxes(zd, -1, -2).reshape(b, 2 * p, HEAD_DIM, HEAD_DIM)


def _layer(x, k_past, v_past, wkv0, shift0, w, *, tm, tq, tb):
    b, t, d = x.shape
    x2d = x.reshape(b * t, d)
    to_feature_major = lambda a: jnp.transpose(a, (0, 2, 3, 1)).reshape(b, SB_WIDTH, a.shape[1])
    from_feature_major = lambda a: jnp.transpose(a.reshape(b, SB_HEADS, HEAD_DIM, a.shape[2]), (0, 3, 1, 2))
    if k_past is None:
        q, k, v, rkv, lora = _norm_proj(x, w["norm_mix_g"], w["w_q"], w["w_kv_t"], w["w_rkv"], w["w_lora"],
                                        tm, feature_major_kv=True)
        o_sb = _sb_attention(q.reshape(b, t, SB_WIDTH), k, v, k, v, tq=tq, causal_prefix=True,
                             own_transposed=True, prefix_transposed=True)
        k_heads, v_heads = from_feature_major(k), from_feature_major(v)
    else:
        q, k, v, rkv, lora = _norm_proj(x, w["norm_mix_g"], w["w_q"], w["w_kv"], w["w_rkv"], w["w_lora"],
                                        tm, feature_major_kv=False)
        q3, k3, v3 = (a.reshape(b, t, SB_WIDTH) for a in (q, k, v))
        o_sb = _sb_attention(q3, _pad_rows(k3, KEY_BLOCK), _pad_rows(v3, KEY_BLOCK),
                             to_feature_major(k_past), to_feature_major(v_past), tq=tq,
                             causal_prefix=False, own_transposed=False, prefix_transposed=True)
        k_heads, v_heads = (a.reshape(b, t, SB_HEADS, HEAD_DIM) for a in (k, v))

    rkv3 = rkv.reshape(b, t, 3 * RW_WIDTH)
    lora3 = lora.reshape(b, t, LORA_PAD)
    t_pad = -(-t // tb) * tb
    o_rw, z_out = _rwkv(_pad_rows(rkv3, t_pad), _pad_rows(lora3, t_pad),
                        shift0[..., :3 * RW_WIDTH], _pad_lanes(shift0[..., 3 * RW_WIDTH:], LORA_PAD),
                        _state_to_blockdiag(wkv0), w, tb=tb, t_valid=min(t, tb))
    o_rw = o_rw[:, :t]
    shift_last = jnp.concatenate([rkv3[:, t - 1:, :], lora3[:, t - 1:, :LORA_WIDTH]], axis=-1)

    y = _out_ffn(x2d, o_sb.reshape(b * t, SB_WIDTH), o_rw.reshape(b * t, RW_WIDTH),
                 w["w_out_sb"], w["w_out_rw"], w["norm_ffn_g"], w["w_gate"], w["w_up"], w["w_down"],
                 w["norm_final_g"], tm)
    return y.reshape(b, t, d), k_heads, v_heads, _blockdiag_to_state(z_out), shift_last


def kernel(x_prompt, x_sample, cache_k, cache_v, state_wkv, state_shift, norm_mix_g, w_in, mu_shift, w0,
           w_decay_up, a0, w_aaa_up, w_gate_up, k_k, k_a, r_k, ln_x_w, ln_x_b, w_out, norm_ffn_g, w_gate,
           w_up, w_down, norm_final_g):
    assert w_in.shape[0] == 1, "single-layer trunk"
    l = 0
    rw3 = 3 * RW_WIDTH
    lora_rows = lambda m, lo, hi: jnp.pad(m, ((lo, LORA_PAD - hi), (0, 0)))
    w = {
        "norm_mix_g": norm_mix_g[l][None, :],
        "w_q": w_in[l][:, :SB_WIDTH].astype(BF16),
        "w_kv": w_in[l][:, SB_WIDTH:3 * SB_WIDTH].astype(BF16),
        "w_kv_t": w_in[l][:, SB_WIDTH:3 * SB_WIDTH].T.astype(BF16),
        "w_rkv": w_in[l][:, 3 * SB_WIDTH:3 * SB_WIDTH + rw3].astype(BF16),
        "w_lora": _pad_lanes(w_in[l][:, 3 * SB_WIDTH + rw3:], LORA_PAD).astype(BF16),
        "mu_rkv": mu_shift[l][None, :rw3],
        "mu_lora": _pad_lanes(mu_shift[l][None, rw3:], LORA_PAD),
        "w0": w0[l][None, :],
        "wd": lora_rows(w_decay_up[l], 0, LORA_DECAY),
        "a0": a0[l][None, :],
        "wa": lora_rows(w_aaa_up[l], LORA_DECAY, LORA_DECAY + LORA_AAA),
        "wg": lora_rows(w_gate_up[l], LORA_DECAY + LORA_AAA, LORA_WIDTH),
        "k_k": k_k[l][None, :],
        "k_a": k_a[l][None, :],
        "r_k": r_k[l].reshape(1, RW_WIDTH),
        "ln_w": ln_x_w[l][None, :],
        "ln_b": ln_x_b[l][None, :],
        "w_out_sb": w_out[l][:SB_WIDTH].astype(BF16),
        "w_out_rw": w_out[l][SB_WIDTH:].astype(BF16),
        "norm_ffn_g": norm_ffn_g[l][None, :],
        "w_gate": w_gate[l].astype(BF16),
        "w_up": w_up[l].astype(BF16),
        "w_down": w_down[l].astype(BF16),
        "norm_final_g": norm_final_g[None, :],
    }
    bp = x_prompt.shape[0]
    heads = RW_WIDTH // HEAD_DIM
    wkv_zero = jnp.zeros((bp, heads, HEAD_DIM, HEAD_DIM), x_prompt.dtype)
    shift_zero = jnp.zeros((bp, 1, rw3 + LORA_WIDTH), x_prompt.dtype)
    yp, k1, v1, s1, sh1 = _layer(x_prompt, None, None, wkv_zero, shift_zero, w, tm=512, tq=128, tb=512)
    ys, k2, v2, s2, sh2 = _layer(x_sample, cache_k[l], cache_v[l], state_wkv[l], state_shift[l], w,
                                 tm=256, tq=x_sample.shape[1], tb=CHUNK)
    return (yp, ys, k1[None], v1[None], s1[None], sh1[None], k2[None], v2[None], s2[None], sh2[None])
```

```python
import functools
import math

import jax
import jax.numpy as jnp
from jax import lax
from jax.experimental import pallas as pl
from jax.experimental.pallas import tpu as pltpu

F32 = jnp.float32
BF16 = jnp.bfloat16

LANES = 128
HEAD_DIM = 64
SB_WIDTH = 512
SB_HEADS = SB_WIDTH // HEAD_DIM
RW_WIDTH = 512
LORA_DECAY = 64
LORA_AAA = 64
LORA_GATE = 160
LORA_WIDTH = LORA_DECAY + LORA_AAA + LORA_GATE
LORA_PAD = 384
RMS_EPS = 1e-6
GN_EPS = 64e-5
DECAY_SCALE = math.exp(-0.5)
KEY_BLOCK = 128
SB_DEAD_LOG = -104.0
CHUNK = 64
VMEM_LIMIT = 56 * 1024 * 1024


def _dot(a, b):
    return jnp.dot(a, b, preferred_element_type=F32)


def _split2(x):
    hi = x.astype(BF16)
    lo = (x - hi.astype(F32)).astype(BF16)
    return hi, lo


def _bdot(a, b):
    return jnp.einsum("bij,bjk->bik", a, b, preferred_element_type=F32)


def _bdot_nt(a, b):
    return jnp.einsum("bik,bjk->bij", a, b, preferred_element_type=F32)


def _bmm1(a, b):
    return _bdot(a.astype(BF16), b.astype(BF16))


def _bmm3(a, b):
    ah, al = _split2(a)
    bh, bl = _split2(b)
    return _bdot(ah, bh) + (_bdot(ah, bl) + _bdot(al, bh))


def _mm_exact_rhs(x, m):
    hi, lo = _split2(x)
    return _dot(hi, m) + _dot(lo, m)


def _sigmoid(x):
    return 0.5 * jnp.tanh(0.5 * x) + 0.5


def _rmsnorm(x, g):
    return x * lax.rsqrt(jnp.mean(x * x, axis=-1, keepdims=True) + RMS_EPS) * g


def _const_spec(shape):
    return pl.BlockSpec(shape, lambda *_: (0,) * len(shape), pipeline_mode=pl.Buffered(1))


def _norm_proj_kernel(x_ref, g_ref, wq_ref, wkv_ref, wrkv_ref, wl_ref,
                      q_ref, k_ref, v_ref, rkv_ref, l_ref, *, feature_major_kv):
    hb = _rmsnorm(x_ref[...], g_ref[...]).astype(BF16)
    q_ref[...] = _dot(hb, wq_ref[...])
    if feature_major_kv:
        kv = lax.dot_general(wkv_ref[...], hb, (((1,), (1,)), ((), ())), preferred_element_type=F32)
        k_ref[0] = kv[:SB_WIDTH]
        v_ref[0] = kv[SB_WIDTH:]
    else:
        kv = _dot(hb, wkv_ref[...])
        k_ref[...] = kv[:, :SB_WIDTH]
        v_ref[...] = kv[:, SB_WIDTH:]
    rkv_ref[...] = _dot(hb, wrkv_ref[...])
    l_ref[...] = _dot(hb, wl_ref[...])


def _norm_proj(x, g, wq, wkv, wrkv, wl, tm, feature_major_kv):
    b, t, d = x.shape
    m = b * t
    row = lambda w: pl.BlockSpec((tm, w), lambda i: (i, 0))
    if feature_major_kv:
        per_batch = t // tm
        kv_spec = pl.BlockSpec((1, SB_WIDTH, tm), lambda i: (i // per_batch, 0, i % per_batch))
        kv_shape = jax.ShapeDtypeStruct((b, SB_WIDTH, t), F32)
    else:
        kv_spec, kv_shape = row(SB_WIDTH), jax.ShapeDtypeStruct((m, SB_WIDTH), F32)
    return pl.pallas_call(
        functools.partial(_norm_proj_kernel, feature_major_kv=feature_major_kv),
        grid=(m // tm,),
        in_specs=[row(d), _const_spec((1, d)), _const_spec(wq.shape), _const_spec(wkv.shape),
                  _const_spec(wrkv.shape), _const_spec(wl.shape)],
        out_specs=[row(SB_WIDTH), kv_spec, kv_spec, row(3 * RW_WIDTH), row(LORA_PAD)],
        out_shape=[jax.ShapeDtypeStruct((m, SB_WIDTH), F32), kv_shape, kv_shape,
                   jax.ShapeDtypeStruct((m, 3 * RW_WIDTH), F32), jax.ShapeDtypeStruct((m, LORA_PAD), F32)],
        compiler_params=pltpu.CompilerParams(dimension_semantics=("arbitrary",),
                                             vmem_limit_bytes=VMEM_LIMIT),
        name="norm_proj",
    )(x.reshape(m, d), g, wq, wkv, wrkv, wl)


def _sb_kernel(q_ref, kd_ref, vd_ref, kp_ref, vp_ref, o_ref, *, tq, causal_prefix, own_transposed,
               prefix_transposed, n_prefix):
    pairs = SB_WIDTH // LANES
    by_pair = lambda x: jnp.stack([x[:, p * LANES:(p + 1) * LANES] for p in range(pairs)])
    q = by_pair(q_ref[0] * (1.0 / math.sqrt(HEAD_DIM)))
    first = lax.broadcasted_iota(jnp.int32, (pairs, tq, LANES), 2) < HEAD_DIM
    qe = jnp.concatenate([jnp.where(first, q, 0.0), jnp.where(first, 0.0, q)], axis=1).astype(BF16)

    row = lax.broadcasted_iota(jnp.int32, (pairs, 2 * tq, KEY_BLOCK), 1)
    col = lax.broadcasted_iota(jnp.int32, (pairs, 2 * tq, KEY_BLOCK), 2)
    causal = col < jnp.where(row >= tq, row - tq, row)

    jj = lax.broadcasted_iota(jnp.int32, (KEY_BLOCK, 2 * KEY_BLOCK), 0)
    ss = lax.broadcasted_iota(jnp.int32, (KEY_BLOCK, 2 * KEY_BLOCK), 1)
    suffix = jnp.where((jj > ss) | (ss >= KEY_BLOCK), 1.0, 0.0).astype(BF16)
    suffix2 = jnp.concatenate([suffix, suffix], axis=0)

    def prefix_block(ref, start):
        if not prefix_transposed:
            return by_pair(ref[0, pl.ds(start, KEY_BLOCK), :])
        return ref[0, :, pl.ds(start, KEY_BLOCK)].reshape(pairs, LANES, KEY_BLOCK)

    def block(kb, vb, c, o, mask, transposed=False):
        z = (_bdot if transposed else _bdot_nt)(qe, kb.astype(BF16))
        log1mb = -(jnp.maximum(z, 0.0) + jnp.log(1.0 + jnp.exp(-jnp.abs(z))))
        if mask is not None:
            log1mb = jnp.where(mask, log1mb, 0.0)
        hi_lo = jnp.concatenate(_split2(log1mb.reshape(pairs * 2 * tq, KEY_BLOCK)), axis=1)
        sums = _dot(hi_lo, suffix2).reshape(pairs, 2 * tq, 2 * KEY_BLOCK)
        w = jnp.exp(z + log1mb + (sums[:, :, :KEY_BLOCK] + c))
        if mask is not None:
            w = jnp.where(mask, w, 0.0)
        o = o + (_bdot_nt if transposed else _bdot)(w.astype(BF16), vb.astype(BF16))
        return c + sums[:, :, KEY_BLOCK:], o

    zeros = jnp.zeros((pairs, 2 * tq, LANES), F32)
    own = (lambda ref: ref[0].reshape(pairs, LANES, KEY_BLOCK)) if own_transposed else (lambda ref: by_pair(ref[0]))
    c, o = block(own(kd_ref), own(vd_ref), zeros, zeros, causal, own_transposed)

    n = pl.program_id(1) if causal_prefix else n_prefix

    def live(carry):
        return (carry[0] < n) & (carry[1] > 0)

    def body(carry):
        it, _, c, o = carry
        start = pl.multiple_of((n - 1 - it) * KEY_BLOCK, KEY_BLOCK)
        c, o = block(prefix_block(kp_ref, start), prefix_block(vp_ref, start), c, o, None,
                     prefix_transposed)
        return it + 1, (jnp.max(c) >= SB_DEAD_LOG).astype(jnp.int32), c, o

    _, _, c, o = lax.while_loop(live, body, (jnp.int32(0), jnp.int32(1), c, o))
    o = jnp.where(first, o[:, :tq], o[:, tq:])
    o_ref[0] = jnp.concatenate([o[p] for p in range(pairs)], axis=1)


def _sb_attention(q, kd, vd, kp, vp, *, tq, causal_prefix, own_transposed, prefix_transposed):
    b, t, _ = q.shape
    nq = t // tq
    tp = kp.shape[2] if prefix_transposed else kp.shape[1]
    blk = lambda rows: pl.BlockSpec((1, rows, SB_WIDTH), lambda bi, i: (bi, i, 0))
    own = pl.BlockSpec((1, SB_WIDTH, KEY_BLOCK), lambda bi, i: (bi, 0, i)) if own_transposed else blk(KEY_BLOCK)
    pre = pl.BlockSpec((1,) + kp.shape[1:], lambda bi, i: (bi, 0, 0), pipeline_mode=pl.Buffered(1))
    return pl.pallas_call(
        functools.partial(_sb_kernel, tq=tq, causal_prefix=causal_prefix, own_transposed=own_transposed,
                          prefix_transposed=prefix_transposed, n_prefix=tp // KEY_BLOCK),
        grid=(b, nq),
        in_specs=[blk(tq), own, own, pre, pre],
        out_specs=blk(tq),
        out_shape=jax.ShapeDtypeStruct(q.shape, F32),
        compiler_params=pltpu.CompilerParams(dimension_semantics=("arbitrary",) * 2,
                                             vmem_limit_bytes=VMEM_LIMIT),
        name="stickbreak",
    )(q, kd, vd, kp, vp)


def _rwkv_kernel(x_ref, l_ref, sx_ref, sl_ref, z0_ref, mux_ref, mul_ref, w0_ref, wd_ref, a0_ref, wa_ref,
                 wg_ref, kkw_ref, kaw_ref, rkw_ref, lnw_ref, lnb_ref,
                 o_ref, zout_ref,
                 px_s, pl_s, y_s, z_s, qe_s, yl_s, phi_s, psi_s,
                 *, tb, t_valid):
    t = pl.program_id(1)
    pairs = RW_WIDTH // LANES

    @pl.when(t == 0)
    def _():
        px_s[...] = sx_ref[0]
        pl_s[...] = sl_ref[0]
        z_s[...] = z0_ref[0]

    def shift(x_ref, carry_s, mu_ref):
        x = x_ref[0]
        prev = pltpu.roll(x, 1, 0)
        prev = jnp.where(lax.broadcasted_iota(jnp.int32, x.shape, 0) == 0, carry_s[...], prev)
        carry_s[...] = x[tb - 1:tb, :]
        return x + (prev - x) * mu_ref[...]

    xs = shift(x_ref, px_s, mux_ref)
    xr, xk, xv = xs[:, :RW_WIDTH], xs[:, RW_WIDTH:2 * RW_WIDTH], xs[:, 2 * RW_WIDTH:]
    xl = shift(l_ref, pl_s, mul_ref)

    hr = lax.broadcasted_iota(jnp.int32, (LANES, LANES), 0) // HEAD_DIM
    hc = lax.broadcasted_iota(jnp.int32, (LANES, LANES), 1) // HEAD_DIM
    head_ones = jnp.where(hr == hc, 1.0, 0.0).astype(BF16)
    head_sum = lambda x: jnp.concatenate(
        [_mm_exact_rhs(x[:, p * LANES:(p + 1) * LANES], head_ones) for p in range(pairs)], axis=1)

    lora = lambda x, w_ref: _dot(x.astype(BF16), w_ref[...].astype(BF16))
    log_w = -DECAY_SCALE * _sigmoid(w0_ref[...] + lora(jnp.tanh(xl), wd_ref))
    a = _sigmoid(a0_ref[...] + lora(xl, wa_ref))
    kk = xk * kkw_ref[...]
    kk = kk * lax.rsqrt(jnp.maximum(head_sum(kk * kk), 1e-24))
    k2 = xk * (1.0 + (a - 1.0) * kaw_ref[...])
    b = kk * a
    if t_valid < tb:
        valid = lax.broadcasted_iota(jnp.int32, (tb, RW_WIDTH), 0) < t_valid
        log_w = jnp.where(valid, log_w, 0.0)
        kk = jnp.where(valid, kk, 0.0)
        b = jnp.where(valid, b, 0.0)
        k2 = jnp.where(valid, k2, 0.0)
        xv = jnp.where(valid, xv, 0.0)
    nc = tb // CHUNK
    nb = pairs * nc
    e2 = 2 * CHUNK
    by_chunk = lambda x: jnp.stack(
        [x[:, p * LANES:(p + 1) * LANES] for p in range(pairs)]).reshape(nb, CHUNK, LANES)
    ti = lax.broadcasted_iota(jnp.int32, (nb, CHUNK, CHUNK), 1)
    tj = lax.broadcasted_iota(jnp.int32, (nb, CHUNK, CHUNK), 2)
    tri = jnp.where(tj <= ti, 1.0, 0.0).astype(BF16)
    lw3 = by_chunk(log_w)
    lw_hi, lw_lo = _split2(lw3)
    cum3 = _bdot(tri, lw_hi) + _bdot(tri, lw_lo)
    tot = cum3[:, CHUNK - 1:CHUNK, :]
    first = lax.broadcasted_iota(jnp.int32, (nb, CHUNK, LANES), 2) < HEAD_DIM
    expand = lambda x: jnp.concatenate([jnp.where(first, x, 0.0), jnp.where(first, 0.0, x)], axis=1)
    e_neg = jnp.exp(-cum3)
    e_tail = jnp.exp(tot - cum3)
    kk3, b3, k3, v3 = by_chunk(kk), by_chunk(b), by_chunk(k2), by_chunk(xv)
    kkt = kk3 * jnp.exp(cum3 - lw3)
    rt = by_chunk(xr) * jnp.exp(cum3)

    wr = lax.broadcasted_iota(jnp.int32, (nb, CHUNK, e2), 1)
    wc = lax.broadcasted_iota(jnp.int32, (nb, CHUNK, e2), 2)
    head0 = wc < CHUNK
    ws = jnp.where(head0, wc, wc - CHUNK)
    strict, incl, eye_w = ws < wr, ws <= wr, ws == wr
    diag2 = lambda x: jnp.concatenate([jnp.where(head0, x, 0.0), jnp.where(head0, 0.0, x)], axis=1)

    gram = _bdot_nt(jnp.concatenate([kkt, rt], axis=1).astype(BF16),
                    jnp.concatenate([expand(k3 * e_neg), expand(b3 * e_neg)], axis=1).astype(BF16))
    ak = jnp.where(strict, gram[:, :CHUNK, :e2], 0.0)
    ab = jnp.where(strict, gram[:, :CHUNK, e2:], 0.0)
    rk = jnp.where(incl, gram[:, CHUNK:, :e2], 0.0)
    rb = jnp.where(incl, gram[:, CHUNK:, e2:], 0.0)

    eye2 = jnp.where(eye_w, 1.0, 0.0)
    tinv = eye2 - ab
    ab_d = diag2(ab)
    pw = _bmm1(ab, ab_d)
    n_sq = CHUNK.bit_length() - 2
    for s in range(n_sq):
        if s + 1 < n_sq:
            both = _bmm1(pw, jnp.concatenate([diag2(tinv), diag2(pw)], axis=2))
            tinv = tinv + both[:, :, :e2]
            pw = both[:, :, e2:]
        else:
            tinv = tinv + _bmm1(pw, diag2(tinv))
    resid = eye2 - tinv - _bmm3(ab, diag2(tinv))
    tinv = tinv + _bmm1(tinv, diag2(resid))

    ve = expand(v3)
    akve = _bmm1(ak, ve)
    wu = _bmm1(tinv, jnp.concatenate([expand(kkt), expand(akve)], axis=2))
    w_, u_loc = wu[:, :, :LANES], wu[:, :, LANES:]
    yq = _bmm1(jnp.concatenate([rk, -rb], axis=2),
               jnp.concatenate([jnp.concatenate([ve, jnp.zeros_like(ve)], axis=2),
                                jnp.concatenate([expand(u_loc), expand(w_)], axis=2)], axis=1))
    per_pair = lambda x: x.reshape((pairs, nc) + x.shape[1:])
    qe_s[...] = per_pair(rt + yq[:, :, LANES:])
    yl_s[...] = per_pair(yq[:, :, :LANES])
    zp = _bmm1(jnp.swapaxes(jnp.concatenate([k3 * e_tail, -(b3 * e_tail)], axis=1), 1, 2),
               jnp.concatenate([jnp.concatenate([v3, jnp.zeros_like(v3)], axis=2),
                                jnp.concatenate([u_loc, w_], axis=2)], axis=1))
    zr = lax.broadcasted_iota(jnp.int32, (nb, HEAD_DIM, LANES), 1)
    zc = lax.broadcasted_iota(jnp.int32, (nb, HEAD_DIM, LANES), 2)
    zhead0 = zc < HEAD_DIM
    own_block = lambda x: jnp.where(zhead0, x[:, :HEAD_DIM], x[:, HEAD_DIM:])
    decay = jnp.where(jnp.where(zhead0, zc, zc - HEAD_DIM) == zr, jnp.exp(tot), 0.0)
    phi_s[...] = per_pair(decay + own_block(zp[:, :, LANES:]))
    psi_s[...] = per_pair(own_block(zp[:, :, :LANES]))
    shead0 = lax.broadcasted_iota(jnp.int32, (pairs, HEAD_DIM, LANES), 2) < HEAD_DIM

    def advance(c, _):
        rows = pl.ds(pl.multiple_of(c * CHUNK, CHUNK), CHUNK)
        z = z_s[...]
        y_s[:, rows, :] = _bmm1(qe_s[:, c], z) + yl_s[:, c]
        z_new = _bmm3(phi_s[:, c], z) + psi_s[:, c]
        z_s[...] = jnp.concatenate([jnp.where(shead0, z_new, 0.0), jnp.where(shead0, 0.0, z_new)], axis=1)
        return 0

    lax.fori_loop(0, nc, advance, 0, unroll=min(nc, 2))

    y = jnp.concatenate([y_s[p] for p in range(pairs)], axis=1)
    d = y - head_sum(y) * (1.0 / HEAD_DIM)
    var = head_sum(d * d) * (1.0 / HEAD_DIM)
    o = d * lax.rsqrt(var + GN_EPS) * lnw_ref[...] + lnb_ref[...]
    bonus = head_sum(xr * k2 * rkw_ref[...]) * xv
    o_ref[0] = (o + bonus) * lora(_sigmoid(xl), wg_ref)

    @pl.when(t == pl.num_programs(1) - 1)
    def _():
        zout_ref[0] = z_s[...]


def _rwkv(rkv, lora, shift_rkv, shift_lora, z0, prm, *, tb, t_valid):
    b, t, _ = rkv.shape
    nt = t // tb
    nc = tb // CHUNK
    pairs = RW_WIDTH // LANES
    rows = lambda w: pl.BlockSpec((1, tb, w), lambda bi, ti: (bi, ti, 0))
    first = lambda w: pl.BlockSpec((1, 1, w), lambda bi, ti: (bi, 0, 0))
    state = pl.BlockSpec((1, pairs, LANES, LANES), lambda bi, ti: (bi, 0, 0, 0))
    vec = _const_spec((1, RW_WIDTH))
    mat = _const_spec((LORA_PAD, RW_WIDTH))
    return pl.pallas_call(
        functools.partial(_rwkv_kernel, tb=tb, t_valid=t_valid),
        grid=(b, nt),
        in_specs=[rows(3 * RW_WIDTH), rows(LORA_PAD), first(3 * RW_WIDTH), first(LORA_PAD), state,
                  _const_spec((1, 3 * RW_WIDTH)), _const_spec((1, LORA_PAD)),
                  vec, mat, vec, mat, mat, vec, vec, vec, vec, vec],
        out_specs=[rows(RW_WIDTH), state],
        out_shape=[jax.ShapeDtypeStruct((b, t, RW_WIDTH), F32),
                   jax.ShapeDtypeStruct((b, pairs, LANES, LANES), F32)],
        scratch_shapes=[pltpu.VMEM((1, 3 * RW_WIDTH), F32), pltpu.VMEM((1, LORA_PAD), F32),
                        pltpu.VMEM((pairs, tb, LANES), F32), pltpu.VMEM((pairs, LANES, LANES), F32)]
        + [pltpu.VMEM((pairs, nc, CHUNK, LANES), F32)] * 2
        + [pltpu.VMEM((pairs, nc, HEAD_DIM, LANES), F32)] * 2,
        compiler_params=pltpu.CompilerParams(dimension_semantics=("arbitrary",) * 2,
                                             vmem_limit_bytes=VMEM_LIMIT),
        name="rwkv7",
    )(rkv, lora, shift_rkv, shift_lora, z0, prm["mu_rkv"], prm["mu_lora"],
      prm["w0"], prm["wd"], prm["a0"], prm["wa"], prm["wg"],
      prm["k_k"], prm["k_a"], prm["r_k"], prm["ln_w"], prm["ln_b"])


def _out_ffn_kernel(x_ref, sb_ref, rw_ref, woa_ref, wob_ref, gf_ref, wg_ref, wu_ref, wd_ref, gl_ref,
                    y_ref):
    x1 = x_ref[...] + _dot(sb_ref[...].astype(BF16), woa_ref[...]) \
        + _dot(rw_ref[...].astype(BF16), wob_ref[...])
    h2 = _rmsnorm(x1, gf_ref[...]).astype(BF16)
    gate = _dot(h2, wg_ref[...])
    act = gate * _sigmoid(gate) * _dot(h2, wu_ref[...])
    x2 = x1 + _dot(act.astype(BF16), wd_ref[...])
    y_ref[...] = _rmsnorm(x2, gl_ref[...])


def _out_ffn(x2d, o_sb, o_rw, woa, wob, gf, wg, wu, wd, gl, tm):
    m, d = x2d.shape
    row = lambda w: pl.BlockSpec((tm, w), lambda i: (i, 0))
    return pl.pallas_call(
        _out_ffn_kernel,
        grid=(m // tm,),
        in_specs=[row(d), row(SB_WIDTH), row(RW_WIDTH), _const_spec(woa.shape), _const_spec(wob.shape),
                  _const_spec((1, d)), _const_spec(wg.shape), _const_spec(wu.shape),
                  _const_spec(wd.shape), _const_spec((1, d))],
        out_specs=row(d),
        out_shape=jax.ShapeDtypeStruct((m, d), F32),
        compiler_params=pltpu.CompilerParams(dimension_semantics=("arbitrary",),
                                             vmem_limit_bytes=VMEM_LIMIT),
        name="out_ffn",
    )(x2d, o_sb, o_rw, woa, wob, gf, wg, wu, wd, gl)


def _pad_rows(x, rows):
    return jnp.pad(x, ((0, 0), (0, rows - x.shape[1]), (0, 0)))


def _pad_lanes(x, lanes):
    return jnp.pad(x, [(0, 0)] * (x.ndim - 1) + [(0, lanes - x.shape[-1])])


def _state_to_blockdiag(wkv):
    b, h = wkv.shape[:2]
    zt = jnp.swapaxes(wkv, -1, -2).reshape(b, h // 2, 2, HEAD_DIM, HEAD_DIM)
    z = jnp.einsum("bpikv,ij->bpikjv", zt, jnp.eye(2, dtype=wkv.dtype))
    return z.reshape(b, h // 2, LANES, LANES)


def _blockdiag_to_state(z):
    b, p = z.shape[:2]
    z6 = z.reshape(b, p, 2, HEAD_DIM, 2, HEAD_DIM)
    zd = jnp.stack([z6[:, :, 0, :, 0, :], z6[:, :, 1, :, 1, :]], axis=2)
    return jnp.swapaxes(zd, -1, -2).reshape(b, 2 * p, HEAD_DIM, HEAD_DIM)


def _layer(x, k_past, v_past, wkv0, shift0, w, *, tm, tq, tb):
    b, t, d = x.shape
    x2d = x.reshape(b * t, d)
    to_feature_major = lambda a: jnp.transpose(a, (0, 2, 3, 1)).reshape(b, SB_WIDTH, a.shape[1])
    from_feature_major = lambda a: jnp.transpose(a.reshape(b, SB_HEADS, HEAD_DIM, a.shape[2]), (0, 3, 1, 2))
    if k_past is None:
        q, k, v, rkv, lora = _norm_proj(x, w["norm_mix_g"], w["w_q"], w["w_kv_t"], w["w_rkv"], w["w_lora"],
                                        tm, feature_major_kv=True)
        o_sb = _sb_attention(q.reshape(b, t, SB_WIDTH), k, v, k, v, tq=tq, causal_prefix=True,
                             own_transposed=True, prefix_transposed=True)
        k_heads, v_heads = from_feature_major(k), from_feature_major(v)
    else:
        q, k, v, rkv, lora = _norm_proj(x, w["norm_mix_g"], w["w_q"], w["w_kv"], w["w_rkv"], w["w_lora"],
                                        tm, feature_major_kv=False)
        q3, k3, v3 = (a.reshape(b, t, SB_WIDTH) for a in (q, k, v))
        o_sb = _sb_attention(q3, _pad_rows(k3, KEY_BLOCK), _pad_rows(v3, KEY_BLOCK),
                             to_feature_major(k_past), to_feature_major(v_past), tq=tq,
                             causal_prefix=False, own_transposed=False, prefix_transposed=True)
        k_heads, v_heads = (a.reshape(b, t, SB_HEADS, HEAD_DIM) for a in (k, v))

    rkv3 = rkv.reshape(b, t, 3 * RW_WIDTH)
    lora3 = lora.reshape(b, t, LORA_PAD)
    t_pad = -(-t // tb) * tb
    o_rw, z_out = _rwkv(_pad_rows(rkv3, t_pad), _pad_rows(lora3, t_pad),
                        shift0[..., :3 * RW_WIDTH], _pad_lanes(shift0[..., 3 * RW_WIDTH:], LORA_PAD),
                        _state_to_blockdiag(wkv0), w, tb=tb, t_valid=min(t, tb))
    o_rw = o_rw[:, :t]
    shift_last = jnp.concatenate([rkv3[:, t - 1:, :], lora3[:, t - 1:, :LORA_WIDTH]], axis=-1)

    y = _out_ffn(x2d, o_sb.reshape(b * t, SB_WIDTH), o_rw.reshape(b * t, RW_WIDTH),
                 w["w_out_sb"], w["w_out_rw"], w["norm_ffn_g"], w["w_gate"], w["w_up"], w["w_down"],
                 w["norm_final_g"], tm)
    return y.reshape(b, t, d), k_heads, v_heads, _blockdiag_to_state(z_out), shift_last


def kernel(x_prompt, x_sample, cache_k, cache_v, state_wkv, state_shift, norm_mix_g, w_in, mu_shift, w0,
           w_decay_up, a0, w_aaa_up, w_gate_up, k_k, k_a, r_k, ln_x_w, ln_x_b, w_out, norm_ffn_g, w_gate,
           w_up, w_down, norm_final_g):
    assert w_in.shape[0] == 1, "single-layer trunk"
    l = 0
    rw3 = 3 * RW_WIDTH
    lora_rows = lambda m, lo, hi: jnp.pad(m, ((lo, LORA_PAD - hi), (0, 0)))
    w = {
        "norm_mix_g": norm_mix_g[l][None, :],
        "w_q": w_in[l][:, :SB_WIDTH].astype(BF16),
        "w_kv": w_in[l][:, SB_WIDTH:3 * SB_WIDTH].astype(BF16),
        "w_kv_t": w_in[l][:, SB_WIDTH:3 * SB_WIDTH].T.astype(BF16),
        "w_rkv": w_in[l][:, 3 * SB_WIDTH:3 * SB_WIDTH + rw3].astype(BF16),
        "w_lora": _pad_lanes(w_in[l][:, 3 * SB_WIDTH + rw3:], LORA_PAD).astype(BF16),
        "mu_rkv": mu_shift[l][None, :rw3],
        "mu_lora": _pad_lanes(mu_shift[l][None, rw3:], LORA_PAD),
        "w0": w0[l][None, :],
        "wd": lora_rows(w_decay_up[l], 0, LORA_DECAY),
        "a0": a0[l][None, :],
        "wa": lora_rows(w_aaa_up[l], LORA_DECAY, LORA_DECAY + LORA_AAA),
        "wg": lora_rows(w_gate_up[l], LORA_DECAY + LORA_AAA, LORA_WIDTH),
        "k_k": k_k[l][None, :],
        "k_a": k_a[l][None, :],
        "r_k": r_k[l].reshape(1, RW_WIDTH),
        "ln_w": ln_x_w[l][None, :],
        "ln_b": ln_x_b[l][None, :],
        "w_out_sb": w_out[l][:SB_WIDTH].astype(BF16),
        "w_out_rw": w_out[l][SB_WIDTH:].astype(BF16),
        "norm_ffn_g": norm_ffn_g[l][None, :],
        "w_gate": w_gate[l].astype(BF16),
        "w_up": w_up[l].astype(BF16),
        "w_down": w_down[l].astype(BF16),
        "norm_final_g": norm_final_g[None, :],
    }
    bp = x_prompt.shape[0]
    heads = RW_WIDTH // HEAD_DIM
    wkv_zero = jnp.zeros((bp, heads, HEAD_DIM, HEAD_DIM), x_prompt.dtype)
    shift_zero = jnp.zeros((bp, 1, rw3 + LORA_WIDTH), x_prompt.dtype)
    yp, k1, v1, s1, sh1 = _layer(x_prompt, None, None, wkv_zero, shift_zero, w, tm=512, tq=128, tb=512)
    ys, k2, v2, s2, sh2 = _layer(x_sample, cache_k[l], cache_v[l], state_wkv[l], state_shift[l], w,
                                 tm=256, tq=x_sample.shape[1], tb=CHUNK)
    return (yp, ys, k1[None], v1[None], s1[None], sh1[None], k2[None], v2[None], s2[None], sh2[None])
```

```python
import functools
import math

import jax
import jax.numpy as jnp
from jax import lax
from jax.experimental import pallas as pl
from jax.experimental.pallas import tpu as pltpu

F32 = jnp.float32
BF16 = jnp.bfloat16

LANES = 128
HEAD_DIM = 64
SB_WIDTH = 512
SB_HEADS = SB_WIDTH // HEAD_DIM
RW_WIDTH = 512
LORA_DECAY = 64
LORA_AAA = 64
LORA_GATE = 160
LORA_WIDTH = LORA_DECAY + LORA_AAA + LORA_GATE
LORA_PAD = 384
RMS_EPS = 1e-6
GN_EPS = 64e-5
DECAY_SCALE = math.exp(-0.5)
KEY_BLOCK = 128
SB_DEAD_LOG = -104.0
CHUNK = 64
VMEM_LIMIT = 56 * 1024 * 1024


def _dot(a, b):
    return jnp.dot(a, b, preferred_element_type=F32)


def _split2(x):
    hi = x.astype(BF16)
    lo = (x - hi.astype(F32)).astype(BF16)
    return hi, lo


def _bdot(a, b):
    return jnp.einsum("bij,bjk->bik", a, b, preferred_element_type=F32)


def _bdot_nt(a, b):
    return jnp.einsum("bik,bjk->bij", a, b, preferred_element_type=F32)


def _bmm1(a, b):
    return _bdot(a.astype(BF16), b.astype(BF16))


def _bmm3(a, b):
    ah, al = _split2(a)
    bh, bl = _split2(b)
    return _bdot(ah, bh) + (_bdot(ah, bl) + _bdot(al, bh))


def _mm_exact_rhs(x, m):
    hi, lo = _split2(x)
    return _dot(hi, m) + _dot(lo, m)


def _sigmoid(x):
    return 0.5 * jnp.tanh(0.5 * x) + 0.5


def _rmsnorm(x, g):
    return x * lax.rsqrt(jnp.mean(x * x, axis=-1, keepdims=True) + RMS_EPS) * g


def _const_spec(shape):
    return pl.BlockSpec(shape, lambda *_: (0,) * len(shape), pipeline_mode=pl.Buffered(1))


def _norm_proj_kernel(x_ref, g_ref, wq_ref, wkv_ref, wrkv_ref, wl_ref,
                      q_ref, k_ref, v_ref, rkv_ref, l_ref, *, feature_major_kv):
    hb = _rmsnorm(x_ref[...], g_ref[...]).astype(BF16)
    q_ref[...] = _dot(hb, wq_ref[...])
    if feature_major_kv:
        kv = lax.dot_general(wkv_ref[...], hb, (((1,), (1,)), ((), ())), preferred_element_type=F32)
        k_ref[0] = kv[:SB_WIDTH]
        v_ref[0] = kv[SB_WIDTH:]
    else:
        kv = _dot(hb, wkv_ref[...])
        k_ref[...] = kv[:, :SB_WIDTH]
        v_ref[...] = kv[:, SB_WIDTH:]
    rkv_ref[...] = _dot(hb, wrkv_ref[...])
    l_ref[...] = _dot(hb, wl_ref[...])


def _norm_proj(x, g, wq, wkv, wrkv, wl, tm, feature_major_kv):
    b, t, d = x.shape
    m = b * t
    row = lambda w: pl.BlockSpec((tm, w), lambda i: (i, 0))
    if feature_major_kv:
        per_batch = t // tm
        kv_spec = pl.BlockSpec((1, SB_WIDTH, tm), lambda i: (i // per_batch, 0, i % per_batch))
        kv_shape = jax.ShapeDtypeStruct((b, SB_WIDTH, t), F32)
    else:
        kv_spec, kv_shape = row(SB_WIDTH), jax.ShapeDtypeStruct((m, SB_WIDTH), F32)
    return pl.pallas_call(
        functools.partial(_norm_proj_kernel, feature_major_kv=feature_major_kv),
        grid=(m // tm,),
        in_specs=[row(d), _const_spec((1, d)), _const_spec(wq.shape), _const_spec(wkv.shape),
                  _const_spec(wrkv.shape), _const_spec(wl.shape)],
        out_specs=[row(SB_WIDTH), kv_spec, kv_spec, row(3 * RW_WIDTH), row(LORA_PAD)],
        out_shape=[jax.ShapeDtypeStruct((m, SB_WIDTH), F32), kv_shape, kv_shape,
                   jax.ShapeDtypeStruct((m, 3 * RW_WIDTH), F32), jax.ShapeDtypeStruct((m, LORA_PAD), F32)],
        compiler_params=pltpu.CompilerParams(dimension_semantics=("arbitrary",),
                                             vmem_limit_bytes=VMEM_LIMIT),
        name="norm_proj",
    )(x.reshape(m, d), g, wq, wkv, wrkv, wl)


def _sb_kernel(q_ref, kd_ref, vd_ref, kp_ref, vp_ref, o_ref, *, tq, causal_prefix, own_transposed,
               n_prefix):
    pairs = SB_WIDTH // LANES
    by_pair = lambda x: jnp.stack([x[:, p * LANES:(p + 1) * LANES] for p in range(pairs)])
    q = by_pair(q_ref[0] * (1.0 / math.sqrt(HEAD_DIM)))
    first = lax.broadcasted_iota(jnp.int32, (pairs, tq, LANES), 2) < HEAD_DIM
    qe = jnp.concatenate([jnp.where(first, q, 0.0), jnp.where(first, 0.0, q)], axis=1).astype(BF16)

    row = lax.broadcasted_iota(jnp.int32, (pairs, 2 * tq, KEY_BLOCK), 1)
    col = lax.broadcasted_iota(jnp.int32, (pairs, 2 * tq, KEY_BLOCK), 2)
    causal = col < jnp.where(row >= tq, row - tq, row)

    jj = lax.broadcasted_iota(jnp.int32, (KEY_BLOCK, 2 * KEY_BLOCK), 0)
    ss = lax.broadcasted_iota(jnp.int32, (KEY_BLOCK, 2 * KEY_BLOCK), 1)
    suffix = jnp.where((jj > ss) | (ss >= KEY_BLOCK), 1.0, 0.0).astype(BF16)
    suffix2 = jnp.concatenate([suffix, suffix], axis=0)

    def prefix(ref, start, n_keys):
        return ref[0, :, pl.ds(start, n_keys)].reshape(pairs, LANES, n_keys)

    def suffix_sums(log1mb):
        hi_lo = jnp.concatenate(_split2(log1mb.reshape(pairs * 2 * tq, KEY_BLOCK)), axis=1)
        return _dot(hi_lo, suffix2).reshape(pairs, 2 * tq, 2 * KEY_BLOCK)

    def softplus(z):
        return jnp.maximum(z, 0.0) + jnp.log(1.0 + jnp.exp(-jnp.abs(z)))

    def block(kb, vb, c, o, mask, transposed):
        z = (_bdot if transposed else _bdot_nt)(qe, kb.astype(BF16))
        log1mb = -softplus(z)
        if mask is not None:
            log1mb = jnp.where(mask, log1mb, 0.0)
        sums = suffix_sums(log1mb)
        w = jnp.exp(z + log1mb + (sums[:, :, :KEY_BLOCK] + c))
        if mask is not None:
            w = jnp.where(mask, w, 0.0)
        o = o + (_bdot_nt if transposed else _bdot)(w.astype(BF16), vb.astype(BF16))
        return c + sums[:, :, KEY_BLOCK:], o

    def two_blocks(kb, vb, c, o):
        z = _bdot(qe, kb.astype(BF16))
        log1mb = -softplus(z)
        new = suffix_sums(log1mb[:, :, KEY_BLOCK:])
        old = suffix_sums(log1mb[:, :, :KEY_BLOCK])
        c_mid = c + new[:, :, KEY_BLOCK:]
        after = jnp.concatenate([old[:, :, :KEY_BLOCK] + c_mid, new[:, :, :KEY_BLOCK] + c], axis=2)
        w = jnp.exp(z + log1mb + after)
        return c_mid + old[:, :, KEY_BLOCK:], o + _bdot_nt(w.astype(BF16), vb.astype(BF16))

    zeros = jnp.zeros((pairs, 2 * tq, LANES), F32)
    own = (lambda ref: ref[0].reshape(pairs, LANES, KEY_BLOCK)) if own_transposed else (lambda ref: by_pair(ref[0]))
    c, o = block(own(kd_ref), own(vd_ref), zeros, zeros, causal, own_transposed)

    n = pl.program_id(1) if causal_prefix else n_prefix

    alive = lambda c: (jnp.max(c) >= SB_DEAD_LOG).astype(jnp.int32)

    def live(carry):
        return (carry[0] < (n >> 1)) & (carry[1] > 0)

    def body(carry):
        it, _, c, o = carry
        start = pl.multiple_of((n - 2 - 2 * it) * KEY_BLOCK, KEY_BLOCK)
        c, o = two_blocks(prefix(kp_ref, start, 2 * KEY_BLOCK), prefix(vp_ref, start, 2 * KEY_BLOCK), c, o)
        return it + 1, alive(c), c, o

    _, still, c, o = lax.while_loop(live, body, (jnp.int32(0), jnp.int32(1), c, o))
    c, o = lax.cond(((n & 1) == 1) & (still > 0),
                    lambda c, o: block(prefix(kp_ref, 0, KEY_BLOCK), prefix(vp_ref, 0, KEY_BLOCK), c, o, None, True),
                    lambda c, o: (c, o), c, o)
    o = jnp.where(first, o[:, :tq], o[:, tq:])
    o_ref[0] = jnp.concatenate([o[p] for p in range(pairs)], axis=1)


def _sb_attention(q, kd, vd, kp, vp, *, tq, causal_prefix, own_transposed):
    b, t, _ = q.shape
    nq = t // tq
    tp = kp.shape[2]
    blk = lambda rows: pl.BlockSpec((1, rows, SB_WIDTH), lambda bi, i: (bi, i, 0))
    own = pl.BlockSpec((1, SB_WIDTH, KEY_BLOCK), lambda bi, i: (bi, 0, i)) if own_transposed else blk(KEY_BLOCK)
    pre = pl.BlockSpec((1,) + kp.shape[1:], lambda bi, i: (bi, 0, 0), pipeline_mode=pl.Buffered(1))
    return pl.pallas_call(
        functools.partial(_sb_kernel, tq=tq, causal_prefix=causal_prefix, own_transposed=own_transposed,
                          n_prefix=tp // KEY_BLOCK),
        grid=(b, nq),
        in_specs=[blk(tq), own, own, pre, pre],
        out_specs=blk(tq),
        out_shape=jax.ShapeDtypeStruct(q.shape, F32),
        compiler_params=pltpu.CompilerParams(dimension_semantics=("arbitrary",) * 2,
                                             vmem_limit_bytes=VMEM_LIMIT),
        name="stickbreak",
    )(q, kd, vd, kp, vp)


def _rwkv_kernel(x_ref, l_ref, sx_ref, sl_ref, z0_ref, mux_ref, mul_ref, w0_ref, wd_ref, a0_ref, wa_ref,
                 wg_ref, kkw_ref, kaw_ref, rkw_ref, lnw_ref, lnb_ref,
                 o_ref, zout_ref,
                 px_s, pl_s, y_s, z_s, qe_s, yl_s, phi_s, psi_s,
                 *, tb, t_valid):
    t = pl.program_id(1)
    pairs = RW_WIDTH // LANES

    @pl.when(t == 0)
    def _():
        px_s[...] = sx_ref[0]
        pl_s[...] = sl_ref[0]
        z_s[...] = z0_ref[0]

    def shift(x_ref, carry_s, mu_ref):
        x = x_ref[0]
        prev = pltpu.roll(x, 1, 0)
        prev = jnp.where(lax.broadcasted_iota(jnp.int32, x.shape, 0) == 0, carry_s[...], prev)
        carry_s[...] = x[tb - 1:tb, :]
        return x + (prev - x) * mu_ref[...]

    xs = shift(x_ref, px_s, mux_ref)
    xr, xk, xv = xs[:, :RW_WIDTH], xs[:, RW_WIDTH:2 * RW_WIDTH], xs[:, 2 * RW_WIDTH:]
    xl = shift(l_ref, pl_s, mul_ref)

    hr = lax.broadcasted_iota(jnp.int32, (LANES, LANES), 0) // HEAD_DIM
    hc = lax.broadcasted_iota(jnp.int32, (LANES, LANES), 1) // HEAD_DIM
    head_ones = jnp.where(hr == hc, 1.0, 0.0).astype(BF16)
    head_sum = lambda x: jnp.concatenate(
        [_mm_exact_rhs(x[:, p * LANES:(p + 1) * LANES], head_ones) for p in range(pairs)], axis=1)

    lora = lambda x, w_ref: _dot(x.astype(BF16), w_ref[...].astype(BF16))
    log_w = -DECAY_SCALE * _sigmoid(w0_ref[...] + lora(jnp.tanh(xl), wd_ref))
    a = _sigmoid(a0_ref[...] + lora(xl, wa_ref))
    kk = xk * kkw_ref[...]
    kk = kk * lax.rsqrt(jnp.maximum(head_sum(kk * kk), 1e-24))
    k2 = xk * (1.0 + (a - 1.0) * kaw_ref[...])
    b = kk * a
    if t_valid < tb:
        valid = lax.broadcasted_iota(jnp.int32, (tb, RW_WIDTH), 0) < t_valid
        log_w = jnp.where(valid, log_w, 0.0)
        kk = jnp.where(valid, kk, 0.0)
        b = jnp.where(valid, b, 0.0)
        k2 = jnp.where(valid, k2, 0.0)
        xv = jnp.where(valid, xv, 0.0)
    nc = tb // CHUNK
    nb = pairs * nc
    e2 = 2 * CHUNK
    by_chunk = lambda x: jnp.stack(
        [x[:, p * LANES:(p + 1) * LANES] for p in range(pairs)]).reshape(nb, CHUNK, LANES)
    ti = lax.broadcasted_iota(jnp.int32, (nb, CHUNK, CHUNK), 1)
    tj = lax.broadcasted_iota(jnp.int32, (nb, CHUNK, CHUNK), 2)
    tri = jnp.where(tj <= ti, 1.0, 0.0).astype(BF16)
    lw3 = by_chunk(log_w)
    lw_hi, lw_lo = _split2(lw3)
    cum3 = _bdot(tri, lw_hi) + _bdot(tri, lw_lo)
    tot = cum3[:, CHUNK - 1:CHUNK, :]
    first = lax.broadcasted_iota(jnp.int32, (nb, CHUNK, LANES), 2) < HEAD_DIM
    expand = lambda x: jnp.concatenate([jnp.where(first, x, 0.0), jnp.where(first, 0.0, x)], axis=1)
    e_neg = jnp.exp(-cum3)
    e_tail = jnp.exp(tot - cum3)
    kk3, b3, k3, v3 = by_chunk(kk), by_chunk(b), by_chunk(k2), by_chunk(xv)
    kkt = kk3 * jnp.exp(cum3 - lw3)
    rt = by_chunk(xr) * jnp.exp(cum3)

    wr = lax.broadcasted_iota(jnp.int32, (nb, CHUNK, e2), 1)
    wc = lax.broadcasted_iota(jnp.int32, (nb, CHUNK, e2), 2)
    head0 = wc < CHUNK
    ws = jnp.where(head0, wc, wc - CHUNK)
    strict, incl, eye_w = ws < wr, ws <= wr, ws == wr
    diag2 = lambda x: jnp.concatenate([jnp.where(head0, x, 0.0), jnp.where(head0, 0.0, x)], axis=1)

    gram = _bdot_nt(jnp.concatenate([kkt, rt], axis=1).astype(BF16),
                    jnp.concatenate([expand(k3 * e_neg), expand(b3 * e_neg)], axis=1).astype(BF16))
    ak = jnp.where(strict, gram[:, :CHUNK, :e2], 0.0)
    ab = jnp.where(strict, gram[:, :CHUNK, e2:], 0.0)
    rk = jnp.where(incl, gram[:, CHUNK:, :e2], 0.0)
    rb = jnp.where(incl, gram[:, CHUNK:, e2:], 0.0)

    eye2 = jnp.where(eye_w, 1.0, 0.0)
    tinv = eye2 - ab
    ab_d = diag2(ab)
    pw = _bmm1(ab, ab_d)
    n_sq = CHUNK.bit_length() - 2
    for s in range(n_sq):
        if s + 1 < n_sq:
            both = _bmm1(pw, jnp.concatenate([diag2(tinv), diag2(pw)], axis=2))
            tinv = tinv + both[:, :, :e2]
            pw = both[:, :, e2:]
        else:
            tinv = tinv + _bmm1(pw, diag2(tinv))
    resid = eye2 - tinv - _bmm3(ab, diag2(tinv))
    tinv = tinv + _bmm1(tinv, diag2(resid))

    ve = expand(v3)
    akve = _bmm1(ak, ve)
    wu = _bmm1(tinv, jnp.concatenate([expand(kkt), expand(akve)], axis=2))
    w_, u_loc = wu[:, :, :LANES], wu[:, :, LANES:]
    yq = _bmm1(jnp.concatenate([rk, -rb], axis=2),
               jnp.concatenate([jnp.concatenate([ve, jnp.zeros_like(ve)], axis=2),
                                jnp.concatenate([expand(u_loc), expand(w_)], axis=2)], axis=1))
    per_pair = lambda x: x.reshape((pairs, nc) + x.shape[1:])
    qe_s[...] = per_pair(rt + yq[:, :, LANES:])
    yl_s[...] = per_pair(yq[:, :, :LANES])
    zp = _bmm1(jnp.swapaxes(jnp.concatenate([k3 * e_tail, -(b3 * e_tail)], axis=1), 1, 2),
               jnp.concatenate([jnp.concatenate([v3, jnp.zeros_like(v3)], axis=2),
                                jnp.concatenate([u_loc, w_], axis=2)], axis=1))
    zr = lax.broadcasted_iota(jnp.int32, (nb, HEAD_DIM, LANES), 1)
    zc = lax.broadcasted_iota(jnp.int32, (nb, HEAD_DIM, LANES), 2)
    zhead0 = zc < HEAD_DIM
    own_block = lambda x: jnp.where(zhead0, x[:, :HEAD_DIM], x[:, HEAD_DIM:])
    decay = jnp.where(jnp.where(zhead0, zc, zc - HEAD_DIM) == zr, jnp.exp(tot), 0.0)
    phi_s[...] = per_pair(decay + own_block(zp[:, :, LANES:]))
    psi_s[...] = per_pair(own_block(zp[:, :, :LANES]))
    shead0 = lax.broadcasted_iota(jnp.int32, (pairs, HEAD_DIM, LANES), 2) < HEAD_DIM

    def advance(c, _):
        rows = pl.ds(pl.multiple_of(c * CHUNK, CHUNK), CHUNK)
        z = z_s[...]
        y_s[:, rows, :] = _bmm1(qe_s[:, c], z) + yl_s[:, c]
        z_new = _bmm3(phi_s[:, c], z) + psi_s[:, c]
        z_s[...] = jnp.concatenate([jnp.where(shead0, z_new, 0.0), jnp.where(shead0, 0.0, z_new)], axis=1)
        return 0

    lax.fori_loop(0, nc, advance, 0, unroll=min(nc, 2))

    y = jnp.concatenate([y_s[p] for p in range(pairs)], axis=1)
    d = y - head_sum(y) * (1.0 / HEAD_DIM)
    var = head_sum(d * d) * (1.0 / HEAD_DIM)
    o = d * lax.rsqrt(var + GN_EPS) * lnw_ref[...] + lnb_ref[...]
    bonus = head_sum(xr * k2 * rkw_ref[...]) * xv
    o_ref[0] = (o + bonus) * lora(_sigmoid(xl), wg_ref)

    @pl.when(t == pl.num_programs(1) - 1)
    def _():
        zout_ref[0] = z_s[...]


def _rwkv(rkv, lora, shift_rkv, shift_lora, z0, prm, *, tb, t_valid):
    b, t, _ = rkv.shape
    nt = t // tb
    nc = tb // CHUNK
    pairs = RW_WIDTH // LANES
    rows = lambda w: pl.BlockSpec((1, tb, w), lambda bi, ti: (bi, ti, 0))
    first = lambda w: pl.BlockSpec((1, 1, w), lambda bi, ti: (bi, 0, 0))
    state = pl.BlockSpec((1, pairs, LANES, LANES), lambda bi, ti: (bi, 0, 0, 0))
    vec = _const_spec((1, RW_WIDTH))
    mat = _const_spec((LORA_PAD, RW_WIDTH))
    return pl.pallas_call(
        functools.partial(_rwkv_kernel, tb=tb, t_valid=t_valid),
        grid=(b, nt),
        in_specs=[rows(3 * RW_WIDTH), rows(LORA_PAD), first(3 * RW_WIDTH), first(LORA_PAD), state,
                  _const_spec((1, 3 * RW_WIDTH)), _const_spec((1, LORA_PAD)),
                  vec, mat, vec, mat, mat, vec, vec, vec, vec, vec],
        out_specs=[rows(RW_WIDTH), state],
        out_shape=[jax.ShapeDtypeStruct((b, t, RW_WIDTH), F32),
                   jax.ShapeDtypeStruct((b, pairs, LANES, LANES), F32)],
        scratch_shapes=[pltpu.VMEM((1, 3 * RW_WIDTH), F32), pltpu.VMEM((1, LORA_PAD), F32),
                        pltpu.VMEM((pairs, tb, LANES), F32), pltpu.VMEM((pairs, LANES, LANES), F32)]
        + [pltpu.VMEM((pairs, nc, CHUNK, LANES), F32)] * 2
        + [pltpu.VMEM((pairs, nc, HEAD_DIM, LANES), F32)] * 2,
        compiler_params=pltpu.CompilerParams(dimension_semantics=("arbitrary",) * 2,
                                             vmem_limit_bytes=VMEM_LIMIT),
        name="rwkv7",
    )(rkv, lora, shift_rkv, shift_lora, z0, prm["mu_rkv"], prm["mu_lora"],
      prm["w0"], prm["wd"], prm["a0"], prm["wa"], prm["wg"],
      prm["k_k"], prm["k_a"], prm["r_k"], prm["ln_w"], prm["ln_b"])


def _out_ffn_kernel(x_ref, sb_ref, rw_ref, woa_ref, wob_ref, gf_ref, wg_ref, wu_ref, wd_ref, gl_ref,
                    y_ref):
    x1 = x_ref[...] + _dot(sb_ref[...].astype(BF16), woa_ref[...]) \
        + _dot(rw_ref[...].astype(BF16), wob_ref[...])
    h2 = _rmsnorm(x1, gf_ref[...]).astype(BF16)
    gate = _dot(h2, wg_ref[...])
    act = gate * _sigmoid(gate) * _dot(h2, wu_ref[...])
    x2 = x1 + _dot(act.astype(BF16), wd_ref[...])
    y_ref[...] = _rmsnorm(x2, gl_ref[...])


def _out_ffn(x2d, o_sb, o_rw, woa, wob, gf, wg, wu, wd, gl, tm):
    m, d = x2d.shape
    row = lambda w: pl.BlockSpec((tm, w), lambda i: (i, 0))
    return pl.pallas_call(
        _out_ffn_kernel,
        grid=(m // tm,),
        in_specs=[row(d), row(SB_WIDTH), row(RW_WIDTH), _const_spec(woa.shape), _const_spec(wob.shape),
                  _const_spec((1, d)), _const_spec(wg.shape), _const_spec(wu.shape),
                  _const_spec(wd.shape), _const_spec((1, d))],
        out_specs=row(d),
        out_shape=jax.ShapeDtypeStruct((m, d), F32),
        compiler_params=pltpu.CompilerParams(dimension_semantics=("arbitrary",),
                                             vmem_limit_bytes=VMEM_LIMIT),
        name="out_ffn",
    )(x2d, o_sb, o_rw, woa, wob, gf, wg, wu, wd, gl)


def _pad_rows(x, rows):
    return jnp.pad(x, ((0, 0), (0, rows - x.shape[1]), (0, 0)))


def _pad_lanes(x, lanes):
    return jnp.pad(x, [(0, 0)] * (x.ndim - 1) + [(0, lanes - x.shape[-1])])


def _state_to_blockdiag(wkv):
    b, h = wkv.shape[:2]
    zt = jnp.swapaxes(wkv, -1, -2).reshape(b, h // 2, 2, HEAD_DIM, HEAD_DIM)
    z = jnp.einsum("bpikv,ij->bpikjv", zt, jnp.eye(2, dtype=wkv.dtype))
    return z.reshape(b, h // 2, LANES, LANES)


def _blockdiag_to_state(z):
    b, p = z.shape[:2]
    z6 = z.reshape(b, p, 2, HEAD_DIM, 2, HEAD_DIM)
    zd = jnp.stack([z6[:, :, 0, :, 0, :], z6[:, :, 1, :, 1, :]], axis=2)
    return jnp.swapaxes(zd, -1, -2).reshape(b, 2 * p, HEAD_DIM, HEAD_DIM)


def _layer(x, k_past, v_past, wkv0, shift0, w, *, tm, tq, tb):
    b, t, d = x.shape
    x2d = x.reshape(b * t, d)
    to_feature_major = lambda a: jnp.transpose(a, (0, 2, 3, 1)).reshape(b, SB_WIDTH, a.shape[1])
    from_feature_major = lambda a: jnp.transpose(a.reshape(b, SB_HEADS, HEAD_DIM, a.shape[2]), (0, 3, 1, 2))
    if k_past is None:
        q, k, v, rkv, lora = _norm_proj(x, w["norm_mix_g"], w["w_q"], w["w_kv_t"], w["w_rkv"], w["w_lora"],
                                        tm, feature_major_kv=True)
        o_sb = _sb_attention(q.reshape(b, t, SB_WIDTH), k, v, k, v, tq=tq, causal_prefix=True,
                             own_transposed=True)
        k_heads, v_heads = from_feature_major(k), from_feature_major(v)
    else:
        q, k, v, rkv, lora = _norm_proj(x, w["norm_mix_g"], w["w_q"], w["w_kv"], w["w_rkv"], w["w_lora"],
                                        tm, feature_major_kv=False)
        q3, k3, v3 = (a.reshape(b, t, SB_WIDTH) for a in (q, k, v))
        o_sb = _sb_attention(q3, _pad_rows(k3, KEY_BLOCK), _pad_rows(v3, KEY_BLOCK),
                             to_feature_major(k_past), to_feature_major(v_past), tq=tq,
                             causal_prefix=False, own_transposed=False)
        k_heads, v_heads = (a.reshape(b, t, SB_HEADS, HEAD_DIM) for a in (k, v))

    rkv3 = rkv.reshape(b, t, 3 * RW_WIDTH)
    lora3 = lora.reshape(b, t, LORA_PAD)
    t_pad = -(-t // tb) * tb
    o_rw, z_out = _rwkv(_pad_rows(rkv3, t_pad), _pad_rows(lora3, t_pad),
                        shift0[..., :3 * RW_WIDTH], _pad_lanes(shift0[..., 3 * RW_WIDTH:], LORA_PAD),
                        _state_to_blockdiag(wkv0), w, tb=tb, t_valid=min(t, tb))
    o_rw = o_rw[:, :t]
    shift_last = jnp.concatenate([rkv3[:, t - 1:, :], lora3[:, t - 1:, :LORA_WIDTH]], axis=-1)

    y = _out_ffn(x2d, o_sb.reshape(b * t, SB_WIDTH), o_rw.reshape(b * t, RW_WIDTH),
                 w["w_out_sb"], w["w_out_rw"], w["norm_ffn_g"], w["w_gate"], w["w_up"], w["w_down"],
                 w["norm_final_g"], tm)
    return y.reshape(b, t, d), k_heads, v_heads, _blockdiag_to_state(z_out), shift_last


def kernel(x_prompt, x_sample, cache_k, cache_v, state_wkv, state_shift, norm_mix_g, w_in, mu_shift, w0,
           w_decay_up, a0, w_aaa_up, w_gate_up, k_k, k_a, r_k, ln_x_w, ln_x_b, w_out, norm_ffn_g, w_gate,
           w_up, w_down, norm_final_g):
    assert w_in.shape[0] == 1, "single-layer trunk"
    l = 0
    rw3 = 3 * RW_WIDTH
    lora_rows = lambda m, lo, hi: jnp.pad(m, ((lo, LORA_PAD - hi), (0, 0)))
    w = {
        "norm_mix_g": norm_mix_g[l][None, :],
        "w_q": w_in[l][:, :SB_WIDTH].astype(BF16),
        "w_kv": w_in[l][:, SB_WIDTH:3 * SB_WIDTH].astype(BF16),
        "w_kv_t": w_in[l][:, SB_WIDTH:3 * SB_WIDTH].T.astype(BF16),
        "w_rkv": w_in[l][:, 3 * SB_WIDTH:3 * SB_WIDTH + rw3].astype(BF16),
        "w_lora": _pad_lanes(w_in[l][:, 3 * SB_WIDTH + rw3:], LORA_PAD).astype(BF16),
        "mu_rkv": mu_shift[l][None, :rw3],
        "mu_lora": _pad_lanes(mu_shift[l][None, rw3:], LORA_PAD),
        "w0": w0[l][None, :],
        "wd": lora_rows(w_decay_up[l], 0, LORA_DECAY),
        "a0": a0[l][None, :],
        "wa": lora_rows(w_aaa_up[l], LORA_DECAY, LORA_DECAY + LORA_AAA),
        "wg": lora_rows(w_gate_up[l], LORA_DECAY + LORA_AAA, LORA_WIDTH),
        "k_k": k_k[l][None, :],
        "k_a": k_a[l][None, :],
        "r_k": r_k[l].reshape(1, RW_WIDTH),
        "ln_w": ln_x_w[l][None, :],
        "ln_b": ln_x_b[l][None, :],
        "w_out_sb": w_out[l][:SB_WIDTH].astype(BF16),
        "w_out_rw": w_out[l][SB_WIDTH:].astype(BF16),
        "norm_ffn_g": norm_ffn_g[l][None, :],
        "w_gate": w_gate[l].astype(BF16),
        "w_up": w_up[l].astype(BF16),
        "w_down": w_down[l].astype(BF16),
        "norm_final_g": norm_final_g[None, :],
    }
    bp = x_prompt.shape[0]
    heads = RW_WIDTH // HEAD_DIM
    wkv_zero = jnp.zeros((bp, heads, HEAD_DIM, HEAD_DIM), x_prompt.dtype)
    shift_zero = jnp.zeros((bp, 1, rw3 + LORA_WIDTH), x_prompt.dtype)
    yp, k1, v1, s1, sh1 = _layer(x_prompt, None, None, wkv_zero, shift_zero, w, tm=512, tq=128, tb=512)
    ys, k2, v2, s2, sh2 = _layer(x_sample, cache_k[l], cache_v[l], state_wkv[l], state_shift[l], w,
                                 tm=256, tq=x_sample.shape[1], tb=CHUNK)
    return (yp, ys, k1[None], v1[None], s1[None], sh1[None], k2[None], v2[None], s2[None], sh2[None])
```

```python
import functools
import math

import jax
import jax.numpy as jnp
from jax import lax
from jax.experimental import pallas as pl
from jax.experimental.pallas import tpu as pltpu

F32 = jnp.float32
BF16 = jnp.bfloat16

LANES = 128
HEAD_DIM = 64
SB_WIDTH = 512
SB_HEADS = SB_WIDTH // HEAD_DIM
RW_WIDTH = 512
LORA_DECAY = 64
LORA_AAA = 64
LORA_GATE = 160
LORA_WIDTH = LORA_DECAY + LORA_AAA + LORA_GATE
LORA_PAD = 384
RMS_EPS = 1e-6
GN_EPS = 64e-5
DECAY_SCALE = math.exp(-0.5)
KEY_BLOCK = 128
SB_DEAD_LOG = -104.0
CHUNK = 64
VMEM_LIMIT = 56 * 1024 * 1024


def _dot(a, b):
    return jnp.dot(a, b, preferred_element_type=F32)


def _split2(x):
    hi = x.astype(BF16)
    lo = (x - hi.astype(F32)).astype(BF16)
    return hi, lo


def _bdot(a, b):
    return jnp.einsum("bij,bjk->bik", a, b, preferred_element_type=F32)


def _bdot_nt(a, b):
    return jnp.einsum("bik,bjk->bij", a, b, preferred_element_type=F32)


def _bmm1(a, b):
    return _bdot(a.astype(BF16), b.astype(BF16))


def _bmm3(a, b):
    ah, al = _split2(a)
    bh, bl = _split2(b)
    return _bdot(ah, bh) + (_bdot(ah, bl) + _bdot(al, bh))


def _mm_exact_rhs(x, m, parts=2, transposed=False):
    contract = (((1,), (1 if transposed else 0,)), ((), ()))
    acc = None
    for _ in range(parts):
        part = x.astype(BF16)
        prod = lax.dot_general(part, m, contract, preferred_element_type=F32)
        acc = prod if acc is None else acc + prod
        x = x - part.astype(F32)
    return acc


def _sigmoid(x):
    return 0.5 * jnp.tanh(0.5 * x) + 0.5


def _rmsnorm(x, g):
    return x * lax.rsqrt(jnp.mean(x * x, axis=-1, keepdims=True) + RMS_EPS) * g


def _const_spec(shape):
    return pl.BlockSpec(shape, lambda *_: (0,) * len(shape), pipeline_mode=pl.Buffered(1))


def _norm_proj_kernel(x_ref, g_ref, wq_ref, wkv_ref, wrkv_ref, wl_ref,
                      q_ref, k_ref, v_ref, rkv_ref, l_ref, *, feature_major_kv):
    hb = _rmsnorm(x_ref[...], g_ref[...]).astype(BF16)
    q_ref[...] = _dot(hb, wq_ref[...])
    if feature_major_kv:
        kv = lax.dot_general(wkv_ref[...], hb, (((1,), (1,)), ((), ())), preferred_element_type=F32)
        k_ref[0] = kv[:SB_WIDTH]
        v_ref[0] = kv[SB_WIDTH:]
    else:
        kv = _dot(hb, wkv_ref[...])
        k_ref[...] = kv[:, :SB_WIDTH]
        v_ref[...] = kv[:, SB_WIDTH:]
    rkv_ref[...] = _dot(hb, wrkv_ref[...])
    l_ref[...] = _dot(hb, wl_ref[...])


def _norm_proj(x, g, wq, wkv, wrkv, wl, tm, feature_major_kv):
    b, t, d = x.shape
    m = b * t
    row = lambda w: pl.BlockSpec((tm, w), lambda i: (i, 0))
    if feature_major_kv:
        per_batch = t // tm
        kv_spec = pl.BlockSpec((1, SB_WIDTH, tm), lambda i: (i // per_batch, 0, i % per_batch))
        kv_shape = jax.ShapeDtypeStruct((b, SB_WIDTH, t), F32)
    else:
        kv_spec, kv_shape = row(SB_WIDTH), jax.ShapeDtypeStruct((m, SB_WIDTH), F32)
    return pl.pallas_call(
        functools.partial(_norm_proj_kernel, feature_major_kv=feature_major_kv),
        grid=(m // tm,),
        in_specs=[row(d), _const_spec((1, d)), _const_spec(wq.shape), _const_spec(wkv.shape),
                  _const_spec(wrkv.shape), _const_spec(wl.shape)],
        out_specs=[row(SB_WIDTH), kv_spec, kv_spec, row(3 * RW_WIDTH), row(LORA_PAD)],
        out_shape=[jax.ShapeDtypeStruct((m, SB_WIDTH), F32), kv_shape, kv_shape,
                   jax.ShapeDtypeStruct((m, 3 * RW_WIDTH), F32), jax.ShapeDtypeStruct((m, LORA_PAD), F32)],
        compiler_params=pltpu.CompilerParams(dimension_semantics=("arbitrary",),
                                             vmem_limit_bytes=VMEM_LIMIT),
        name="norm_proj",
    )(x.reshape(m, d), g, wq, wkv, wrkv, wl)


def _sb_kernel(q_ref, kd_ref, vd_ref, kp_ref, vp_ref, o_ref, *, tq, causal_prefix, own_transposed,
               n_prefix):
    pairs = SB_WIDTH // LANES
    by_pair = lambda x: jnp.stack([x[:, p * LANES:(p + 1) * LANES] for p in range(pairs)])
    q = by_pair(q_ref[0] * (1.0 / math.sqrt(HEAD_DIM)))
    first = lax.broadcasted_iota(jnp.int32, (pairs, tq, LANES), 2) < HEAD_DIM
    qe = jnp.concatenate([jnp.where(first, q, 0.0), jnp.where(first, 0.0, q)], axis=1).astype(BF16)

    row = lax.broadcasted_iota(jnp.int32, (pairs, 2 * tq, KEY_BLOCK), 1)
    col = lax.broadcasted_iota(jnp.int32, (pairs, 2 * tq, KEY_BLOCK), 2)
    causal = col < jnp.where(row >= tq, row - tq, row)

    jj = lax.broadcasted_iota(jnp.int32, (KEY_BLOCK, 2 * KEY_BLOCK), 0)
    ss = lax.broadcasted_iota(jnp.int32, (KEY_BLOCK, 2 * KEY_BLOCK), 1)
    suffix = jnp.where((jj > ss) | (ss >= KEY_BLOCK), 1.0, 0.0).astype(BF16)
    suffix2 = jnp.concatenate([suffix, suffix], axis=0)

    def prefix(ref, start, n_keys):
        return ref[0, :, pl.ds(start, n_keys)].reshape(pairs, LANES, n_keys)

    def suffix_sums(log1mb):
        hi_lo = jnp.concatenate(_split2(log1mb.reshape(pairs * 2 * tq, KEY_BLOCK)), axis=1)
        return _dot(hi_lo, suffix2).reshape(pairs, 2 * tq, 2 * KEY_BLOCK)

    def softplus(z):
        return jnp.maximum(z, 0.0) + jnp.log(1.0 + jnp.exp(-jnp.abs(z)))

    def block(kb, vb, c, o, mask, transposed):
        z = (_bdot if transposed else _bdot_nt)(qe, kb.astype(BF16))
        log1mb = -softplus(z)
        if mask is not None:
            log1mb = jnp.where(mask, log1mb, 0.0)
        sums = suffix_sums(log1mb)
        w = jnp.exp(z + log1mb + (sums[:, :, :KEY_BLOCK] + c))
        if mask is not None:
            w = jnp.where(mask, w, 0.0)
        o = o + (_bdot_nt if transposed else _bdot)(w.astype(BF16), vb.astype(BF16))
        return c + sums[:, :, KEY_BLOCK:], o

    def two_blocks(kb, vb, c, o):
        z = _bdot(qe, kb.astype(BF16))
        log1mb = -softplus(z)
        new = suffix_sums(log1mb[:, :, KEY_BLOCK:])
        old = suffix_sums(log1mb[:, :, :KEY_BLOCK])
        c_mid = c + new[:, :, KEY_BLOCK:]
        after = jnp.concatenate([old[:, :, :KEY_BLOCK] + c_mid, new[:, :, :KEY_BLOCK] + c], axis=2)
        w = jnp.exp(z + log1mb + after)
        return c_mid + old[:, :, KEY_BLOCK:], o + _bdot_nt(w.astype(BF16), vb.astype(BF16))

    zeros = jnp.zeros((pairs, 2 * tq, LANES), F32)
    own = (lambda ref: ref[0].reshape(pairs, LANES, KEY_BLOCK)) if own_transposed else (lambda ref: by_pair(ref[0]))
    c, o = block(own(kd_ref), own(vd_ref), zeros, zeros, causal, own_transposed)

    n = pl.program_id(1) if causal_prefix else n_prefix

    alive = lambda c: (jnp.max(c) >= SB_DEAD_LOG).astype(jnp.int32)

    def live(carry):
        return (carry[0] < (n >> 1)) & (carry[1] > 0)

    def body(carry):
        it, _, c, o = carry
        start = pl.multiple_of((n - 2 - 2 * it) * KEY_BLOCK, KEY_BLOCK)
        c, o = two_blocks(prefix(kp_ref, start, 2 * KEY_BLOCK), prefix(vp_ref, start, 2 * KEY_BLOCK), c, o)
        return it + 1, alive(c), c, o

    _, still, c, o = lax.while_loop(live, body, (jnp.int32(0), jnp.int32(1), c, o))
    c, o = lax.cond(((n & 1) == 1) & (still > 0),
                    lambda c, o: block(prefix(kp_ref, 0, KEY_BLOCK), prefix(vp_ref, 0, KEY_BLOCK), c, o, None, True),
                    lambda c, o: (c, o), c, o)
    o = jnp.where(first, o[:, :tq], o[:, tq:])
    o_ref[0] = jnp.concatenate([o[p] for p in range(pairs)], axis=1)


def _sb_attention(q, kd, vd, kp, vp, *, tq, causal_prefix, own_transposed):
    b, t, _ = q.shape
    nq = t // tq
    tp = kp.shape[2]
    blk = lambda rows: pl.BlockSpec((1, rows, SB_WIDTH), lambda bi, i: (bi, i, 0))
    own = pl.BlockSpec((1, SB_WIDTH, KEY_BLOCK), lambda bi, i: (bi, 0, i)) if own_transposed else blk(KEY_BLOCK)
    pre = pl.BlockSpec((1,) + kp.shape[1:], lambda bi, i: (bi, 0, 0))
    return pl.pallas_call(
        functools.partial(_sb_kernel, tq=tq, causal_prefix=causal_prefix, own_transposed=own_transposed,
                          n_prefix=tp // KEY_BLOCK),
        grid=(b, nq),
        in_specs=[blk(tq), own, own, pre, pre],
        out_specs=blk(tq),
        out_shape=jax.ShapeDtypeStruct(q.shape, F32),
        compiler_params=pltpu.CompilerParams(dimension_semantics=("arbitrary",) * 2,
                                             vmem_limit_bytes=VMEM_LIMIT),
        name="stickbreak",
    )(q, kd, vd, kp, vp)


def _rwkv_kernel(x_ref, l_ref, sx_ref, sl_ref, s0_ref, mux_ref, mul_ref, w0_ref, wd_ref, a0_ref, wa_ref,
                 wg_ref, kkw_ref, kaw_ref, rkw_ref, lnw_ref, lnb_ref,
                 o_ref, sout_ref,
                 px_s, pl_s, y_s, z_s, qe_s, yl_s, phi_s, psi_s,
                 *, tb, t_valid):
    t = pl.program_id(1)
    pairs = RW_WIDTH // LANES
    pr = lax.broadcasted_iota(jnp.int32, (HEAD_DIM, LANES), 0)
    pc = lax.broadcasted_iota(jnp.int32, (HEAD_DIM, LANES), 1)
    to_low = jnp.where(pc == pr, 1.0, 0.0).astype(BF16)
    to_high = jnp.where(pc == pr + HEAD_DIM, 1.0, 0.0).astype(BF16)

    @pl.when(t == 0)
    def _():
        px_s[...] = sx_ref[0]
        pl_s[...] = sl_ref[0]
        for p in range(pairs):
            z_s[p] = jnp.concatenate([_mm_exact_rhs(s0_ref[0, 2 * p], to_low, 3),
                                      _mm_exact_rhs(s0_ref[0, 2 * p + 1], to_high, 3)], axis=0).T

    def shift(x_ref, carry_s, mu_ref):
        x = x_ref[0]
        prev = pltpu.roll(x, 1, 0)
        prev = jnp.where(lax.broadcasted_iota(jnp.int32, x.shape, 0) == 0, carry_s[...], prev)
        carry_s[...] = x[tb - 1:tb, :]
        return x + (prev - x) * mu_ref[...]

    xs = shift(x_ref, px_s, mux_ref)
    xr, xk, xv = xs[:, :RW_WIDTH], xs[:, RW_WIDTH:2 * RW_WIDTH], xs[:, 2 * RW_WIDTH:]
    xl = shift(l_ref, pl_s, mul_ref)

    hr = lax.broadcasted_iota(jnp.int32, (LANES, LANES), 0) // HEAD_DIM
    hc = lax.broadcasted_iota(jnp.int32, (LANES, LANES), 1) // HEAD_DIM
    head_ones = jnp.where(hr == hc, 1.0, 0.0).astype(BF16)
    head_sum = lambda x, parts=2: jnp.concatenate(
        [_mm_exact_rhs(x[:, p * LANES:(p + 1) * LANES], head_ones, parts) for p in range(pairs)], axis=1)

    lora = lambda x, w_ref: _dot(x.astype(BF16), w_ref[...].astype(BF16))
    log_w = -DECAY_SCALE * _sigmoid(w0_ref[...] + lora(jnp.tanh(xl), wd_ref))
    a = _sigmoid(a0_ref[...] + lora(xl, wa_ref))
    kk = xk * kkw_ref[...]
    kk = kk * lax.rsqrt(jnp.maximum(head_sum(kk * kk, 1), 1e-24))
    k2 = xk * (1.0 + (a - 1.0) * kaw_ref[...])
    b = kk * a
    if t_valid < tb:
        valid = lax.broadcasted_iota(jnp.int32, (tb, RW_WIDTH), 0) < t_valid
        log_w = jnp.where(valid, log_w, 0.0)
        kk = jnp.where(valid, kk, 0.0)
        b = jnp.where(valid, b, 0.0)
        k2 = jnp.where(valid, k2, 0.0)
        xv = jnp.where(valid, xv, 0.0)
    nc = tb // CHUNK
    nb = pairs * nc
    e2 = 2 * CHUNK
    by_chunk = lambda x: jnp.stack(
        [x[:, p * LANES:(p + 1) * LANES] for p in range(pairs)]).reshape(nb, CHUNK, LANES)
    ti = lax.broadcasted_iota(jnp.int32, (nb, CHUNK, CHUNK), 1)
    tj = lax.broadcasted_iota(jnp.int32, (nb, CHUNK, CHUNK), 2)
    tri = jnp.where(tj <= ti, 1.0, 0.0).astype(BF16)
    lw3 = by_chunk(log_w)
    lw_hi, lw_lo = _split2(lw3)
    cum3 = _bdot(tri, lw_hi) + _bdot(tri, lw_lo)
    tot = cum3[:, CHUNK - 1:CHUNK, :]
    first = lax.broadcasted_iota(jnp.int32, (nb, CHUNK, LANES), 2) < HEAD_DIM
    expand = lambda x: jnp.concatenate([jnp.where(first, x, 0.0), jnp.where(first, 0.0, x)], axis=1)
    e_neg = jnp.exp(-cum3)
    e_tail = jnp.exp(tot - cum3)
    kk3, b3, k3, v3 = by_chunk(kk), by_chunk(b), by_chunk(k2), by_chunk(xv)
    kkt = kk3 * jnp.exp(cum3 - lw3)
    rt = by_chunk(xr) * jnp.exp(cum3)

    wr = lax.broadcasted_iota(jnp.int32, (nb, CHUNK, e2), 1)
    wc = lax.broadcasted_iota(jnp.int32, (nb, CHUNK, e2), 2)
    head0 = wc < CHUNK
    ws = jnp.where(head0, wc, wc - CHUNK)
    strict, incl, eye_w = ws < wr, ws <= wr, ws == wr
    diag2 = lambda x: jnp.concatenate([jnp.where(head0, x, 0.0), jnp.where(head0, 0.0, x)], axis=1)

    gram = _bdot_nt(jnp.concatenate([kkt, rt], axis=1).astype(BF16),
                    jnp.concatenate([expand(k3 * e_neg), expand(b3 * e_neg)], axis=1).astype(BF16))
    ak = jnp.where(strict, gram[:, :CHUNK, :e2], 0.0)
    ab = jnp.where(strict, gram[:, :CHUNK, e2:], 0.0)
    rk = jnp.where(incl, gram[:, CHUNK:, :e2], 0.0)
    rb = jnp.where(incl, gram[:, CHUNK:, e2:], 0.0)

    eye2 = jnp.where(eye_w, 1.0, 0.0)
    tinv = eye2 - ab
    ab_d = diag2(ab)
    pw = _bmm1(ab, ab_d)
    n_sq = CHUNK.bit_length() - 2
    for s in range(n_sq):
        if s + 1 < n_sq:
            both = _bmm1(pw, jnp.concatenate([diag2(tinv), diag2(pw)], axis=2))
            tinv = tinv + both[:, :, :e2]
            pw = both[:, :, e2:]
        else:
            tinv = tinv + _bmm1(pw, diag2(tinv))
    resid = eye2 - tinv - _bmm3(ab, diag2(tinv))
    tinv = tinv + _bmm1(tinv, diag2(resid))

    ve = expand(v3)
    akve = _bmm1(ak, ve)
    wu = _bmm1(tinv, jnp.concatenate([expand(kkt), expand(akve)], axis=2))
    w_, u_loc = wu[:, :, :LANES], wu[:, :, LANES:]
    yq = _bmm1(jnp.concatenate([rk, -rb], axis=2),
               jnp.concatenate([jnp.concatenate([ve, jnp.zeros_like(ve)], axis=2),
                                jnp.concatenate([expand(u_loc), expand(w_)], axis=2)], axis=1))
    per_pair = lambda x: x.reshape((pairs, nc) + x.shape[1:])
    qe_s[...] = per_pair(rt + yq[:, :, LANES:])
    yl_s[...] = per_pair(yq[:, :, :LANES])
    zp = _bmm1(jnp.swapaxes(jnp.concatenate([k3 * e_tail, -(b3 * e_tail)], axis=1), 1, 2),
               jnp.concatenate([jnp.concatenate([v3, jnp.zeros_like(v3)], axis=2),
                                jnp.concatenate([u_loc, w_], axis=2)], axis=1))
    zr = lax.broadcasted_iota(jnp.int32, (nb, HEAD_DIM, LANES), 1)
    zc = lax.broadcasted_iota(jnp.int32, (nb, HEAD_DIM, LANES), 2)
    zhead0 = zc < HEAD_DIM
    own_block = lambda x: jnp.where(zhead0, x[:, :HEAD_DIM], x[:, HEAD_DIM:])
    decay = jnp.where(jnp.where(zhead0, zc, zc - HEAD_DIM) == zr, jnp.exp(tot), 0.0)
    phi_s[...] = per_pair(decay + own_block(zp[:, :, LANES:]))
    psi_s[...] = per_pair(own_block(zp[:, :, :LANES]))
    shead0 = lax.broadcasted_iota(jnp.int32, (pairs, HEAD_DIM, LANES), 2) < HEAD_DIM

    def advance(c, _):
        rows = pl.ds(pl.multiple_of(c * CHUNK, CHUNK), CHUNK)
        z = z_s[...]
        y_s[:, rows, :] = _bmm1(qe_s[:, c], z) + yl_s[:, c]
        z_new = _bmm3(phi_s[:, c], z) + psi_s[:, c]
        z_s[...] = jnp.concatenate([jnp.where(shead0, z_new, 0.0), jnp.where(shead0, 0.0, z_new)], axis=1)
        return 0

    lax.fori_loop(0, nc, advance, 0, unroll=min(nc, 2))

    y = jnp.concatenate([y_s[p] for p in range(pairs)], axis=1)
    d = y - head_sum(y) * (1.0 / HEAD_DIM)
    var = head_sum(d * d) * (1.0 / HEAD_DIM)
    o = d * lax.rsqrt(var + GN_EPS) * lnw_ref[...] + lnb_ref[...]
    bonus = head_sum(xr * k2 * rkw_ref[...]) * xv
    o_ref[0] = (o + bonus) * lora(_sigmoid(xl), wg_ref)

    @pl.when(t == pl.num_programs(1) - 1)
    def _():
        for p in range(pairs):
            s_pair = z_s[p].T
            sout_ref[0, 2 * p] = s_pair[:HEAD_DIM, :HEAD_DIM]
            sout_ref[0, 2 * p + 1] = _mm_exact_rhs(s_pair[HEAD_DIM:], to_high, 3, transposed=True)


def _rwkv(rkv, lora, shift_rkv, shift_lora, z0, prm, *, tb, t_valid):
    b, t, _ = rkv.shape
    nt = t // tb
    nc = tb // CHUNK
    pairs = RW_WIDTH // LANES
    rows = lambda w: pl.BlockSpec((1, tb, w), lambda bi, ti: (bi, ti, 0))
    first = lambda w: pl.BlockSpec((1, 1, w), lambda bi, ti: (bi, 0, 0))
    state = pl.BlockSpec((1, 2 * pairs, HEAD_DIM, HEAD_DIM), lambda bi, ti: (bi, 0, 0, 0))
    vec = _const_spec((1, RW_WIDTH))
    mat = _const_spec((LORA_PAD, RW_WIDTH))
    return pl.pallas_call(
        functools.partial(_rwkv_kernel, tb=tb, t_valid=t_valid),
        grid=(b, nt),
        in_specs=[rows(3 * RW_WIDTH), rows(LORA_PAD), first(3 * RW_WIDTH), first(LORA_PAD), state,
                  _const_spec((1, 3 * RW_WIDTH)), _const_spec((1, LORA_PAD)),
                  vec, mat, vec, mat, mat, vec, vec, vec, vec, vec],
        out_specs=[rows(RW_WIDTH), state],
        out_shape=[jax.ShapeDtypeStruct((b, t, RW_WIDTH), F32),
                   jax.ShapeDtypeStruct((b, 2 * pairs, HEAD_DIM, HEAD_DIM), F32)],
        scratch_shapes=[pltpu.VMEM((1, 3 * RW_WIDTH), F32), pltpu.VMEM((1, LORA_PAD), F32),
                        pltpu.VMEM((pairs, tb, LANES), F32), pltpu.VMEM((pairs, LANES, LANES), F32)]
        + [pltpu.VMEM((pairs, nc, CHUNK, LANES), F32)] * 2
        + [pltpu.VMEM((pairs, nc, HEAD_DIM, LANES), F32)] * 2,
        compiler_params=pltpu.CompilerParams(dimension_semantics=("arbitrary",) * 2,
                                             vmem_limit_bytes=VMEM_LIMIT),
        name="rwkv7",
    )(rkv, lora, shift_rkv, shift_lora, z0, prm["mu_rkv"], prm["mu_lora"],
      prm["w0"], prm["wd"], prm["a0"], prm["wa"], prm["wg"],
      prm["k_k"], prm["k_a"], prm["r_k"], prm["ln_w"], prm["ln_b"])


def _out_ffn_kernel(x_ref, sb_ref, rw_ref, woa_ref, wob_ref, gf_ref, wg_ref, wu_ref, wd_ref, gl_ref,
                    y_ref):
    x1 = x_ref[...] + _dot(sb_ref[...].astype(BF16), woa_ref[...]) \
        + _dot(rw_ref[...].astype(BF16), wob_ref[...])
    h2 = _rmsnorm(x1, gf_ref[...]).astype(BF16)
    gate = _dot(h2, wg_ref[...])
    act = gate * _sigmoid(gate) * _dot(h2, wu_ref[...])
    x2 = x1 + _dot(act.astype(BF16), wd_ref[...])
    y_ref[...] = _rmsnorm(x2, gl_ref[...])


def _out_ffn(x2d, o_sb, o_rw, woa, wob, gf, wg, wu, wd, gl, tm):
    m, d = x2d.shape
    row = lambda w: pl.BlockSpec((tm, w), lambda i: (i, 0))
    return pl.pallas_call(
        _out_ffn_kernel,
        grid=(m // tm,),
        in_specs=[row(d), row(SB_WIDTH), row(RW_WIDTH), _const_spec(woa.shape), _const_spec(wob.shape),
                  _const_spec((1, d)), _const_spec(wg.shape), _const_spec(wu.shape),
                  _const_spec(wd.shape), _const_spec((1, d))],
        out_specs=row(d),
        out_shape=jax.ShapeDtypeStruct((m, d), F32),
        compiler_params=pltpu.CompilerParams(dimension_semantics=("arbitrary",),
                                             vmem_limit_bytes=VMEM_LIMIT),
        name="out_ffn",
    )(x2d, o_sb, o_rw, woa, wob, gf, wg, wu, wd, gl)


def _pad_rows(x, rows):
    return jnp.pad(x, ((0, 0), (0, rows - x.shape[1]), (0, 0)))


def _pad_lanes(x, lanes):
    return jnp.pad(x, [(0, 0)] * (x.ndim - 1) + [(0, lanes - x.shape[-1])])


def _layer(x, k_past, v_past, wkv0, shift0, w, *, tm, tq, tb):
    b, t, d = x.shape
    x2d = x.reshape(b * t, d)
    to_feature_major = lambda a: jnp.transpose(a, (0, 2, 3, 1)).reshape(b, SB_WIDTH, a.shape[1])
    from_feature_major = lambda a: jnp.transpose(a.reshape(b, SB_HEADS, HEAD_DIM, a.shape[2]), (0, 3, 1, 2))
    if k_past is None:
        q, k, v, rkv, lora = _norm_proj(x, w["norm_mix_g"], w["w_q"], w["w_kv_t"], w["w_rkv"], w["w_lora"],
                                        tm, feature_major_kv=True)
        o_sb = _sb_attention(q.reshape(b, t, SB_WIDTH), k, v, k, v, tq=tq, causal_prefix=True,
                             own_transposed=True)
        k_heads, v_heads = from_feature_major(k), from_feature_major(v)
    else:
        q, k, v, rkv, lora = _norm_proj(x, w["norm_mix_g"], w["w_q"], w["w_kv"], w["w_rkv"], w["w_lora"],
                                        tm, feature_major_kv=False)
        q3, k3, v3 = (a.reshape(b, t, SB_WIDTH) for a in (q, k, v))
        o_sb = _sb_attention(q3, _pad_rows(k3, KEY_BLOCK), _pad_rows(v3, KEY_BLOCK),
                             to_feature_major(k_past), to_feature_major(v_past), tq=tq,
                             causal_prefix=False, own_transposed=False)
        k_heads, v_heads = (a.reshape(b, t, SB_HEADS, HEAD_DIM) for a in (k, v))

    rkv3 = rkv.reshape(b, t, 3 * RW_WIDTH)
    lora3 = lora.reshape(b, t, LORA_PAD)
    t_pad = -(-t // tb) * tb
    o_rw, wkv_out = _rwkv(_pad_rows(rkv3, t_pad), _pad_rows(lora3, t_pad),
                          shift0[..., :3 * RW_WIDTH], _pad_lanes(shift0[..., 3 * RW_WIDTH:], LORA_PAD),
                          wkv0, w, tb=tb, t_valid=min(t, tb))
    o_rw = o_rw[:, :t]
    shift_last = jnp.concatenate([rkv3[:, t - 1:, :], lora3[:, t - 1:, :LORA_WIDTH]], axis=-1)

    y = _out_ffn(x2d, o_sb.reshape(b * t, SB_WIDTH), o_rw.reshape(b * t, RW_WIDTH),
                 w["w_out_sb"], w["w_out_rw"], w["norm_ffn_g"], w["w_gate"], w["w_up"], w["w_down"],
                 w["norm_final_g"], tm)
    return y.reshape(b, t, d), k_heads, v_heads, wkv_out, shift_last


def kernel(x_prompt, x_sample, cache_k, cache_v, state_wkv, state_shift, norm_mix_g, w_in, mu_shift, w0,
           w_decay_up, a0, w_aaa_up, w_gate_up, k_k, k_a, r_k, ln_x_w, ln_x_b, w_out, norm_ffn_g, w_gate,
           w_up, w_down, norm_final_g):
    assert w_in.shape[0] == 1, "single-layer trunk"
    l = 0
    rw3 = 3 * RW_WIDTH
    lora_rows = lambda m, lo, hi: jnp.pad(m, ((lo, LORA_PAD - hi), (0, 0)))
    w = {
        "norm_mix_g": norm_mix_g[l][None, :],
        "w_q": w_in[l][:, :SB_WIDTH].astype(BF16),
        "w_kv": w_in[l][:, SB_WIDTH:3 * SB_WIDTH].astype(BF16),
        "w_kv_t": w_in[l][:, SB_WIDTH:3 * SB_WIDTH].T.astype(BF16),
        "w_rkv": w_in[l][:, 3 * SB_WIDTH:3 * SB_WIDTH + rw3].astype(BF16),
        "w_lora": _pad_lanes(w_in[l][:, 3 * SB_WIDTH + rw3:], LORA_PAD).astype(BF16),
        "mu_rkv": mu_shift[l][None, :rw3],
        "mu_lora": _pad_lanes(mu_shift[l][None, rw3:], LORA_PAD),
        "w0": w0[l][None, :],
        "wd": lora_rows(w_decay_up[l], 0, LORA_DECAY),
        "a0": a0[l][None, :],
        "wa": lora_rows(w_aaa_up[l], LORA_DECAY, LORA_DECAY + LORA_AAA),
        "wg": lora_rows(w_gate_up[l], LORA_DECAY + LORA_AAA, LORA_WIDTH),
        "k_k": k_k[l][None, :],
        "k_a": k_a[l][None, :],
        "r_k": r_k[l].reshape(1, RW_WIDTH),
        "ln_w": ln_x_w[l][None, :],
        "ln_b": ln_x_b[l][None, :],
        "w_out_sb": w_out[l][:SB_WIDTH].astype(BF16),
        "w_out_rw": w_out[l][SB_WIDTH:].astype(BF16),
        "norm_ffn_g": norm_ffn_g[l][None, :],
        "w_gate": w_gate[l].astype(BF16),
        "w_up": w_up[l].astype(BF16),
        "w_down": w_down[l].astype(BF16),
        "norm_final_g": norm_final_g[None, :],
    }
    bp = x_prompt.shape[0]
    heads = RW_WIDTH // HEAD_DIM
    wkv_zero = jnp.zeros((bp, heads, HEAD_DIM, HEAD_DIM), x_prompt.dtype)
    shift_zero = jnp.zeros((bp, 1, rw3 + LORA_WIDTH), x_prompt.dtype)
    yp, k1, v1, s1, sh1 = _layer(x_prompt, None, None, wkv_zero, shift_zero, w, tm=512, tq=128, tb=512)
    ys, k2, v2, s2, sh2 = _layer(x_sample, cache_k[l], cache_v[l], state_wkv[l], state_shift[l], w,
                                 tm=256, tq=x_sample.shape[1], tb=CHUNK)
    return (yp, ys, k1[None], v1[None], s1[None], sh1[None], k2[None], v2[None], s2[None], sh2[None])
```

```python
import functools
import math

import jax
import jax.numpy as jnp
from jax import lax
from jax.experimental import pallas as pl
from jax.experimental.pallas import tpu as pltpu

F32 = jnp.float32
BF16 = jnp.bfloat16

LANES = 128
HEAD_DIM = 64
SB_WIDTH = 512
SB_HEADS = SB_WIDTH // HEAD_DIM
RW_WIDTH = 512
LORA_DECAY = 64
LORA_AAA = 64
LORA_GATE = 160
LORA_WIDTH = LORA_DECAY + LORA_AAA + LORA_GATE
LORA_PAD = 384
RMS_EPS = 1e-6
GN_EPS = 64e-5
DECAY_SCALE = math.exp(-0.5)
KEY_BLOCK = 128
SB_DEAD_LOG = -104.0
CHUNK = 64
VMEM_LIMIT = 56 * 1024 * 1024


def _dot(a, b):
    return jnp.dot(a, b, preferred_element_type=F32)


def _split2(x):
    hi = x.astype(BF16)
    lo = (x - hi.astype(F32)).astype(BF16)
    return hi, lo


def _bdot(a, b):
    return jnp.einsum("bij,bjk->bik", a, b, preferred_element_type=F32)


def _bdot_nt(a, b):
    return jnp.einsum("bik,bjk->bij", a, b, preferred_element_type=F32)


def _bmm1(a, b):
    return _bdot(a.astype(BF16), b.astype(BF16))


def _bmm3(a, b):
    ah, al = _split2(a)
    bh, bl = _split2(b)
    return _bdot(ah, bh) + (_bdot(ah, bl) + _bdot(al, bh))


def _mm_exact_rhs(x, m, parts=2, transposed=False):
    contract = (((1,), (1 if transposed else 0,)), ((), ()))
    acc = None
    for _ in range(parts):
        part = x.astype(BF16)
        prod = lax.dot_general(part, m, contract, preferred_element_type=F32)
        acc = prod if acc is None else acc + prod
        x = x - part.astype(F32)
    return acc


def _sigmoid(x):
    return 0.5 * jnp.tanh(0.5 * x) + 0.5


def _rmsnorm(x, g):
    return x * lax.rsqrt(jnp.mean(x * x, axis=-1, keepdims=True) + RMS_EPS) * g


def _const_spec(shape):
    return pl.BlockSpec(shape, lambda *_: (0,) * len(shape), pipeline_mode=pl.Buffered(1))


def _norm_proj_kernel(x_ref, g_ref, wq_ref, wkv_ref, wrkv_ref, wl_ref,
                      q_ref, k_ref, v_ref, rkv_ref, l_ref, *, feature_major_kv):
    hb = _rmsnorm(x_ref[...], g_ref[...]).astype(BF16)
    q_ref[...] = _dot(hb, wq_ref[...])
    if feature_major_kv:
        kv = lax.dot_general(wkv_ref[...], hb, (((1,), (1,)), ((), ())), preferred_element_type=F32)
        k_ref[0] = kv[:SB_WIDTH]
        v_ref[0] = kv[SB_WIDTH:]
    else:
        kv = _dot(hb, wkv_ref[...])
        k_ref[...] = kv[:, :SB_WIDTH]
        v_ref[...] = kv[:, SB_WIDTH:]
    rkv_ref[...] = _dot(hb, wrkv_ref[...])
    l_ref[...] = _dot(hb, wl_ref[...])


def _norm_proj(x, g, wq, wkv, wrkv, wl, tm, feature_major_kv):
    b, t, d = x.shape
    m = b * t
    row = lambda w: pl.BlockSpec((tm, w), lambda i: (i, 0))
    if feature_major_kv:
        per_batch = t // tm
        kv_spec = pl.BlockSpec((1, SB_WIDTH, tm), lambda i: (i // per_batch, 0, i % per_batch))
        kv_shape = jax.ShapeDtypeStruct((b, SB_WIDTH, t), F32)
    else:
        kv_spec, kv_shape = row(SB_WIDTH), jax.ShapeDtypeStruct((m, SB_WIDTH), F32)
    return pl.pallas_call(
        functools.partial(_norm_proj_kernel, feature_major_kv=feature_major_kv),
        grid=(m // tm,),
        in_specs=[row(d), _const_spec((1, d)), _const_spec(wq.shape), _const_spec(wkv.shape),
                  _const_spec(wrkv.shape), _const_spec(wl.shape)],
        out_specs=[row(SB_WIDTH), kv_spec, kv_spec, row(3 * RW_WIDTH), row(LORA_PAD)],
        out_shape=[jax.ShapeDtypeStruct((m, SB_WIDTH), F32), kv_shape, kv_shape,
                   jax.ShapeDtypeStruct((m, 3 * RW_WIDTH), F32), jax.ShapeDtypeStruct((m, LORA_PAD), F32)],
        compiler_params=pltpu.CompilerParams(dimension_semantics=("arbitrary",),
                                             vmem_limit_bytes=VMEM_LIMIT),
        name="norm_proj",
    )(x.reshape(m, d), g, wq, wkv, wrkv, wl)


def _sb_kernel(q_ref, kd_ref, vd_ref, kp_ref, vp_ref, o_ref, *, tq, causal_prefix, own_transposed,
               n_prefix):
    pairs = SB_WIDTH // LANES
    by_pair = lambda x: jnp.stack([x[:, p * LANES:(p + 1) * LANES] for p in range(pairs)])
    q = by_pair(q_ref[0] * (1.0 / math.sqrt(HEAD_DIM)))
    first = lax.broadcasted_iota(jnp.int32, (pairs, tq, LANES), 2) < HEAD_DIM
    qe = jnp.concatenate([jnp.where(first, q, 0.0), jnp.where(first, 0.0, q)], axis=1).astype(BF16)

    row = lax.broadcasted_iota(jnp.int32, (pairs, 2 * tq, KEY_BLOCK), 1)
    col = lax.broadcasted_iota(jnp.int32, (pairs, 2 * tq, KEY_BLOCK), 2)
    causal = col < jnp.where(row >= tq, row - tq, row)

    jj = lax.broadcasted_iota(jnp.int32, (KEY_BLOCK, 2 * KEY_BLOCK), 0)
    ss = lax.broadcasted_iota(jnp.int32, (KEY_BLOCK, 2 * KEY_BLOCK), 1)
    suffix = jnp.where((jj > ss) | (ss >= KEY_BLOCK), 1.0, 0.0).astype(BF16)
    suffix2 = jnp.concatenate([suffix, suffix], axis=0)

    def prefix(ref, start, n_keys):
        return ref[0, :, pl.ds(start, n_keys)].reshape(pairs, LANES, n_keys)

    def suffix_sums(log1mb):
        hi_lo = jnp.concatenate(_split2(log1mb.reshape(pairs * 2 * tq, KEY_BLOCK)), axis=1)
        return _dot(hi_lo, suffix2).reshape(pairs, 2 * tq, 2 * KEY_BLOCK)

    def softplus(z):
        return jnp.maximum(z, 0.0) + jnp.log(1.0 + jnp.exp(-jnp.abs(z)))

    def block(kb, vb, c, o, mask, transposed):
        z = (_bdot if transposed else _bdot_nt)(qe, kb.astype(BF16))
        log1mb = -softplus(z)
        if mask is not None:
            log1mb = jnp.where(mask, log1mb, 0.0)
        sums = suffix_sums(log1mb)
        w = jnp.exp(z + log1mb + (sums[:, :, :KEY_BLOCK] + c))
        if mask is not None:
            w = jnp.where(mask, w, 0.0)
        o = o + (_bdot_nt if transposed else _bdot)(w.astype(BF16), vb.astype(BF16))
        return c + sums[:, :, KEY_BLOCK:], o

    def two_blocks(kb, vb, c, o):
        z = _bdot(qe, kb.astype(BF16))
        log1mb = -softplus(z)
        new = suffix_sums(log1mb[:, :, KEY_BLOCK:])
        old = suffix_sums(log1mb[:, :, :KEY_BLOCK])
        c_mid = c + new[:, :, KEY_BLOCK:]
        after = jnp.concatenate([old[:, :, :KEY_BLOCK] + c_mid, new[:, :, :KEY_BLOCK] + c], axis=2)
        w = jnp.exp(z + log1mb + after)
        return c_mid + old[:, :, KEY_BLOCK:], o + _bdot_nt(w.astype(BF16), vb.astype(BF16))

    zeros = jnp.zeros((pairs, 2 * tq, LANES), F32)
    own = (lambda ref: ref[0].reshape(pairs, LANES, KEY_BLOCK)) if own_transposed else (lambda ref: by_pair(ref[0]))
    c, o = block(own(kd_ref), own(vd_ref), zeros, zeros, causal, own_transposed)

    n = pl.program_id(1) if causal_prefix else n_prefix

    alive = lambda c: (jnp.max(c) >= SB_DEAD_LOG).astype(jnp.int32)

    def live(carry):
        return (carry[0] < (n >> 1)) & (carry[1] > 0)

    def body(carry):
        it, _, c, o = carry
        start = pl.multiple_of((n - 2 - 2 * it) * KEY_BLOCK, KEY_BLOCK)
        c, o = two_blocks(prefix(kp_ref, start, 2 * KEY_BLOCK), prefix(vp_ref, start, 2 * KEY_BLOCK), c, o)
        return it + 1, alive(c), c, o

    _, still, c, o = lax.while_loop(live, body, (jnp.int32(0), jnp.int32(1), c, o))
    c, o = lax.cond(((n & 1) == 1) & (still > 0),
                    lambda c, o: block(prefix(kp_ref, 0, KEY_BLOCK), prefix(vp_ref, 0, KEY_BLOCK), c, o, None, True),
                    lambda c, o: (c, o), c, o)
    o = jnp.where(first, o[:, :tq], o[:, tq:])
    o_ref[0] = jnp.concatenate([o[p] for p in range(pairs)], axis=1)


def _sb_attention(q, kd, vd, kp, vp, *, tq, causal_prefix, own_transposed):
    b, t, _ = q.shape
    nq = t // tq
    tp = kp.shape[2]
    blk = lambda rows: pl.BlockSpec((1, rows, SB_WIDTH), lambda bi, i: (bi, i, 0))
    own = pl.BlockSpec((1, SB_WIDTH, KEY_BLOCK), lambda bi, i: (bi, 0, i)) if own_transposed else blk(KEY_BLOCK)
    pre = pl.BlockSpec((1,) + kp.shape[1:], lambda bi, i: (bi, 0, 0))
    return pl.pallas_call(
        functools.partial(_sb_kernel, tq=tq, causal_prefix=causal_prefix, own_transposed=own_transposed,
                          n_prefix=tp // KEY_BLOCK),
        grid=(b, nq),
        in_specs=[blk(tq), own, own, pre, pre],
        out_specs=blk(tq),
        out_shape=jax.ShapeDtypeStruct(q.shape, F32),
        compiler_params=pltpu.CompilerParams(dimension_semantics=("arbitrary",) * 2,
                                             vmem_limit_bytes=VMEM_LIMIT),
        name="stickbreak",
    )(q, kd, vd, kp, vp)


def _rwkv_kernel(x_ref, l_ref, sx_ref, sl_ref, s0_ref, mux_ref, mul_ref, w0_ref, wd_ref, a0_ref, wa_ref,
                 wg_ref, kkw_ref, kaw_ref, rkw_ref, lnw_ref, lnb_ref,
                 o_ref, sout_ref,
                 px_s, pl_s, y_s, z_s, qe_s, yl_s, phi_s, psi_s,
                 *, nbat, tb, t_valid):
    t = pl.program_id(0)
    pairs = RW_WIDTH // LANES
    rows_all = nbat * tb
    pr = lax.broadcasted_iota(jnp.int32, (HEAD_DIM, LANES), 0)
    pc = lax.broadcasted_iota(jnp.int32, (HEAD_DIM, LANES), 1)
    to_low = jnp.where(pc == pr, 1.0, 0.0).astype(BF16)
    to_high = jnp.where(pc == pr + HEAD_DIM, 1.0, 0.0).astype(BF16)

    @pl.when(t == 0)
    def _():
        px_s[...] = sx_ref[...]
        pl_s[...] = sl_ref[...]
        for p in range(pairs):
            for bi in range(nbat):
                z_s[p * nbat + bi] = jnp.concatenate(
                    [_mm_exact_rhs(s0_ref[bi, 2 * p], to_low, 3),
                     _mm_exact_rhs(s0_ref[bi, 2 * p + 1], to_high, 3)], axis=0).T

    def shift(x_ref, carry_s, mu_ref):
        x = x_ref[...]
        prev = pltpu.roll(x.reshape(rows_all, x.shape[2]), 1, 0).reshape(x.shape)
        prev = jnp.where(lax.broadcasted_iota(jnp.int32, x.shape, 1) == 0, carry_s[...], prev)
        carry_s[...] = x[:, tb - 1:tb, :]
        return (x + (prev - x) * mu_ref[...]).reshape(rows_all, x.shape[2])

    xs = shift(x_ref, px_s, mux_ref)
    xr, xk, xv = xs[:, :RW_WIDTH], xs[:, RW_WIDTH:2 * RW_WIDTH], xs[:, 2 * RW_WIDTH:]
    xl = shift(l_ref, pl_s, mul_ref)

    hr = lax.broadcasted_iota(jnp.int32, (LANES, LANES), 0) // HEAD_DIM
    hc = lax.broadcasted_iota(jnp.int32, (LANES, LANES), 1) // HEAD_DIM
    head_ones = jnp.where(hr == hc, 1.0, 0.0).astype(BF16)
    head_sum = lambda x, parts=2: jnp.concatenate(
        [_mm_exact_rhs(x[:, p * LANES:(p + 1) * LANES], head_ones, parts) for p in range(pairs)], axis=1)

    lora = lambda x, w_ref, lo, hi: _dot(x.astype(BF16), w_ref[lo:hi, :].astype(BF16))
    log_w = -DECAY_SCALE * _sigmoid(w0_ref[...] + lora(jnp.tanh(xl[:, :LANES]), wd_ref, 0, LANES))
    a = _sigmoid(a0_ref[...] + lora(xl[:, :LANES], wa_ref, 0, LANES))
    kk = xk * kkw_ref[...]
    kk = kk * lax.rsqrt(jnp.maximum(head_sum(kk * kk, 1), 1e-24))
    k2 = xk * (1.0 + (a - 1.0) * kaw_ref[...])
    b = kk * a
    if t_valid < tb:
        valid = lax.broadcasted_iota(jnp.int32, (nbat, tb, RW_WIDTH), 1).reshape(rows_all, RW_WIDTH) < t_valid
        log_w = jnp.where(valid, log_w, 0.0)
        kk = jnp.where(valid, kk, 0.0)
        b = jnp.where(valid, b, 0.0)
        k2 = jnp.where(valid, k2, 0.0)
        xv = jnp.where(valid, xv, 0.0)
    nc = tb // CHUNK
    nb = pairs * nbat * nc
    e2 = 2 * CHUNK
    by_chunk = lambda x: jnp.stack(
        [x[:, p * LANES:(p + 1) * LANES] for p in range(pairs)]).reshape(nb, CHUNK, LANES)
    ti = lax.broadcasted_iota(jnp.int32, (nb, CHUNK, CHUNK), 1)
    tj = lax.broadcasted_iota(jnp.int32, (nb, CHUNK, CHUNK), 2)
    tri = jnp.where(tj <= ti, 1.0, 0.0).astype(BF16)
    lw3 = by_chunk(log_w)
    lw_hi, lw_lo = _split2(lw3)
    cum3 = _bdot(tri, lw_hi) + _bdot(tri, lw_lo)
    tot = cum3[:, CHUNK - 1:CHUNK, :]
    first = lax.broadcasted_iota(jnp.int32, (nb, CHUNK, LANES), 2) < HEAD_DIM
    expand = lambda x: jnp.concatenate([jnp.where(first, x, 0.0), jnp.where(first, 0.0, x)], axis=1)
    e_neg = jnp.exp(-cum3)
    e_tail = jnp.exp(tot - cum3)
    kk3, b3, k3, v3 = by_chunk(kk), by_chunk(b), by_chunk(k2), by_chunk(xv)
    kkt = kk3 * jnp.exp(cum3 - lw3)
    rt = by_chunk(xr) * jnp.exp(cum3)

    wr = lax.broadcasted_iota(jnp.int32, (nb, CHUNK, e2), 1)
    wc = lax.broadcasted_iota(jnp.int32, (nb, CHUNK, e2), 2)
    head0 = wc < CHUNK
    ws = jnp.where(head0, wc, wc - CHUNK)
    strict, incl, eye_w = ws < wr, ws <= wr, ws == wr
    diag2 = lambda x: jnp.concatenate([jnp.where(head0, x, 0.0), jnp.where(head0, 0.0, x)], axis=1)

    gram = _bdot_nt(jnp.concatenate([kkt, rt], axis=1).astype(BF16),
                    jnp.concatenate([expand(k3 * e_neg), expand(b3 * e_neg)], axis=1).astype(BF16))
    ak = jnp.where(strict, gram[:, :CHUNK, :e2], 0.0)
    ab = jnp.where(strict, gram[:, :CHUNK, e2:], 0.0)
    rk = jnp.where(incl, gram[:, CHUNK:, :e2], 0.0)
    rb = jnp.where(incl, gram[:, CHUNK:, e2:], 0.0)

    eye2 = jnp.where(eye_w, 1.0, 0.0)
    tinv = eye2 - ab
    ab_d = diag2(ab)
    pw = _bmm1(ab, ab_d)
    n_sq = CHUNK.bit_length() - 2
    for s in range(n_sq):
        if s + 1 < n_sq:
            both = _bmm1(pw, jnp.concatenate([diag2(tinv), diag2(pw)], axis=2))
            tinv = tinv + both[:, :, :e2]
            pw = both[:, :, e2:]
        else:
            tinv = tinv + _bmm1(pw, diag2(tinv))
    resid = eye2 - tinv - _bmm3(ab, diag2(tinv))
    tinv = tinv + _bmm1(tinv, diag2(resid))

    ve = expand(v3)
    akve = _bmm1(ak, ve)
    wu = _bmm1(tinv, jnp.concatenate([expand(kkt), expand(akve)], axis=2))
    w_, u_loc = wu[:, :, :LANES], wu[:, :, LANES:]
    yq = _bmm1(jnp.concatenate([rk, -rb], axis=2),
               jnp.concatenate([jnp.concatenate([ve, jnp.zeros_like(ve)], axis=2),
                                jnp.concatenate([expand(u_loc), expand(w_)], axis=2)], axis=1))
    per_pair = lambda x: x.reshape((pairs * nbat, nc) + x.shape[1:])
    qe_s[...] = per_pair(rt + yq[:, :, LANES:])
    yl_s[...] = per_pair(yq[:, :, :LANES])
    zp = _bmm1(jnp.swapaxes(jnp.concatenate([k3 * e_tail, -(b3 * e_tail)], axis=1), 1, 2),
               jnp.concatenate([jnp.concatenate([v3, jnp.zeros_like(v3)], axis=2),
                                jnp.concatenate([u_loc, w_], axis=2)], axis=1))
    zr = lax.broadcasted_iota(jnp.int32, (nb, HEAD_DIM, LANES), 1)
    zc = lax.broadcasted_iota(jnp.int32, (nb, HEAD_DIM, LANES), 2)
    zhead0 = zc < HEAD_DIM
    own_block = lambda x: jnp.where(zhead0, x[:, :HEAD_DIM], x[:, HEAD_DIM:])
    decay = jnp.where(jnp.where(zhead0, zc, zc - HEAD_DIM) == zr, jnp.exp(tot), 0.0)
    phi_s[...] = per_pair(decay + own_block(zp[:, :, LANES:]))
    psi_s[...] = per_pair(own_block(zp[:, :, :LANES]))
    shead0 = lax.broadcasted_iota(jnp.int32, (pairs * nbat, HEAD_DIM, LANES), 2) < HEAD_DIM

    def advance(c, _):
        rows = pl.ds(pl.multiple_of(c * CHUNK, CHUNK), CHUNK)
        z = z_s[...]
        y_s[:, rows, :] = _bmm1(qe_s[:, c], z) + yl_s[:, c]
        z_new = _bmm3(phi_s[:, c], z) + psi_s[:, c]
        z_s[...] = jnp.concatenate([jnp.where(shead0, z_new, 0.0), jnp.where(shead0, 0.0, z_new)], axis=1)
        return 0

    lax.fori_loop(0, nc, advance, 0, unroll=min(nc, 2))

    y = jnp.concatenate([y_s[p * nbat:(p + 1) * nbat].reshape(rows_all, LANES) for p in range(pairs)], axis=1)
    d = y - head_sum(y) * (1.0 / HEAD_DIM)
    var = head_sum(d * d) * (1.0 / HEAD_DIM)
    o = d * lax.rsqrt(var + GN_EPS) * lnw_ref[...] + lnb_ref[...]
    bonus = head_sum(xr * k2 * rkw_ref[...]) * xv
    gate = lora(_sigmoid(xl[:, LANES:]), wg_ref, LANES, LORA_PAD)
    o_ref[...] = ((o + bonus) * gate).reshape(nbat, tb, RW_WIDTH)

    @pl.when(t == pl.num_programs(0) - 1)
    def _():
        for p in range(pairs):
            for bi in range(nbat):
                s_pair = z_s[p * nbat + bi].T
                sout_ref[bi, 2 * p] = s_pair[:HEAD_DIM, :HEAD_DIM]
                sout_ref[bi, 2 * p + 1] = _mm_exact_rhs(s_pair[HEAD_DIM:], to_high, 3, transposed=True)


def _rwkv(rkv, lora, shift_rkv, shift_lora, z0, prm, *, tb, t_valid):
    b, t, _ = rkv.shape
    nt = t // tb
    nc = tb // CHUNK
    pairs = RW_WIDTH // LANES
    rows = lambda w: pl.BlockSpec((b, tb, w), lambda ti: (0, ti, 0))
    first = lambda w: pl.BlockSpec((b, 1, w), lambda ti: (0, 0, 0))
    state = pl.BlockSpec((b, 2 * pairs, HEAD_DIM, HEAD_DIM), lambda ti: (0, 0, 0, 0))
    vec = _const_spec((1, RW_WIDTH))
    mat = _const_spec((LORA_PAD, RW_WIDTH))
    return pl.pallas_call(
        functools.partial(_rwkv_kernel, nbat=b, tb=tb, t_valid=t_valid),
        grid=(nt,),
        in_specs=[rows(3 * RW_WIDTH), rows(LORA_PAD), first(3 * RW_WIDTH), first(LORA_PAD), state,
                  _const_spec((1, 3 * RW_WIDTH)), _const_spec((1, LORA_PAD)),
                  vec, mat, vec, mat, mat, vec, vec, vec, vec, vec],
        out_specs=[rows(RW_WIDTH), state],
        out_shape=[jax.ShapeDtypeStruct((b, t, RW_WIDTH), F32),
                   jax.ShapeDtypeStruct((b, 2 * pairs, HEAD_DIM, HEAD_DIM), F32)],
        scratch_shapes=[pltpu.VMEM((b, 1, 3 * RW_WIDTH), F32), pltpu.VMEM((b, 1, LORA_PAD), F32),
                        pltpu.VMEM((pairs * b, tb, LANES), F32), pltpu.VMEM((pairs * b, LANES, LANES), F32)]
        + [pltpu.VMEM((pairs * b, nc, CHUNK, LANES), F32)] * 2
        + [pltpu.VMEM((pairs * b, nc, HEAD_DIM, LANES), F32)] * 2,
        compiler_params=pltpu.CompilerParams(dimension_semantics=("arbitrary",),
                                             vmem_limit_bytes=VMEM_LIMIT),
        name="rwkv7",
    )(rkv, lora, shift_rkv, shift_lora, z0, prm["mu_rkv"], prm["mu_lora"],
      prm["w0"], prm["wd"], prm["a0"], prm["wa"], prm["wg"],
      prm["k_k"], prm["k_a"], prm["r_k"], prm["ln_w"], prm["ln_b"])


def _out_ffn_kernel(x_ref, sb_ref, rw_ref, woa_ref, wob_ref, gf_ref, wg_ref, wu_ref, wd_ref, gl_ref,
                    y_ref):
    x1 = x_ref[...] + _dot(sb_ref[...].astype(BF16), woa_ref[...]) \
        + _dot(rw_ref[...].astype(BF16), wob_ref[...])
    h2 = _rmsnorm(x1, gf_ref[...]).astype(BF16)
    gate = _dot(h2, wg_ref[...])
    act = gate * _sigmoid(gate) * _dot(h2, wu_ref[...])
    x2 = x1 + _dot(act.astype(BF16), wd_ref[...])
    y_ref[...] = _rmsnorm(x2, gl_ref[...])


def _out_ffn(x2d, o_sb, o_rw, woa, wob, gf, wg, wu, wd, gl, tm):
    m, d = x2d.shape
    row = lambda w: pl.BlockSpec((tm, w), lambda i: (i, 0))
    return pl.pallas_call(
        _out_ffn_kernel,
        grid=(m // tm,),
        in_specs=[row(d), row(SB_WIDTH), row(RW_WIDTH), _const_spec(woa.shape), _const_spec(wob.shape),
                  _const_spec((1, d)), _const_spec(wg.shape), _const_spec(wu.shape),
                  _const_spec(wd.shape), _const_spec((1, d))],
        out_specs=row(d),
        out_shape=jax.ShapeDtypeStruct((m, d), F32),
        compiler_params=pltpu.CompilerParams(dimension_semantics=("arbitrary",),
                                             vmem_limit_bytes=VMEM_LIMIT),
        name="out_ffn",
    )(x2d, o_sb, o_rw, woa, wob, gf, wg, wu, wd, gl)


def _pad_rows(x, rows):
    return jnp.pad(x, ((0, 0), (0, rows - x.shape[1]), (0, 0)))


def _pad_lanes(x, lanes):
    return jnp.pad(x, [(0, 0)] * (x.ndim - 1) + [(0, lanes - x.shape[-1])])


def _layer(x, k_past, v_past, wkv0, shift0, w, *, tm, tq, tb):
    b, t, d = x.shape
    x2d = x.reshape(b * t, d)
    to_feature_major = lambda a: jnp.transpose(a, (0, 2, 3, 1)).reshape(b, SB_WIDTH, a.shape[1])
    from_feature_major = lambda a: jnp.transpose(a.reshape(b, SB_HEADS, HEAD_DIM, a.shape[2]), (0, 3, 1, 2))
    if k_past is None:
        q, k, v, rkv, lora = _norm_proj(x, w["norm_mix_g"], w["w_q"], w["w_kv_t"], w["w_rkv"], w["w_lora"],
                                        tm, feature_major_kv=True)
        o_sb = _sb_attention(q.reshape(b, t, SB_WIDTH), k, v, k, v, tq=tq, causal_prefix=True,
                             own_transposed=True)
        k_heads, v_heads = from_feature_major(k), from_feature_major(v)
    else:
        q, k, v, rkv, lora = _norm_proj(x, w["norm_mix_g"], w["w_q"], w["w_kv"], w["w_rkv"], w["w_lora"],
                                        tm, feature_major_kv=False)
        q3, k3, v3 = (a.reshape(b, t, SB_WIDTH) for a in (q, k, v))
        o_sb = _sb_attention(q3, _pad_rows(k3, KEY_BLOCK), _pad_rows(v3, KEY_BLOCK),
                             to_feature_major(k_past), to_feature_major(v_past), tq=tq,
                             causal_prefix=False, own_transposed=False)
        k_heads, v_heads = (a.reshape(b, t, SB_HEADS, HEAD_DIM) for a in (k, v))

    rkv3 = rkv.reshape(b, t, 3 * RW_WIDTH)
    lora3 = lora.reshape(b, t, LORA_PAD)
    t_pad = -(-t // tb) * tb
    o_rw, wkv_out = _rwkv(_pad_rows(rkv3, t_pad), _pad_rows(lora3, t_pad),
                          shift0[..., :3 * RW_WIDTH], _pad_lanes(shift0[..., 3 * RW_WIDTH:], LORA_PAD),
                          wkv0, w, tb=tb, t_valid=min(t, tb))
    o_rw = o_rw[:, :t]
    shift_last = jnp.concatenate([rkv3[:, t - 1:, :], lora3[:, t - 1:, :LORA_WIDTH]], axis=-1)

    y = _out_ffn(x2d, o_sb.reshape(b * t, SB_WIDTH), o_rw.reshape(b * t, RW_WIDTH),
                 w["w_out_sb"], w["w_out_rw"], w["norm_ffn_g"], w["w_gate"], w["w_up"], w["w_down"],
                 w["norm_final_g"], tm)
    return y.reshape(b, t, d), k_heads, v_heads, wkv_out, shift_last


def kernel(x_prompt, x_sample, cache_k, cache_v, state_wkv, state_shift, norm_mix_g, w_in, mu_shift, w0,
           w_decay_up, a0, w_aaa_up, w_gate_up, k_k, k_a, r_k, ln_x_w, ln_x_b, w_out, norm_ffn_g, w_gate,
           w_up, w_down, norm_final_g):
    assert w_in.shape[0] == 1, "single-layer trunk"
    l = 0
    rw3 = 3 * RW_WIDTH
    lora_rows = lambda m, lo, hi: jnp.pad(m, ((lo, LORA_PAD - hi), (0, 0)))
    w = {
        "norm_mix_g": norm_mix_g[l][None, :],
        "w_q": w_in[l][:, :SB_WIDTH].astype(BF16),
        "w_kv": w_in[l][:, SB_WIDTH:3 * SB_WIDTH].astype(BF16),
        "w_kv_t": w_in[l][:, SB_WIDTH:3 * SB_WIDTH].T.astype(BF16),
        "w_rkv": w_in[l][:, 3 * SB_WIDTH:3 * SB_WIDTH + rw3].astype(BF16),
        "w_lora": _pad_lanes(w_in[l][:, 3 * SB_WIDTH + rw3:], LORA_PAD).astype(BF16),
        "mu_rkv": mu_shift[l][None, :rw3],
        "mu_lora": _pad_lanes(mu_shift[l][None, rw3:], LORA_PAD),
        "w0": w0[l][None, :],
        "wd": lora_rows(w_decay_up[l], 0, LORA_DECAY),
        "a0": a0[l][None, :],
        "wa": lora_rows(w_aaa_up[l], LORA_DECAY, LORA_DECAY + LORA_AAA),
        "wg": lora_rows(w_gate_up[l], LORA_DECAY + LORA_AAA, LORA_WIDTH),
        "k_k": k_k[l][None, :],
        "k_a": k_a[l][None, :],
        "r_k": r_k[l].reshape(1, RW_WIDTH),
        "ln_w": ln_x_w[l][None, :],
        "ln_b": ln_x_b[l][None, :],
        "w_out_sb": w_out[l][:SB_WIDTH].astype(BF16),
        "w_out_rw": w_out[l][SB_WIDTH:].astype(BF16),
        "norm_ffn_g": norm_ffn_g[l][None, :],
        "w_gate": w_gate[l].astype(BF16),
        "w_up": w_up[l].astype(BF16),
        "w_down": w_down[l].astype(BF16),
        "norm_final_g": norm_final_g[None, :],
    }
    bp = x_prompt.shape[0]
    heads = RW_WIDTH // HEAD_DIM
    wkv_zero = jnp.zeros((bp, heads, HEAD_DIM, HEAD_DIM), x_prompt.dtype)
    shift_zero = jnp.zeros((bp, 1, rw3 + LORA_WIDTH), x_prompt.dtype)
    yp, k1, v1, s1, sh1 = _layer(x_prompt, None, None, wkv_zero, shift_zero, w, tm=512, tq=128, tb=2 * CHUNK)
    ys, k2, v2, s2, sh2 = _layer(x_sample, cache_k[l], cache_v[l], state_wkv[l], state_shift[l], w,
                                 tm=256, tq=x_sample.shape[1], tb=CHUNK)
    return (yp, ys, k1[None], v1[None], s1[None], sh1[None], k2[None], v2[None], s2[None], sh2[None])
```

```python
import functools
import math

import jax
import jax.numpy as jnp
from jax import lax
from jax.experimental import pallas as pl
from jax.experimental.pallas import tpu as pltpu

F32 = jnp.float32
BF16 = jnp.bfloat16

LANES = 128
HEAD_DIM = 64
SB_WIDTH = 512
SB_HEADS = SB_WIDTH // HEAD_DIM
RW_WIDTH = 512
LORA_DECAY = 64
LORA_AAA = 64
LORA_GATE = 160
LORA_WIDTH = LORA_DECAY + LORA_AAA + LORA_GATE
LORA_PAD = 384
RMS_EPS = 1e-6
GN_EPS = 64e-5
DECAY_SCALE = math.exp(-0.5)
KEY_BLOCK = 128
SB_DEAD_LOG = -104.0
SB_WINDOW = 3
CHUNK = 64
VMEM_LIMIT = 56 * 1024 * 1024


def _dot(a, b):
    return jnp.dot(a, b, preferred_element_type=F32)


def _split2(x):
    hi = x.astype(BF16)
    lo = (x - hi.astype(F32)).astype(BF16)
    return hi, lo


def _bdot(a, b):
    return jnp.einsum("bij,bjk->bik", a, b, preferred_element_type=F32)


def _bdot_nt(a, b):
    return jnp.einsum("bik,bjk->bij", a, b, preferred_element_type=F32)


def _bmm1(a, b):
    return _bdot(a.astype(BF16), b.astype(BF16))


def _bmm3(a, b):
    ah, al = _split2(a)
    bh, bl = _split2(b)
    return _bdot(ah, bh) + (_bdot(ah, bl) + _bdot(al, bh))


def _mm_exact_rhs(x, m, parts=2, transposed=False):
    contract = (((1,), (1 if transposed else 0,)), ((), ()))
    acc = None
    for _ in range(parts):
        part = x.astype(BF16)
        prod = lax.dot_general(part, m, contract, preferred_element_type=F32)
        acc = prod if acc is None else acc + prod
        x = x - part.astype(F32)
    return acc


def _sigmoid(x):
    return 0.5 * jnp.tanh(0.5 * x) + 0.5


def _rmsnorm(x, g):
    return x * lax.rsqrt(jnp.mean(x * x, axis=-1, keepdims=True) + RMS_EPS) * g


def _const_spec(shape):
    return pl.BlockSpec(shape, lambda *_: (0,) * len(shape), pipeline_mode=pl.Buffered(1))


def _norm_proj_kernel(x_ref, g_ref, wq_ref, wkv_ref, wrkv_ref, wl_ref,
                      q_ref, k_ref, v_ref, rkv_ref, l_ref, *, feature_major_kv):
    hb = _rmsnorm(x_ref[...], g_ref[...]).astype(BF16)
    q_ref[...] = _dot(hb, wq_ref[...])
    if feature_major_kv:
        kv = lax.dot_general(wkv_ref[...], hb, (((1,), (1,)), ((), ())), preferred_element_type=F32)
        k_ref[0] = kv[:SB_WIDTH]
        v_ref[0] = kv[SB_WIDTH:]
    else:
        kv = _dot(hb, wkv_ref[...])
        k_ref[...] = kv[:, :SB_WIDTH]
        v_ref[...] = kv[:, SB_WIDTH:]
    rkv_ref[...] = _dot(hb, wrkv_ref[...])
    l_ref[...] = _dot(hb, wl_ref[...])


def _norm_proj(x, g, wq, wkv, wrkv, wl, tm, feature_major_kv):
    b, t, d = x.shape
    m = b * t
    row = lambda w: pl.BlockSpec((tm, w), lambda i: (i, 0))
    if feature_major_kv:
        per_batch = t // tm
        kv_spec = pl.BlockSpec((1, SB_WIDTH, tm), lambda i: (i // per_batch, 0, i % per_batch))
        kv_shape = jax.ShapeDtypeStruct((b, SB_WIDTH, t), F32)
    else:
        kv_spec, kv_shape = row(SB_WIDTH), jax.ShapeDtypeStruct((m, SB_WIDTH), F32)
    return pl.pallas_call(
        functools.partial(_norm_proj_kernel, feature_major_kv=feature_major_kv),
        grid=(m // tm,),
        in_specs=[row(d), _const_spec((1, d)), _const_spec(wq.shape), _const_spec(wkv.shape),
                  _const_spec(wrkv.shape), _const_spec(wl.shape)],
        out_specs=[row(SB_WIDTH), kv_spec, kv_spec, row(3 * RW_WIDTH), row(LORA_PAD)],
        out_shape=[jax.ShapeDtypeStruct((m, SB_WIDTH), F32), kv_shape, kv_shape,
                   jax.ShapeDtypeStruct((m, 3 * RW_WIDTH), F32), jax.ShapeDtypeStruct((m, LORA_PAD), F32)],
        compiler_params=pltpu.CompilerParams(dimension_semantics=("arbitrary",),
                                             vmem_limit_bytes=VMEM_LIMIT),
        name="norm_proj",
    )(x.reshape(m, d), g, wq, wkv, wrkv, wl)


def _sb_kernel(q_ref, *refs, tq, own_in_prefix, n_prefix):
    if own_in_prefix:
        kp_ref, vp_ref, o_ref = refs
    else:
        kd_ref, vd_ref, kp_ref, vp_ref, o_ref = refs
    pairs = SB_WIDTH // LANES
    by_pair = lambda x: jnp.stack([x[:, p * LANES:(p + 1) * LANES] for p in range(pairs)])
    q = by_pair(q_ref[0] * (1.0 / math.sqrt(HEAD_DIM)))
    first = lax.broadcasted_iota(jnp.int32, (pairs, tq, LANES), 2) < HEAD_DIM
    qe = jnp.concatenate([jnp.where(first, q, 0.0), jnp.where(first, 0.0, q)], axis=1).astype(BF16)

    row = lax.broadcasted_iota(jnp.int32, (pairs, 2 * tq, KEY_BLOCK), 1)
    col = lax.broadcasted_iota(jnp.int32, (pairs, 2 * tq, KEY_BLOCK), 2)
    causal = col < jnp.where(row >= tq, row - tq, row)

    jj = lax.broadcasted_iota(jnp.int32, (KEY_BLOCK, 2 * KEY_BLOCK), 0)
    ss = lax.broadcasted_iota(jnp.int32, (KEY_BLOCK, 2 * KEY_BLOCK), 1)
    suffix = jnp.where((jj > ss) | (ss >= KEY_BLOCK), 1.0, 0.0).astype(BF16)
    suffix2 = jnp.concatenate([suffix, suffix], axis=0)

    def prefix(ref, start, n_keys):
        return ref[0, :, pl.ds(start, n_keys)].reshape(pairs, LANES, n_keys)

    def suffix_sums(log1mb):
        hi_lo = jnp.concatenate(_split2(log1mb.reshape(pairs * 2 * tq, KEY_BLOCK)), axis=1)
        return _dot(hi_lo, suffix2).reshape(pairs, 2 * tq, 2 * KEY_BLOCK)

    def softplus(z):
        return jnp.maximum(z, 0.0) + jnp.log(1.0 + jnp.exp(-jnp.abs(z)))

    def own_block(kb, vb, c, o, mask):
        z = _bdot_nt(qe, kb.astype(BF16))
        log1mb = jnp.where(mask, -softplus(z), 0.0)
        sums = suffix_sums(log1mb)
        w = jnp.where(mask, jnp.exp(z + log1mb + (sums[:, :, :KEY_BLOCK] + c)), 0.0)
        return c + sums[:, :, KEY_BLOCK:], o + _bdot(w.astype(BF16), vb.astype(BF16))

    def sweep(kb, vb, c, o, mask=None):
        z = _bdot(qe, kb.astype(BF16))
        log1mb = -softplus(z)
        if mask is not None:
            log1mb = jnp.where(mask, log1mb, 0.0)
        n_blocks = kb.shape[2] // KEY_BLOCK
        after = [None] * n_blocks
        for j in reversed(range(n_blocks)):
            sums = suffix_sums(log1mb[:, :, j * KEY_BLOCK:(j + 1) * KEY_BLOCK])
            after[j] = sums[:, :, :KEY_BLOCK] + c
            c = c + sums[:, :, KEY_BLOCK:]
        w = jnp.exp(z + log1mb + jnp.concatenate(after, axis=2))
        if mask is not None:
            w = jnp.where(mask, w, 0.0)
        return c, o + _bdot_nt(w.astype(BF16), vb.astype(BF16))

    alive = lambda c: (jnp.max(c) >= SB_DEAD_LOG).astype(jnp.int32)
    zeros = jnp.zeros((pairs, 2 * tq, LANES), F32)
    if own_in_prefix:
        i = pl.program_id(1)
        n = jnp.maximum(i - (SB_WINDOW - 1), 0)
        start = pl.multiple_of(n * KEY_BLOCK, KEY_BLOCK)
        wrow = lax.broadcasted_iota(jnp.int32, (pairs, 2 * tq, SB_WINDOW * KEY_BLOCK), 1)
        wcol = lax.broadcasted_iota(jnp.int32, (pairs, 2 * tq, SB_WINDOW * KEY_BLOCK), 2)
        visible = start + wcol < i * tq + jnp.where(wrow >= tq, wrow - tq, wrow)
        c, o = sweep(prefix(kp_ref, start, SB_WINDOW * KEY_BLOCK), prefix(vp_ref, start, SB_WINDOW * KEY_BLOCK),
                     zeros, zeros, visible)
        still = alive(c)
    else:
        c, o = own_block(by_pair(kd_ref[0]), by_pair(vd_ref[0]), zeros, zeros, causal)
        n = n_prefix
        still = jnp.int32(1)

    def live(carry):
        return (carry[0] < (n >> 1)) & (carry[1] > 0)

    def body(carry):
        it, _, c, o = carry
        start = pl.multiple_of((n - 2 - 2 * it) * KEY_BLOCK, KEY_BLOCK)
        c, o = sweep(prefix(kp_ref, start, 2 * KEY_BLOCK), prefix(vp_ref, start, 2 * KEY_BLOCK), c, o)
        return it + 1, alive(c), c, o

    _, still, c, o = lax.while_loop(live, body, (jnp.int32(0), still, c, o))
    c, o = lax.cond(((n & 1) == 1) & (still > 0),
                    lambda c, o: sweep(prefix(kp_ref, 0, KEY_BLOCK), prefix(vp_ref, 0, KEY_BLOCK), c, o),
                    lambda c, o: (c, o), c, o)
    o = jnp.where(first, o[:, :tq], o[:, tq:])
    o_ref[0] = jnp.concatenate([o[p] for p in range(pairs)], axis=1)


def _sb_attention(q, kp, vp, own=None, *, tq):
    b, t, _ = q.shape
    tp = kp.shape[2]
    blk = lambda rows: pl.BlockSpec((1, rows, SB_WIDTH), lambda bi, i: (bi, i, 0))
    pre = pl.BlockSpec((1,) + kp.shape[1:], lambda bi, i: (bi, 0, 0))
    if own is None:
        assert tq == KEY_BLOCK and tp == t and t >= SB_WINDOW * KEY_BLOCK
        in_specs, args = [blk(tq), pre, pre], (q, kp, vp)
    else:
        in_specs, args = [blk(tq), blk(KEY_BLOCK), blk(KEY_BLOCK), pre, pre], (q,) + tuple(own) + (kp, vp)
    return pl.pallas_call(
        functools.partial(_sb_kernel, tq=tq, own_in_prefix=own is None, n_prefix=tp // KEY_BLOCK),
        grid=(b, t // tq),
        in_specs=in_specs,
        out_specs=blk(tq),
        out_shape=jax.ShapeDtypeStruct(q.shape, F32),
        compiler_params=pltpu.CompilerParams(dimension_semantics=("arbitrary",) * 2,
                                             vmem_limit_bytes=VMEM_LIMIT),
        name="stickbreak",
    )(*args)


def _rwkv_kernel(x_ref, l_ref, sx_ref, sl_ref, s0_ref, mux_ref, mul_ref, w0_ref, wd_ref, a0_ref, wa_ref,
                 wg_ref, kkw_ref, kaw_ref, rkw_ref, lnw_ref, lnb_ref,
                 o_ref, sout_ref,
                 px_s, pl_s, y_s, z_s, qe_s, yl_s, phi_s, psi_s,
                 *, nbat, tb, t_valid):
    t = pl.program_id(0)
    pairs = RW_WIDTH // LANES
    rows_all = nbat * tb
    pr = lax.broadcasted_iota(jnp.int32, (HEAD_DIM, LANES), 0)
    pc = lax.broadcasted_iota(jnp.int32, (HEAD_DIM, LANES), 1)
    to_low = jnp.where(pc == pr, 1.0, 0.0).astype(BF16)
    to_high = jnp.where(pc == pr + HEAD_DIM, 1.0, 0.0).astype(BF16)

    @pl.when(t == 0)
    def _():
        px_s[...] = sx_ref[...]
        pl_s[...] = sl_ref[...]
        for p in range(pairs):
            for bi in range(nbat):
                z_s[p * nbat + bi] = jnp.concatenate(
                    [_mm_exact_rhs(s0_ref[bi, 2 * p], to_low, 3),
                     _mm_exact_rhs(s0_ref[bi, 2 * p + 1], to_high, 3)], axis=0).T

    def shift(x_ref, carry_s, mu_ref):
        x = x_ref[...]
        prev = pltpu.roll(x.reshape(rows_all, x.shape[2]), 1, 0).reshape(x.shape)
        prev = jnp.where(lax.broadcasted_iota(jnp.int32, x.shape, 1) == 0, carry_s[...], prev)
        carry_s[...] = x[:, tb - 1:tb, :]
        return (x + (prev - x) * mu_ref[...]).reshape(rows_all, x.shape[2])

    xs = shift(x_ref, px_s, mux_ref)
    xr, xk, xv = xs[:, :RW_WIDTH], xs[:, RW_WIDTH:2 * RW_WIDTH], xs[:, 2 * RW_WIDTH:]
    xl = shift(l_ref, pl_s, mul_ref)

    hr = lax.broadcasted_iota(jnp.int32, (LANES, LANES), 0) // HEAD_DIM
    hc = lax.broadcasted_iota(jnp.int32, (LANES, LANES), 1) // HEAD_DIM
    head_ones = jnp.where(hr == hc, 1.0, 0.0).astype(BF16)
    head_sum = lambda x, parts=2: jnp.concatenate(
        [_mm_exact_rhs(x[:, p * LANES:(p + 1) * LANES], head_ones, parts) for p in range(pairs)], axis=1)

    lora = lambda x, w_ref, lo, hi: _dot(x.astype(BF16), w_ref[lo:hi, :].astype(BF16))
    log_w = -DECAY_SCALE * _sigmoid(w0_ref[...] + lora(jnp.tanh(xl[:, :LANES]), wd_ref, 0, LANES))
    a = _sigmoid(a0_ref[...] + lora(xl[:, :LANES], wa_ref, 0, LANES))
    kk = xk * kkw_ref[...]
    kk = kk * lax.rsqrt(jnp.maximum(head_sum(kk * kk, 1), 1e-24))
    k2 = xk * (1.0 + (a - 1.0) * kaw_ref[...])
    b = kk * a
    if t_valid < tb:
        valid = lax.broadcasted_iota(jnp.int32, (nbat, tb, RW_WIDTH), 1).reshape(rows_all, RW_WIDTH) < t_valid
        log_w = jnp.where(valid, log_w, 0.0)
        kk = jnp.where(valid, kk, 0.0)
        b = jnp.where(valid, b, 0.0)
        k2 = jnp.where(valid, k2, 0.0)
        xv = jnp.where(valid, xv, 0.0)
    nc = tb // CHUNK
    nb = pairs * nbat * nc
    e2 = 2 * CHUNK
    by_chunk = lambda x: jnp.stack(
        [x[:, p * LANES:(p + 1) * LANES] for p in range(pairs)]).reshape(nb, CHUNK, LANES)
    ti = lax.broadcasted_iota(jnp.int32, (nb, CHUNK, CHUNK), 1)
    tj = lax.broadcasted_iota(jnp.int32, (nb, CHUNK, CHUNK), 2)
    tri = jnp.where(tj <= ti, 1.0, 0.0).astype(BF16)
    lw3 = by_chunk(log_w)
    lw_hi, lw_lo = _split2(lw3)
    cum3 = _bdot(tri, lw_hi) + _bdot(tri, lw_lo)
    tot = cum3[:, CHUNK - 1:CHUNK, :]
    first = lax.broadcasted_iota(jnp.int32, (nb, CHUNK, LANES), 2) < HEAD_DIM
    expand = lambda x: jnp.concatenate([jnp.where(first, x, 0.0), jnp.where(first, 0.0, x)], axis=1)
    e_neg = jnp.exp(-cum3)
    e_tail = jnp.exp(tot - cum3)
    kk3, b3, k3, v3 = by_chunk(kk), by_chunk(b), by_chunk(k2), by_chunk(xv)
    kkt = kk3 * jnp.exp(cum3 - lw3)
    rt = by_chunk(xr) * jnp.exp(cum3)

    wr = lax.broadcasted_iota(jnp.int32, (nb, CHUNK, e2), 1)
    wc = lax.broadcasted_iota(jnp.int32, (nb, CHUNK, e2), 2)
    head0 = wc < CHUNK
    ws = jnp.where(head0, wc, wc - CHUNK)
    strict, incl, eye_w = ws < wr, ws <= wr, ws == wr
    diag2 = lambda x: jnp.concatenate([jnp.where(head0, x, 0.0), jnp.where(head0, 0.0, x)], axis=1)

    gram = _bdot_nt(jnp.concatenate([kkt, rt], axis=1).astype(BF16),
                    jnp.concatenate([expand(k3 * e_neg), expand(b3 * e_neg)], axis=1).astype(BF16))
    ak = jnp.where(strict, gram[:, :CHUNK, :e2], 0.0)
    ab = jnp.where(strict, gram[:, :CHUNK, e2:], 0.0)
    rk = jnp.where(incl, gram[:, CHUNK:, :e2], 0.0)
    rb = jnp.where(incl, gram[:, CHUNK:, e2:], 0.0)

    eye2 = jnp.where(eye_w, 1.0, 0.0)
    tinv = eye2 - ab
    ab_d = diag2(ab)
    pw = _bmm1(ab, ab_d)
    n_sq = CHUNK.bit_length() - 2
    for s in range(n_sq):
        if s + 1 < n_sq:
            both = _bmm1(pw, jnp.concatenate([diag2(tinv), diag2(pw)], axis=2))
            tinv = tinv + both[:, :, :e2]
            pw = both[:, :, e2:]
        else:
            tinv = tinv + _bmm1(pw, diag2(tinv))
    resid = eye2 - tinv - _bmm3(ab, diag2(tinv))
    tinv = tinv + _bmm1(tinv, diag2(resid))

    ve = expand(v3)
    akve = _bmm1(ak, ve)
    wu = _bmm1(tinv, jnp.concatenate([expand(kkt), expand(akve)], axis=2))
    w_, u_loc = wu[:, :, :LANES], wu[:, :, LANES:]
    yq = _bmm1(jnp.concatenate([rk, -rb], axis=2),
               jnp.concatenate([jnp.concatenate([ve, jnp.zeros_like(ve)], axis=2),
                                jnp.concatenate([expand(u_loc), expand(w_)], axis=2)], axis=1))
    per_pair = lambda x: x.reshape((pairs * nbat, nc) + x.shape[1:])
    qe_s[...] = per_pair(rt + yq[:, :, LANES:])
    yl_s[...] = per_pair(yq[:, :, :LANES])
    zp = _bmm1(jnp.swapaxes(jnp.concatenate([k3 * e_tail, -(b3 * e_tail)], axis=1), 1, 2),
               jnp.concatenate([jnp.concatenate([v3, jnp.zeros_like(v3)], axis=2),
                                jnp.concatenate([u_loc, w_], axis=2)], axis=1))
    zr = lax.broadcasted_iota(jnp.int32, (nb, HEAD_DIM, LANES), 1)
    zc = lax.broadcasted_iota(jnp.int32, (nb, HEAD_DIM, LANES), 2)
    zhead0 = zc < HEAD_DIM
    own_block = lambda x: jnp.where(zhead0, x[:, :HEAD_DIM], x[:, HEAD_DIM:])
    decay = jnp.where(jnp.where(zhead0, zc, zc - HEAD_DIM) == zr, jnp.exp(tot), 0.0)
    phi_s[...] = per_pair(decay + own_block(zp[:, :, LANES:]))
    psi_s[...] = per_pair(own_block(zp[:, :, :LANES]))
    shead0 = lax.broadcasted_iota(jnp.int32, (pairs * nbat, HEAD_DIM, LANES), 2) < HEAD_DIM

    def advance(c, _):
        rows = pl.ds(pl.multiple_of(c * CHUNK, CHUNK), CHUNK)
        z = z_s[...]
        y_s[:, rows, :] = _bmm1(qe_s[:, c], z) + yl_s[:, c]
        z_new = _bmm3(phi_s[:, c], z) + psi_s[:, c]
        z_s[...] = jnp.concatenate([jnp.where(shead0, z_new, 0.0), jnp.where(shead0, 0.0, z_new)], axis=1)
        return 0

    lax.fori_loop(0, nc, advance, 0, unroll=min(nc, 2))

    y = jnp.concatenate([y_s[p * nbat:(p + 1) * nbat].reshape(rows_all, LANES) for p in range(pairs)], axis=1)
    d = y - head_sum(y) * (1.0 / HEAD_DIM)
    var = head_sum(d * d) * (1.0 / HEAD_DIM)
    o = d * lax.rsqrt(var + GN_EPS) * lnw_ref[...] + lnb_ref[...]
    bonus = head_sum(xr * k2 * rkw_ref[...]) * xv
    gate = lora(_sigmoid(xl[:, LANES:]), wg_ref, LANES, LORA_PAD)
    o_ref[...] = ((o + bonus) * gate).reshape(nbat, tb, RW_WIDTH)

    @pl.when(t == pl.num_programs(0) - 1)
    def _():
        for p in range(pairs):
            for bi in range(nbat):
                s_pair = z_s[p * nbat + bi].T
                sout_ref[bi, 2 * p] = s_pair[:HEAD_DIM, :HEAD_DIM]
                sout_ref[bi, 2 * p + 1] = _mm_exact_rhs(s_pair[HEAD_DIM:], to_high, 3, transposed=True)


def _rwkv(rkv, lora, shift_rkv, shift_lora, z0, prm, *, tb, t_valid):
    b, t, _ = rkv.shape
    nt = t // tb
    nc = tb // CHUNK
    pairs = RW_WIDTH // LANES
    rows = lambda w: pl.BlockSpec((b, tb, w), lambda ti: (0, ti, 0))
    first = lambda w: pl.BlockSpec((b, 1, w), lambda ti: (0, 0, 0))
    state = pl.BlockSpec((b, 2 * pairs, HEAD_DIM, HEAD_DIM), lambda ti: (0, 0, 0, 0))
    vec = _const_spec((1, RW_WIDTH))
    mat = _const_spec((LORA_PAD, RW_WIDTH))
    return pl.pallas_call(
        functools.partial(_rwkv_kernel, nbat=b, tb=tb, t_valid=t_valid),
        grid=(nt,),
        in_specs=[rows(3 * RW_WIDTH), rows(LORA_PAD), first(3 * RW_WIDTH), first(LORA_PAD), state,
                  _const_spec((1, 3 * RW_WIDTH)), _const_spec((1, LORA_PAD)),
                  vec, mat, vec, mat, mat, vec, vec, vec, vec, vec],
        out_specs=[rows(RW_WIDTH), state],
        out_shape=[jax.ShapeDtypeStruct((b, t, RW_WIDTH), F32),
                   jax.ShapeDtypeStruct((b, 2 * pairs, HEAD_DIM, HEAD_DIM), F32)],
        scratch_shapes=[pltpu.VMEM((b, 1, 3 * RW_WIDTH), F32), pltpu.VMEM((b, 1, LORA_PAD), F32),
                        pltpu.VMEM((pairs * b, tb, LANES), F32), pltpu.VMEM((pairs * b, LANES, LANES), F32)]
        + [pltpu.VMEM((pairs * b, nc, CHUNK, LANES), F32)] * 2
        + [pltpu.VMEM((pairs * b, nc, HEAD_DIM, LANES), F32)] * 2,
        compiler_params=pltpu.CompilerParams(dimension_semantics=("arbitrary",),
                                             vmem_limit_bytes=VMEM_LIMIT),
        name="rwkv7",
    )(rkv, lora, shift_rkv, shift_lora, z0, prm["mu_rkv"], prm["mu_lora"],
      prm["w0"], prm["wd"], prm["a0"], prm["wa"], prm["wg"],
      prm["k_k"], prm["k_a"], prm["r_k"], prm["ln_w"], prm["ln_b"])


def _out_ffn_kernel(x_ref, sb_ref, rw_ref, woa_ref, wob_ref, gf_ref, wg_ref, wu_ref, wd_ref, gl_ref,
                    y_ref):
    x1 = x_ref[...] + _dot(sb_ref[...].astype(BF16), woa_ref[...]) \
        + _dot(rw_ref[...].astype(BF16), wob_ref[...])
    h2 = _rmsnorm(x1, gf_ref[...]).astype(BF16)
    gate = _dot(h2, wg_ref[...])
    act = gate * _sigmoid(gate) * _dot(h2, wu_ref[...])
    x2 = x1 + _dot(act.astype(BF16), wd_ref[...])
    y_ref[...] = _rmsnorm(x2, gl_ref[...])


def _out_ffn(x2d, o_sb, o_rw, woa, wob, gf, wg, wu, wd, gl, tm):
    m, d = x2d.shape
    row = lambda w: pl.BlockSpec((tm, w), lambda i: (i, 0))
    return pl.pallas_call(
        _out_ffn_kernel,
        grid=(m // tm,),
        in_specs=[row(d), row(SB_WIDTH), row(RW_WIDTH), _const_spec(woa.shape), _const_spec(wob.shape),
                  _const_spec((1, d)), _const_spec(wg.shape), _const_spec(wu.shape),
                  _const_spec(wd.shape), _const_spec((1, d))],
        out_specs=row(d),
        out_shape=jax.ShapeDtypeStruct((m, d), F32),
        compiler_params=pltpu.CompilerParams(dimension_semantics=("arbitrary",),
                                             vmem_limit_bytes=VMEM_LIMIT),
        name="out_ffn",
    )(x2d, o_sb, o_rw, woa, wob, gf, wg, wu, wd, gl)


def _pad_rows(x, rows):
    return jnp.pad(x, ((0, 0), (0, rows - x.shape[1]), (0, 0)))


def _pad_lanes(x, lanes):
    return jnp.pad(x, [(0, 0)] * (x.ndim - 1) + [(0, lanes - x.shape[-1])])


def _layer(x, k_past, v_past, wkv0, shift0, w, *, tm, tq, tb):
    b, t, d = x.shape
    x2d = x.reshape(b * t, d)
    to_feature_major = lambda a: jnp.transpose(a, (0, 2, 3, 1)).reshape(b, SB_WIDTH, a.shape[1])
    from_feature_major = lambda a: jnp.transpose(a.reshape(b, SB_HEADS, HEAD_DIM, a.shape[2]), (0, 3, 1, 2))
    if k_past is None:
        q, k, v, rkv, lora = _norm_proj(x, w["norm_mix_g"], w["w_q"], w["w_kv_t"], w["w_rkv"], w["w_lora"],
                                        tm, feature_major_kv=True)
        o_sb = _sb_attention(q.reshape(b, t, SB_WIDTH), k, v, tq=tq)
        k_heads, v_heads = from_feature_major(k), from_feature_major(v)
    else:
        q, k, v, rkv, lora = _norm_proj(x, w["norm_mix_g"], w["w_q"], w["w_kv"], w["w_rkv"], w["w_lora"],
                                        tm, feature_major_kv=False)
        q3, k3, v3 = (a.reshape(b, t, SB_WIDTH) for a in (q, k, v))
        o_sb = _sb_attention(q3, to_feature_major(k_past), to_feature_major(v_past),
                             (_pad_rows(k3, KEY_BLOCK), _pad_rows(v3, KEY_BLOCK)), tq=tq)
        k_heads, v_heads = (a.reshape(b, t, SB_HEADS, HEAD_DIM) for a in (k, v))

    rkv3 = rkv.reshape(b, t, 3 * RW_WIDTH)
    lora3 = lora.reshape(b, t, LORA_PAD)
    t_pad = -(-t // tb) * tb
    o_rw, wkv_out = _rwkv(_pad_rows(rkv3, t_pad), _pad_rows(lora3, t_pad),
                          shift0[..., :3 * RW_WIDTH], _pad_lanes(shift0[..., 3 * RW_WIDTH:], LORA_PAD),
                          wkv0, w, tb=tb, t_valid=min(t, tb))
    o_rw = o_rw[:, :t]
    shift_last = jnp.concatenate([rkv3[:, t - 1:, :], lora3[:, t - 1:, :LORA_WIDTH]], axis=-1)

    y = _out_ffn(x2d, o_sb.reshape(b * t, SB_WIDTH), o_rw.reshape(b * t, RW_WIDTH),
                 w["w_out_sb"], w["w_out_rw"], w["norm_ffn_g"], w["w_gate"], w["w_up"], w["w_down"],
                 w["norm_final_g"], tm)
    return y.reshape(b, t, d), k_heads, v_heads, wkv_out, shift_last


def kernel(x_prompt, x_sample, cache_k, cache_v, state_wkv, state_shift, norm_mix_g, w_in, mu_shift, w0,
           w_decay_up, a0, w_aaa_up, w_gate_up, k_k, k_a, r_k, ln_x_w, ln_x_b, w_out, norm_ffn_g, w_gate,
           w_up, w_down, norm_final_g):
    assert w_in.shape[0] == 1, "single-layer trunk"
    l = 0
    rw3 = 3 * RW_WIDTH
    lora_rows = lambda m, lo, hi: jnp.pad(m, ((lo, LORA_PAD - hi), (0, 0)))
    w = {
        "norm_mix_g": norm_mix_g[l][None, :],
        "w_q": w_in[l][:, :SB_WIDTH].astype(BF16),
        "w_kv": w_in[l][:, SB_WIDTH:3 * SB_WIDTH].astype(BF16),
        "w_kv_t": w_in[l][:, SB_WIDTH:3 * SB_WIDTH].T.astype(BF16),
        "w_rkv": w_in[l][:, 3 * SB_WIDTH:3 * SB_WIDTH + rw3].astype(BF16),
        "w_lora": _pad_lanes(w_in[l][:, 3 * SB_WIDTH + rw3:], LORA_PAD).astype(BF16),
        "mu_rkv": mu_shift[l][None, :rw3],
        "mu_lora": _pad_lanes(mu_shift[l][None, rw3:], LORA_PAD),
        "w0": w0[l][None, :],
        "wd": lora_rows(w_decay_up[l], 0, LORA_DECAY),
        "a0": a0[l][None, :],
        "wa": lora_rows(w_aaa_up[l], LORA_DECAY, LORA_DECAY + LORA_AAA),
        "wg": lora_rows(w_gate_up[l], LORA_DECAY + LORA_AAA, LORA_WIDTH),
        "k_k": k_k[l][None, :],
        "k_a": k_a[l][None, :],
        "r_k": r_k[l].reshape(1, RW_WIDTH),
        "ln_w": ln_x_w[l][None, :],
        "ln_b": ln_x_b[l][None, :],
        "w_out_sb": w_out[l][:SB_WIDTH].astype(BF16),
        "w_out_rw": w_out[l][SB_WIDTH:].astype(BF16),
        "norm_ffn_g": norm_ffn_g[l][None, :],
        "w_gate": w_gate[l].astype(BF16),
        "w_up": w_up[l].astype(BF16),
        "w_down": w_down[l].astype(BF16),
        "norm_final_g": norm_final_g[None, :],
    }
    bp = x_prompt.shape[0]
    heads = RW_WIDTH // HEAD_DIM
    wkv_zero = jnp.zeros((bp, heads, HEAD_DIM, HEAD_DIM), x_prompt.dtype)
    shift_zero = jnp.zeros((bp, 1, rw3 + LORA_WIDTH), x_prompt.dtype)
    yp, k1, v1, s1, sh1 = _layer(x_prompt, None, None, wkv_zero, shift_zero, w, tm=512, tq=128, tb=2 * CHUNK)
    ys, k2, v2, s2, sh2 = _layer(x_sample, cache_k[l], cache_v[l], state_wkv[l], state_shift[l], w,
                                 tm=256, tq=x_sample.shape[1], tb=CHUNK)
    return (yp, ys, k1[None], v1[None], s1[None], sh1[None], k2[None], v2[None], s2[None], sh2[None])
```

```python
import functools
import math

import jax
import jax.numpy as jnp
from jax import lax
from jax.experimental import pallas as pl
from jax.experimental.pallas import tpu as pltpu

F32 = jnp.float32
BF16 = jnp.bfloat16

LANES = 128
HEAD_DIM = 64
SB_WIDTH = 512
SB_HEADS = SB_WIDTH // HEAD_DIM
RW_WIDTH = 512
LORA_DECAY = 64
LORA_AAA = 64
LORA_GATE = 160
LORA_WIDTH = LORA_DECAY + LORA_AAA + LORA_GATE
LORA_PAD = 384
RMS_EPS = 1e-6
GN_EPS = 64e-5
DECAY_SCALE = math.exp(-0.5)
KEY_BLOCK = 128
SB_DEAD_LOG = -104.0
SB_WINDOW = 3
CHUNK = 64
VMEM_LIMIT = 56 * 1024 * 1024
PROJ_ROWS = 1024
FFN_ROWS = 512


def _dot(a, b):
    return jnp.dot(a, b, preferred_element_type=F32)


def _split2(x):
    hi = x.astype(BF16)
    lo = (x - hi.astype(F32)).astype(BF16)
    return hi, lo


def _bdot(a, b):
    return jnp.einsum("bij,bjk->bik", a, b, preferred_element_type=F32)


def _bdot_nt(a, b):
    return jnp.einsum("bik,bjk->bij", a, b, preferred_element_type=F32)


def _bmm1(a, b):
    return _bdot(a.astype(BF16), b.astype(BF16))


def _bmm3(a, b):
    ah, al = _split2(a)
    bh, bl = _split2(b)
    return _bdot(ah, bh) + (_bdot(ah, bl) + _bdot(al, bh))


def _mm_exact_rhs(x, m, parts=2, transposed=False):
    contract = (((1,), (1 if transposed else 0,)), ((), ()))
    acc = None
    for _ in range(parts):
        part = x.astype(BF16)
        prod = lax.dot_general(part, m, contract, preferred_element_type=F32)
        acc = prod if acc is None else acc + prod
        x = x - part.astype(F32)
    return acc


def _sigmoid(x):
    return 0.5 * jnp.tanh(0.5 * x) + 0.5


def _rmsnorm(x, g):
    return x * lax.rsqrt(jnp.mean(x * x, axis=-1, keepdims=True) + RMS_EPS) * g


def _const_spec(shape):
    return pl.BlockSpec(shape, lambda *_: (0,) * len(shape), pipeline_mode=pl.Buffered(1))


def _norm_proj_kernel(x_ref, g_ref, wq_ref, wkv_ref, wrkv_ref, wl_ref,
                      q_ref, k_ref, v_ref, rkv_ref, l_ref, *, feature_major_kv):
    hb = _rmsnorm(x_ref[...], g_ref[...]).astype(BF16)
    q_ref[...] = _dot(hb, wq_ref[...])
    if feature_major_kv:
        kv = lax.dot_general(wkv_ref[...], hb, (((1,), (1,)), ((), ())), preferred_element_type=F32)
        k_ref[0] = kv[:SB_WIDTH]
        v_ref[0] = kv[SB_WIDTH:]
    else:
        kv = _dot(hb, wkv_ref[...])
        k_ref[...] = kv[:, :SB_WIDTH]
        v_ref[...] = kv[:, SB_WIDTH:]
    rkv_ref[...] = _dot(hb, wrkv_ref[...])
    l_ref[...] = _dot(hb, wl_ref[...])


def _norm_proj(x, g, wq, wkv, wrkv, wl, tm, feature_major_kv):
    b, t, d = x.shape
    m = b * t
    row = lambda w: pl.BlockSpec((tm, w), lambda i: (i, 0))
    if feature_major_kv:
        per_batch = t // tm
        kv_spec = pl.BlockSpec((1, SB_WIDTH, tm), lambda i: (i // per_batch, 0, i % per_batch))
        kv_shape = jax.ShapeDtypeStruct((b, SB_WIDTH, t), F32)
    else:
        kv_spec, kv_shape = row(SB_WIDTH), jax.ShapeDtypeStruct((m, SB_WIDTH), F32)
    return pl.pallas_call(
        functools.partial(_norm_proj_kernel, feature_major_kv=feature_major_kv),
        grid=(m // tm,),
        in_specs=[row(d), _const_spec((1, d)), _const_spec(wq.shape), _const_spec(wkv.shape),
                  _const_spec(wrkv.shape), _const_spec(wl.shape)],
        out_specs=[row(SB_WIDTH), kv_spec, kv_spec, row(3 * RW_WIDTH), row(LORA_PAD)],
        out_shape=[jax.ShapeDtypeStruct((m, SB_WIDTH), F32), kv_shape, kv_shape,
                   jax.ShapeDtypeStruct((m, 3 * RW_WIDTH), F32), jax.ShapeDtypeStruct((m, LORA_PAD), F32)],
        compiler_params=pltpu.CompilerParams(dimension_semantics=("arbitrary",),
                                             vmem_limit_bytes=VMEM_LIMIT),
        name="norm_proj",
    )(x.reshape(m, d), g, wq, wkv, wrkv, wl)


def _sb_kernel(q_ref, *refs, tq, own_in_prefix, n_prefix):
    if own_in_prefix:
        kp_ref, vp_ref, o_ref = refs
    else:
        kd_ref, vd_ref, kp_ref, vp_ref, o_ref = refs
    pairs = SB_WIDTH // LANES
    by_pair = lambda x: jnp.stack([x[:, p * LANES:(p + 1) * LANES] for p in range(pairs)])
    q = by_pair(q_ref[0] * (1.0 / math.sqrt(HEAD_DIM)))
    first = lax.broadcasted_iota(jnp.int32, (pairs, tq, LANES), 2) < HEAD_DIM
    qe = jnp.concatenate([jnp.where(first, q, 0.0), jnp.where(first, 0.0, q)], axis=1).astype(BF16)

    row = lax.broadcasted_iota(jnp.int32, (pairs, 2 * tq, KEY_BLOCK), 1)
    col = lax.broadcasted_iota(jnp.int32, (pairs, 2 * tq, KEY_BLOCK), 2)
    causal = col < jnp.where(row >= tq, row - tq, row)

    jj = lax.broadcasted_iota(jnp.int32, (KEY_BLOCK, 2 * KEY_BLOCK), 0)
    ss = lax.broadcasted_iota(jnp.int32, (KEY_BLOCK, 2 * KEY_BLOCK), 1)
    suffix = jnp.where((jj > ss) | (ss >= KEY_BLOCK), 1.0, 0.0).astype(BF16)
    suffix2 = jnp.concatenate([suffix, suffix], axis=0)

    def prefix(ref, start, n_keys):
        return ref[0, :, pl.ds(start, n_keys)].reshape(pairs, LANES, n_keys)

    def suffix_sums(log1mb):
        hi_lo = jnp.concatenate(_split2(log1mb.reshape(pairs * 2 * tq, KEY_BLOCK)), axis=1)
        return _dot(hi_lo, suffix2).reshape(pairs, 2 * tq, 2 * KEY_BLOCK)

    def softplus(z):
        return jnp.maximum(z, 0.0) + jnp.log(1.0 + jnp.exp(-jnp.abs(z)))

    def own_block(kb, vb, c, o, mask):
        z = _bdot_nt(qe, kb.astype(BF16))
        log1mb = jnp.where(mask, -softplus(z), 0.0)
        sums = suffix_sums(log1mb)
        w = jnp.where(mask, jnp.exp(z + log1mb + (sums[:, :, :KEY_BLOCK] + c)), 0.0)
        return c + sums[:, :, KEY_BLOCK:], o + _bdot(w.astype(BF16), vb.astype(BF16))

    def sweep(kb, vb, c, o, mask=None):
        z = _bdot(qe, kb.astype(BF16))
        log1mb = -softplus(z)
        if mask is not None:
            log1mb = jnp.where(mask, log1mb, 0.0)
        n_blocks = kb.shape[2] // KEY_BLOCK
        after = [None] * n_blocks
        for j in reversed(range(n_blocks)):
            sums = suffix_sums(log1mb[:, :, j * KEY_BLOCK:(j + 1) * KEY_BLOCK])
            after[j] = sums[:, :, :KEY_BLOCK] + c
            c = c + sums[:, :, KEY_BLOCK:]
        w = jnp.exp(z + log1mb + jnp.concatenate(after, axis=2))
        if mask is not None:
            w = jnp.where(mask, w, 0.0)
        return c, o + _bdot_nt(w.astype(BF16), vb.astype(BF16))

    alive = lambda c: (jnp.max(c) >= SB_DEAD_LOG).astype(jnp.int32)
    zeros = jnp.zeros((pairs, 2 * tq, LANES), F32)
    if own_in_prefix:
        i = pl.program_id(1)
        n = jnp.maximum(i - (SB_WINDOW - 1), 0)
        start = pl.multiple_of(n * KEY_BLOCK, KEY_BLOCK)
        wrow = lax.broadcasted_iota(jnp.int32, (pairs, 2 * tq, SB_WINDOW * KEY_BLOCK), 1)
        wcol = lax.broadcasted_iota(jnp.int32, (pairs, 2 * tq, SB_WINDOW * KEY_BLOCK), 2)
        visible = start + wcol < i * tq + jnp.where(wrow >= tq, wrow - tq, wrow)
        c, o = sweep(prefix(kp_ref, start, SB_WINDOW * KEY_BLOCK), prefix(vp_ref, start, SB_WINDOW * KEY_BLOCK),
                     zeros, zeros, visible)
        still = alive(c)
    else:
        c, o = own_block(by_pair(kd_ref[0]), by_pair(vd_ref[0]), zeros, zeros, causal)
        n = n_prefix
        still = jnp.int32(1)

    def live(carry):
        return (carry[0] < (n >> 1)) & (carry[1] > 0)

    def body(carry):
        it, _, c, o = carry
        start = pl.multiple_of((n - 2 - 2 * it) * KEY_BLOCK, KEY_BLOCK)
        c, o = sweep(prefix(kp_ref, start, 2 * KEY_BLOCK), prefix(vp_ref, start, 2 * KEY_BLOCK), c, o)
        return it + 1, alive(c), c, o

    _, still, c, o = lax.while_loop(live, body, (jnp.int32(0), still, c, o))
    c, o = lax.cond(((n & 1) == 1) & (still > 0),
                    lambda c, o: sweep(prefix(kp_ref, 0, KEY_BLOCK), prefix(vp_ref, 0, KEY_BLOCK), c, o),
                    lambda c, o: (c, o), c, o)
    o = jnp.where(first, o[:, :tq], o[:, tq:])
    o_ref[0] = jnp.concatenate([o[p] for p in range(pairs)], axis=1)


def _sb_attention(q, kp, vp, own=None, *, tq):
    b, t, _ = q.shape
    tp = kp.shape[2]
    blk = lambda rows: pl.BlockSpec((1, rows, SB_WIDTH), lambda bi, i: (bi, i, 0))
    pre = pl.BlockSpec((1,) + kp.shape[1:], lambda bi, i: (bi, 0, 0))
    if own is None:
        assert tq == KEY_BLOCK and tp == t and t >= SB_WINDOW * KEY_BLOCK
        in_specs, args = [blk(tq), pre, pre], (q, kp, vp)
    else:
        in_specs, args = [blk(tq), blk(KEY_BLOCK), blk(KEY_BLOCK), pre, pre], (q,) + tuple(own) + (kp, vp)
    return pl.pallas_call(
        functools.partial(_sb_kernel, tq=tq, own_in_prefix=own is None, n_prefix=tp // KEY_BLOCK),
        grid=(b, t // tq),
        in_specs=in_specs,
        out_specs=blk(tq),
        out_shape=jax.ShapeDtypeStruct(q.shape, F32),
        compiler_params=pltpu.CompilerParams(dimension_semantics=("arbitrary",) * 2,
                                             vmem_limit_bytes=VMEM_LIMIT),
        name="stickbreak",
    )(*args)


def _rwkv_kernel(x_ref, l_ref, sx_ref, sl_ref, s0_ref, mux_ref, mul_ref, w0_ref, wd_ref, a0_ref, wa_ref,
                 wg_ref, kkw_ref, kaw_ref, rkw_ref, lnw_ref, lnb_ref,
                 o_ref, sout_ref,
                 px_s, pl_s, y_s, z_s, qe_s, yl_s, phi_s, psi_s,
                 *, nbat, tb, t_valid):
    t = pl.program_id(0)
    pairs = RW_WIDTH // LANES
    rows_all = nbat * tb
    pr = lax.broadcasted_iota(jnp.int32, (HEAD_DIM, LANES), 0)
    pc = lax.broadcasted_iota(jnp.int32, (HEAD_DIM, LANES), 1)
    to_low = jnp.where(pc == pr, 1.0, 0.0).astype(BF16)
    to_high = jnp.where(pc == pr + HEAD_DIM, 1.0, 0.0).astype(BF16)

    @pl.when(t == 0)
    def _():
        px_s[...] = sx_ref[...]
        pl_s[...] = sl_ref[...]
        for p in range(pairs):
            for bi in range(nbat):
                z_s[p * nbat + bi] = jnp.concatenate(
                    [_mm_exact_rhs(s0_ref[bi, 2 * p], to_low, 3),
                     _mm_exact_rhs(s0_ref[bi, 2 * p + 1], to_high, 3)], axis=0).T

    def shift(x_ref, carry_s, mu_ref):
        x = x_ref[...]
        prev = pltpu.roll(x.reshape(rows_all, x.shape[2]), 1, 0).reshape(x.shape)
        prev = jnp.where(lax.broadcasted_iota(jnp.int32, x.shape, 1) == 0, carry_s[...], prev)
        carry_s[...] = x[:, tb - 1:tb, :]
        return (x + (prev - x) * mu_ref[...]).reshape(rows_all, x.shape[2])

    xs = shift(x_ref, px_s, mux_ref)
    xr, xk, xv = xs[:, :RW_WIDTH], xs[:, RW_WIDTH:2 * RW_WIDTH], xs[:, 2 * RW_WIDTH:]
    xl = shift(l_ref, pl_s, mul_ref)

    hr = lax.broadcasted_iota(jnp.int32, (LANES, LANES), 0) // HEAD_DIM
    hc = lax.broadcasted_iota(jnp.int32, (LANES, LANES), 1) // HEAD_DIM
    head_ones = jnp.where(hr == hc, 1.0, 0.0).astype(BF16)
    head_sum = lambda x, parts=2: jnp.concatenate(
        [_mm_exact_rhs(x[:, p * LANES:(p + 1) * LANES], head_ones, parts) for p in range(pairs)], axis=1)

    lora = lambda x, w_ref, lo, hi: _dot(x.astype(BF16), w_ref[lo:hi, :].astype(BF16))
    log_w = -DECAY_SCALE * _sigmoid(w0_ref[...] + lora(jnp.tanh(xl[:, :LANES]), wd_ref, 0, LANES))
    a = _sigmoid(a0_ref[...] + lora(xl[:, :LANES], wa_ref, 0, LANES))
    kk = xk * kkw_ref[...]
    kk = kk * lax.rsqrt(jnp.maximum(head_sum(kk * kk, 1), 1e-24))
    k2 = xk * (1.0 + (a - 1.0) * kaw_ref[...])
    b = kk * a
    if t_valid < tb:
        valid = lax.broadcasted_iota(jnp.int32, (nbat, tb, RW_WIDTH), 1).reshape(rows_all, RW_WIDTH) < t_valid
        log_w = jnp.where(valid, log_w, 0.0)
        kk = jnp.where(valid, kk, 0.0)
        b = jnp.where(valid, b, 0.0)
        k2 = jnp.where(valid, k2, 0.0)
        xv = jnp.where(valid, xv, 0.0)
    nc = tb // CHUNK
    nb = pairs * nbat * nc
    e2 = 2 * CHUNK
    by_chunk = lambda x: jnp.stack(
        [x[:, p * LANES:(p + 1) * LANES] for p in range(pairs)]).reshape(nb, CHUNK, LANES)
    ti = lax.broadcasted_iota(jnp.int32, (nb, CHUNK, CHUNK), 1)
    tj = lax.broadcasted_iota(jnp.int32, (nb, CHUNK, CHUNK), 2)
    tri = jnp.where(tj <= ti, 1.0, 0.0).astype(BF16)
    lw3 = by_chunk(log_w)
    lw_hi, lw_lo = _split2(lw3)
    cum3 = _bdot(tri, lw_hi) + _bdot(tri, lw_lo)
    tot = cum3[:, CHUNK - 1:CHUNK, :]
    first = lax.broadcasted_iota(jnp.int32, (nb, CHUNK, LANES), 2) < HEAD_DIM
    expand = lambda x: jnp.concatenate([jnp.where(first, x, 0.0), jnp.where(first, 0.0, x)], axis=1)
    e_neg = jnp.exp(-cum3)
    e_tail = jnp.exp(tot - cum3)
    kk3, b3, k3, v3 = by_chunk(kk), by_chunk(b), by_chunk(k2), by_chunk(xv)
    kkt = kk3 * jnp.exp(cum3 - lw3)
    rt = by_chunk(xr) * jnp.exp(cum3)

    wr = lax.broadcasted_iota(jnp.int32, (nb, CHUNK, e2), 1)
    wc = lax.broadcasted_iota(jnp.int32, (nb, CHUNK, e2), 2)
    head0 = wc < CHUNK
    ws = jnp.where(head0, wc, wc - CHUNK)
    strict, incl, eye_w = ws < wr, ws <= wr, ws == wr
    diag2 = lambda x: jnp.concatenate([jnp.where(head0, x, 0.0), jnp.where(head0, 0.0, x)], axis=1)

    gram = _bdot_nt(jnp.concatenate([kkt, rt], axis=1).astype(BF16),
                    jnp.concatenate([expand(k3 * e_neg), expand(b3 * e_neg)], axis=1).astype(BF16))
    ak = jnp.where(strict, gram[:, :CHUNK, :e2], 0.0)
    ab = jnp.where(strict, gram[:, :CHUNK, e2:], 0.0)
    rk = jnp.where(incl, gram[:, CHUNK:, :e2], 0.0)
    rb = jnp.where(incl, gram[:, CHUNK:, e2:], 0.0)

    eye2 = jnp.where(eye_w, 1.0, 0.0)
    tinv = eye2 - ab
    ab_d = diag2(ab)
    pw = _bmm1(ab, ab_d)
    n_sq = CHUNK.bit_length() - 2
    for s in range(n_sq):
        if s + 1 < n_sq:
            both = _bmm1(pw, jnp.concatenate([diag2(tinv), diag2(pw)], axis=2))
            tinv = tinv + both[:, :, :e2]
            pw = both[:, :, e2:]
        else:
            tinv = tinv + _bmm1(pw, diag2(tinv))
    resid = eye2 - tinv - _bmm3(ab, diag2(tinv))
    tinv = tinv + _bmm1(tinv, diag2(resid))

    ve = expand(v3)
    akve = _bmm1(ak, ve)
    wu = _bmm1(tinv, jnp.concatenate([expand(kkt), expand(akve)], axis=2))
    w_, u_loc = wu[:, :, :LANES], wu[:, :, LANES:]
    yq = _bmm1(jnp.concatenate([rk, -rb], axis=2),
               jnp.concatenate([jnp.concatenate([ve, jnp.zeros_like(ve)], axis=2),
                                jnp.concatenate([expand(u_loc), expand(w_)], axis=2)], axis=1))
    per_pair = lambda x: x.reshape((pairs * nbat, nc) + x.shape[1:])
    qe_s[...] = per_pair(rt + yq[:, :, LANES:])
    yl_s[...] = per_pair(yq[:, :, :LANES])
    zp = _bmm1(jnp.swapaxes(jnp.concatenate([k3 * e_tail, -(b3 * e_tail)], axis=1), 1, 2),
               jnp.concatenate([jnp.concatenate([v3, jnp.zeros_like(v3)], axis=2),
                                jnp.concatenate([u_loc, w_], axis=2)], axis=1))
    zr = lax.broadcasted_iota(jnp.int32, (nb, HEAD_DIM, LANES), 1)
    zc = lax.broadcasted_iota(jnp.int32, (nb, HEAD_DIM, LANES), 2)
    zhead0 = zc < HEAD_DIM
    own_block = lambda x: jnp.where(zhead0, x[:, :HEAD_DIM], x[:, HEAD_DIM:])
    decay = jnp.where(jnp.where(zhead0, zc, zc - HEAD_DIM) == zr, jnp.exp(tot), 0.0)
    phi_s[...] = per_pair(decay + own_block(zp[:, :, LANES:]))
    psi_s[...] = per_pair(own_block(zp[:, :, :LANES]))
    shead0 = lax.broadcasted_iota(jnp.int32, (pairs * nbat, HEAD_DIM, LANES), 2) < HEAD_DIM

    def advance(c, _):
        rows = pl.ds(pl.multiple_of(c * CHUNK, CHUNK), CHUNK)
        z = z_s[...]
        y_s[:, rows, :] = _bmm1(qe_s[:, c], z) + yl_s[:, c]
        z_new = _bmm3(phi_s[:, c], z) + psi_s[:, c]
        z_s[...] = jnp.concatenate([jnp.where(shead0, z_new, 0.0), jnp.where(shead0, 0.0, z_new)], axis=1)
        return 0

    lax.fori_loop(0, nc, advance, 0, unroll=min(nc, 2))

    y = jnp.concatenate([y_s[p * nbat:(p + 1) * nbat].reshape(rows_all, LANES) for p in range(pairs)], axis=1)
    d = y - head_sum(y) * (1.0 / HEAD_DIM)
    var = head_sum(d * d) * (1.0 / HEAD_DIM)
    o = d * lax.rsqrt(var + GN_EPS) * lnw_ref[...] + lnb_ref[...]
    bonus = head_sum(xr * k2 * rkw_ref[...]) * xv
    gate = lora(_sigmoid(xl[:, LANES:]), wg_ref, LANES, LORA_PAD)
    o_ref[...] = ((o + bonus) * gate).reshape(nbat, tb, RW_WIDTH)

    @pl.when(t == pl.num_programs(0) - 1)
    def _():
        for p in range(pairs):
            for bi in range(nbat):
                s_pair = z_s[p * nbat + bi].T
                sout_ref[bi, 2 * p] = s_pair[:HEAD_DIM, :HEAD_DIM]
                sout_ref[bi, 2 * p + 1] = _mm_exact_rhs(s_pair[HEAD_DIM:], to_high, 3, transposed=True)


def _rwkv(rkv, lora, shift_rkv, shift_lora, z0, prm, *, tb, t_valid):
    b, t, _ = rkv.shape
    nt = t // tb
    nc = tb // CHUNK
    pairs = RW_WIDTH // LANES
    rows = lambda w: pl.BlockSpec((b, tb, w), lambda ti: (0, ti, 0))
    first = lambda w: pl.BlockSpec((b, 1, w), lambda ti: (0, 0, 0))
    state = pl.BlockSpec((b, 2 * pairs, HEAD_DIM, HEAD_DIM), lambda ti: (0, 0, 0, 0))
    vec = _const_spec((1, RW_WIDTH))
    mat = _const_spec((LORA_PAD, RW_WIDTH))
    return pl.pallas_call(
        functools.partial(_rwkv_kernel, nbat=b, tb=tb, t_valid=t_valid),
        grid=(nt,),
        in_specs=[rows(3 * RW_WIDTH), rows(LORA_PAD), first(3 * RW_WIDTH), first(LORA_PAD), state,
                  _const_spec((1, 3 * RW_WIDTH)), _const_spec((1, LORA_PAD)),
                  vec, mat, vec, mat, mat, vec, vec, vec, vec, vec],
        out_specs=[rows(RW_WIDTH), state],
        out_shape=[jax.ShapeDtypeStruct((b, t, RW_WIDTH), F32),
                   jax.ShapeDtypeStruct((b, 2 * pairs, HEAD_DIM, HEAD_DIM), F32)],
        scratch_shapes=[pltpu.VMEM((b, 1, 3 * RW_WIDTH), F32), pltpu.VMEM((b, 1, LORA_PAD), F32),
                        pltpu.VMEM((pairs * b, tb, LANES), F32), pltpu.VMEM((pairs * b, LANES, LANES), F32)]
        + [pltpu.VMEM((pairs * b, nc, CHUNK, LANES), F32)] * 2
        + [pltpu.VMEM((pairs * b, nc, HEAD_DIM, LANES), F32)] * 2,
        compiler_params=pltpu.CompilerParams(dimension_semantics=("arbitrary",),
                                             vmem_limit_bytes=VMEM_LIMIT),
        name="rwkv7",
    )(rkv, lora, shift_rkv, shift_lora, z0, prm["mu_rkv"], prm["mu_lora"],
      prm["w0"], prm["wd"], prm["a0"], prm["wa"], prm["wg"],
      prm["k_k"], prm["k_a"], prm["r_k"], prm["ln_w"], prm["ln_b"])


def _out_ffn_kernel(x_ref, sb_ref, rw_ref, woa_ref, wob_ref, gf_ref, wg_ref, wu_ref, wd_ref, gl_ref,
                    y_ref):
    x1 = x_ref[...] + _dot(sb_ref[...].astype(BF16), woa_ref[...]) \
        + _dot(rw_ref[...].astype(BF16), wob_ref[...])
    h2 = _rmsnorm(x1, gf_ref[...]).astype(BF16)
    gate = _dot(h2, wg_ref[...])
    act = gate * _sigmoid(gate) * _dot(h2, wu_ref[...])
    x2 = x1 + _dot(act.astype(BF16), wd_ref[...])
    y_ref[...] = _rmsnorm(x2, gl_ref[...])


def _out_ffn(x2d, o_sb, o_rw, woa, wob, gf, wg, wu, wd, gl, tm):
    m, d = x2d.shape
    row = lambda w: pl.BlockSpec((tm, w), lambda i: (i, 0))
    return pl.pallas_call(
        _out_ffn_kernel,
        grid=(m // tm,),
        in_specs=[row(d), row(SB_WIDTH), row(RW_WIDTH), _const_spec(woa.shape), _const_spec(wob.shape),
                  _const_spec((1, d)), _const_spec(wg.shape), _const_spec(wu.shape),
                  _const_spec(wd.shape), _const_spec((1, d))],
        out_specs=row(d),
        out_shape=jax.ShapeDtypeStruct((m, d), F32),
        compiler_params=pltpu.CompilerParams(dimension_semantics=("arbitrary",),
                                             vmem_limit_bytes=VMEM_LIMIT),
        name="out_ffn",
    )(x2d, o_sb, o_rw, woa, wob, gf, wg, wu, wd, gl)


def _pad_rows(x, rows):
    return jnp.pad(x, ((0, 0), (0, rows - x.shape[1]), (0, 0)))


def _pad_lanes(x, lanes):
    return jnp.pad(x, [(0, 0)] * (x.ndim - 1) + [(0, lanes - x.shape[-1])])


def _tile_sizes(b, t, feature_major_kv):
    rows = b * t
    return dict(tm_proj=math.gcd(t, PROJ_ROWS) if feature_major_kv else min(rows, PROJ_ROWS),
                tm_ffn=min(rows, FFN_ROWS), tq=min(t, KEY_BLOCK), tb=CHUNK * min(2, -(-t // CHUNK)))


def _layer(x, k_past, v_past, wkv0, shift0, w):
    b, t, d = x.shape
    tiles = _tile_sizes(b, t, feature_major_kv=k_past is None)
    tq, tb = tiles["tq"], tiles["tb"]
    x2d = x.reshape(b * t, d)
    to_feature_major = lambda a: jnp.transpose(a, (0, 2, 3, 1)).reshape(b, SB_WIDTH, a.shape[1])
    from_feature_major = lambda a: jnp.transpose(a.reshape(b, SB_HEADS, HEAD_DIM, a.shape[2]), (0, 3, 1, 2))
    if k_past is None:
        q, k, v, rkv, lora = _norm_proj(x, w["norm_mix_g"], w["w_q"], w["w_kv_t"], w["w_rkv"], w["w_lora"],
                                        tiles["tm_proj"], feature_major_kv=True)
        o_sb = _sb_attention(q.reshape(b, t, SB_WIDTH), k, v, tq=tq)
        k_heads, v_heads = from_feature_major(k), from_feature_major(v)
    else:
        q, k, v, rkv, lora = _norm_proj(x, w["norm_mix_g"], w["w_q"], w["w_kv"], w["w_rkv"], w["w_lora"],
                                        tiles["tm_proj"], feature_major_kv=False)
        q3, k3, v3 = (a.reshape(b, t, SB_WIDTH) for a in (q, k, v))
        o_sb = _sb_attention(q3, to_feature_major(k_past), to_feature_major(v_past),
                             (_pad_rows(k3, KEY_BLOCK), _pad_rows(v3, KEY_BLOCK)), tq=tq)
        k_heads, v_heads = (a.reshape(b, t, SB_HEADS, HEAD_DIM) for a in (k, v))

    rkv3 = rkv.reshape(b, t, 3 * RW_WIDTH)
    lora3 = lora.reshape(b, t, LORA_PAD)
    t_pad = -(-t // tb) * tb
    o_rw, wkv_out = _rwkv(_pad_rows(rkv3, t_pad), _pad_rows(lora3, t_pad),
                          shift0[..., :3 * RW_WIDTH], _pad_lanes(shift0[..., 3 * RW_WIDTH:], LORA_PAD),
                          wkv0, w, tb=tb, t_valid=min(t, tb))
    o_rw = o_rw[:, :t]
    shift_last = jnp.concatenate([rkv3[:, t - 1:, :], lora3[:, t - 1:, :LORA_WIDTH]], axis=-1)

    y = _out_ffn(x2d, o_sb.reshape(b * t, SB_WIDTH), o_rw.reshape(b * t, RW_WIDTH),
                 w["w_out_sb"], w["w_out_rw"], w["norm_ffn_g"], w["w_gate"], w["w_up"], w["w_down"],
                 w["norm_final_g"], tiles["tm_ffn"])
    return y.reshape(b, t, d), k_heads, v_heads, wkv_out, shift_last


def kernel(x_prompt, x_sample, cache_k, cache_v, state_wkv, state_shift, norm_mix_g, w_in, mu_shift, w0,
           w_decay_up, a0, w_aaa_up, w_gate_up, k_k, k_a, r_k, ln_x_w, ln_x_b, w_out, norm_ffn_g, w_gate,
           w_up, w_down, norm_final_g):
    assert w_in.shape[0] == 1, "single-layer trunk"
    l = 0
    rw3 = 3 * RW_WIDTH
    lora_rows = lambda m, lo, hi: jnp.pad(m, ((lo, LORA_PAD - hi), (0, 0)))
    w = {
        "norm_mix_g": norm_mix_g[l][None, :],
        "w_q": w_in[l][:, :SB_WIDTH].astype(BF16),
        "w_kv": w_in[l][:, SB_WIDTH:3 * SB_WIDTH].astype(BF16),
        "w_kv_t": w_in[l][:, SB_WIDTH:3 * SB_WIDTH].T.astype(BF16),
        "w_rkv": w_in[l][:, 3 * SB_WIDTH:3 * SB_WIDTH + rw3].astype(BF16),
        "w_lora": _pad_lanes(w_in[l][:, 3 * SB_WIDTH + rw3:], LORA_PAD).astype(BF16),
        "mu_rkv": mu_shift[l][None, :rw3],
        "mu_lora": _pad_lanes(mu_shift[l][None, rw3:], LORA_PAD),
        "w0": w0[l][None, :],
        "wd": lora_rows(w_decay_up[l], 0, LORA_DECAY),
        "a0": a0[l][None, :],
        "wa": lora_rows(w_aaa_up[l], LORA_DECAY, LORA_DECAY + LORA_AAA),
        "wg": lora_rows(w_gate_up[l], LORA_DECAY + LORA_AAA, LORA_WIDTH),
        "k_k": k_k[l][None, :],
        "k_a": k_a[l][None, :],
        "r_k": r_k[l].reshape(1, RW_WIDTH),
        "ln_w": ln_x_w[l][None, :],
        "ln_b": ln_x_b[l][None, :],
        "w_out_sb": w_out[l][:SB_WIDTH].astype(BF16),
        "w_out_rw": w_out[l][SB_WIDTH:].astype(BF16),
        "norm_ffn_g": norm_ffn_g[l][None, :],
        "w_gate": w_gate[l].astype(BF16),
        "w_up": w_up[l].astype(BF16),
        "w_down": w_down[l].astype(BF16),
        "norm_final_g": norm_final_g[None, :],
    }
    bp = x_prompt.shape[0]
    heads = RW_WIDTH // HEAD_DIM
    wkv_zero = jnp.zeros((bp, heads, HEAD_DIM, HEAD_DIM), x_prompt.dtype)
    shift_zero = jnp.zeros((bp, 1, rw3 + LORA_WIDTH), x_prompt.dtype)
    yp, k1, v1, s1, sh1 = _layer(x_prompt, None, None, wkv_zero, shift_zero, w)
    ys, k2, v2, s2, sh2 = _layer(x_sample, cache_k[l], cache_v[l], state_wkv[l], state_shift[l], w)
    return (yp, ys, k1[None], v1[None], s1[None], sh1[None], k2[None], v2[None], s2[None], sh2[None])
```

```python
import functools
import math

import jax
import jax.numpy as jnp
from jax import lax
from jax.experimental import pallas as pl
from jax.experimental.pallas import tpu as pltpu

F32 = jnp.float32
BF16 = jnp.bfloat16

LANES = 128
HEAD_DIM = 64
SB_WIDTH = 512
SB_HEADS = SB_WIDTH // HEAD_DIM
RW_WIDTH = 512
LORA_DECAY = 64
LORA_AAA = 64
LORA_GATE = 160
LORA_WIDTH = LORA_DECAY + LORA_AAA + LORA_GATE
LORA_PAD = 384
RMS_EPS = 1e-6
GN_EPS = 64e-5
DECAY_SCALE = math.exp(-0.5)
KEY_BLOCK = 128
SB_DEAD_LOG = -104.0
SB_WINDOW = 3
SB_QUERY_BLOCKS = 2
CHUNK = 64
VMEM_LIMIT = 56 * 1024 * 1024
PROJ_ROWS = 1024
FFN_ROWS = 512


def _dot(a, b):
    return jnp.dot(a, b, preferred_element_type=F32)


def _split2(x):
    hi = x.astype(BF16)
    lo = (x - hi.astype(F32)).astype(BF16)
    return hi, lo


def _bdot(a, b):
    return jnp.einsum("bij,bjk->bik", a, b, preferred_element_type=F32)


def _bdot_nt(a, b):
    return jnp.einsum("bik,bjk->bij", a, b, preferred_element_type=F32)


def _bmm1(a, b):
    return _bdot(a.astype(BF16), b.astype(BF16))


def _bmm3(a, b):
    ah, al = _split2(a)
    bh, bl = _split2(b)
    return _bdot(ah, bh) + (_bdot(ah, bl) + _bdot(al, bh))


def _mm_exact_rhs(x, m, parts=2, transposed=False):
    contract = (((1,), (1 if transposed else 0,)), ((), ()))
    acc = None
    for _ in range(parts):
        part = x.astype(BF16)
        prod = lax.dot_general(part, m, contract, preferred_element_type=F32)
        acc = prod if acc is None else acc + prod
        x = x - part.astype(F32)
    return acc


def _sigmoid(x):
    return 0.5 * jnp.tanh(0.5 * x) + 0.5


def _rmsnorm(x, g):
    return x * lax.rsqrt(jnp.mean(x * x, axis=-1, keepdims=True) + RMS_EPS) * g


def _const_spec(shape):
    return pl.BlockSpec(shape, lambda *_: (0,) * len(shape), pipeline_mode=pl.Buffered(1))


def _norm_proj_kernel(x_ref, g_ref, wq_ref, wkv_ref, wrkv_ref, wl_ref,
                      q_ref, k_ref, v_ref, rkv_ref, l_ref, *, feature_major_kv):
    hb = _rmsnorm(x_ref[...], g_ref[...]).astype(BF16)
    q_ref[...] = _dot(hb, wq_ref[...])
    if feature_major_kv:
        kv = lax.dot_general(wkv_ref[...], hb, (((1,), (1,)), ((), ())), preferred_element_type=F32)
        k_ref[0] = kv[:SB_WIDTH]
        v_ref[0] = kv[SB_WIDTH:]
    else:
        kv = _dot(hb, wkv_ref[...])
        k_ref[...] = kv[:, :SB_WIDTH]
        v_ref[...] = kv[:, SB_WIDTH:]
    rkv_ref[...] = _dot(hb, wrkv_ref[...])
    l_ref[...] = _dot(hb, wl_ref[...])


def _norm_proj(x, g, wq, wkv, wrkv, wl, tm, feature_major_kv):
    b, t, d = x.shape
    m = b * t
    row = lambda w: pl.BlockSpec((tm, w), lambda i: (i, 0))
    if feature_major_kv:
        per_batch = t // tm
        kv_spec = pl.BlockSpec((1, SB_WIDTH, tm), lambda i: (i // per_batch, 0, i % per_batch))
        kv_shape = jax.ShapeDtypeStruct((b, SB_WIDTH, t), F32)
    else:
        kv_spec, kv_shape = row(SB_WIDTH), jax.ShapeDtypeStruct((m, SB_WIDTH), F32)
    return pl.pallas_call(
        functools.partial(_norm_proj_kernel, feature_major_kv=feature_major_kv),
        grid=(m // tm,),
        in_specs=[row(d), _const_spec((1, d)), _const_spec(wq.shape), _const_spec(wkv.shape),
                  _const_spec(wrkv.shape), _const_spec(wl.shape)],
        out_specs=[row(SB_WIDTH), kv_spec, kv_spec, row(3 * RW_WIDTH), row(LORA_PAD)],
        out_shape=[jax.ShapeDtypeStruct((m, SB_WIDTH), F32), kv_shape, kv_shape,
                   jax.ShapeDtypeStruct((m, 3 * RW_WIDTH), F32), jax.ShapeDtypeStruct((m, LORA_PAD), F32)],
        compiler_params=pltpu.CompilerParams(dimension_semantics=("arbitrary",),
                                             vmem_limit_bytes=VMEM_LIMIT),
        name="norm_proj",
    )(x.reshape(m, d), g, wq, wkv, wrkv, wl)


def _sb_kernel(q_ref, *refs, tq, q_blocks, own_in_prefix, n_prefix):
    if own_in_prefix:
        kp_ref, vp_ref, o_ref = refs
    else:
        kd_ref, vd_ref, kp_ref, vp_ref, o_ref = refs
    pairs = SB_WIDTH // LANES
    by_pair = lambda x: jnp.stack([x[:, p * LANES:(p + 1) * LANES] for p in range(pairs)])
    q_all = q_ref[0] * (1.0 / math.sqrt(HEAD_DIM))
    q = jnp.concatenate([by_pair(q_all[j * tq:(j + 1) * tq]) for j in range(q_blocks)], axis=0)
    first = lax.broadcasted_iota(jnp.int32, (pairs, tq, LANES), 2) < HEAD_DIM
    first_all = jnp.concatenate([first] * q_blocks, axis=0)
    qe = jnp.concatenate([jnp.where(first_all, q, 0.0), jnp.where(first_all, 0.0, q)], axis=1).astype(BF16)
    entries = lambda x, j: x[j * pairs:(j + 1) * pairs]

    row = lax.broadcasted_iota(jnp.int32, (pairs, 2 * tq, KEY_BLOCK), 1)
    col = lax.broadcasted_iota(jnp.int32, (pairs, 2 * tq, KEY_BLOCK), 2)
    causal = col < jnp.where(row >= tq, row - tq, row)

    jj = lax.broadcasted_iota(jnp.int32, (KEY_BLOCK, 2 * KEY_BLOCK), 0)
    ss = lax.broadcasted_iota(jnp.int32, (KEY_BLOCK, 2 * KEY_BLOCK), 1)
    suffix = jnp.where((jj > ss) | (ss >= KEY_BLOCK), 1.0, 0.0).astype(BF16)
    suffix2 = jnp.concatenate([suffix, suffix], axis=0)

    def prefix(ref, start, n_keys):
        return ref[0, :, pl.ds(start, n_keys)].reshape(pairs, LANES, n_keys)

    def suffix_sums(log1mb):
        n = log1mb.shape[0]
        hi_lo = jnp.concatenate(_split2(log1mb.reshape(n * 2 * tq, KEY_BLOCK)), axis=1)
        return _dot(hi_lo, suffix2).reshape(n, 2 * tq, 2 * KEY_BLOCK)

    def softplus(z):
        return jnp.maximum(z, 0.0) + jnp.log(1.0 + jnp.exp(-jnp.abs(z)))

    def own_block(qe, kb, vb, c, o, mask):
        z = _bdot_nt(qe, kb.astype(BF16))
        log1mb = jnp.where(mask, -softplus(z), 0.0)
        sums = suffix_sums(log1mb)
        w = jnp.where(mask, jnp.exp(z + log1mb + (sums[:, :, :KEY_BLOCK] + c)), 0.0)
        return c + sums[:, :, KEY_BLOCK:], o + _bdot(w.astype(BF16), vb.astype(BF16))

    def sweep(qe, kb, vb, c, o, mask=None):
        z = _bdot(qe, kb.astype(BF16))
        log1mb = -softplus(z)
        if mask is not None:
            log1mb = jnp.where(mask, log1mb, 0.0)
        n_blocks = kb.shape[2] // KEY_BLOCK
        after = [None] * n_blocks
        for j in reversed(range(n_blocks)):
            sums = suffix_sums(log1mb[:, :, j * KEY_BLOCK:(j + 1) * KEY_BLOCK])
            after[j] = sums[:, :, :KEY_BLOCK] + c
            c = c + sums[:, :, KEY_BLOCK:]
        w = jnp.exp(z + log1mb + jnp.concatenate(after, axis=2))
        if mask is not None:
            w = jnp.where(mask, w, 0.0)
        return c, o + _bdot_nt(w.astype(BF16), vb.astype(BF16))

    alive = lambda c: (jnp.max(c) >= SB_DEAD_LOG).astype(jnp.int32)
    zeros = jnp.zeros((q_blocks * pairs, 2 * tq, LANES), F32)
    if own_in_prefix:
        wrow = lax.broadcasted_iota(jnp.int32, (pairs, 2 * tq, SB_WINDOW * KEY_BLOCK), 1)
        wcol = lax.broadcasted_iota(jnp.int32, (pairs, 2 * tq, SB_WINDOW * KEY_BLOCK), 2)
        wrow = jnp.where(wrow >= tq, wrow - tq, wrow)
        older, keys, values, visible = [], [], [], []
        for j in range(q_blocks):
            i = pl.program_id(1) * q_blocks + j
            n = jnp.maximum(i - (SB_WINDOW - 1), 0)
            start = pl.multiple_of(n * KEY_BLOCK, KEY_BLOCK)
            older.append(n)
            keys.append(prefix(kp_ref, start, SB_WINDOW * KEY_BLOCK))
            values.append(prefix(vp_ref, start, SB_WINDOW * KEY_BLOCK))
            visible.append(start + wcol < i * tq + wrow)
        c, o = sweep(qe, jnp.concatenate(keys, axis=0), jnp.concatenate(values, axis=0), zeros, zeros,
                     jnp.concatenate(visible, axis=0))
        still = [alive(entries(c, j)) for j in range(q_blocks)]
    else:
        c, o = own_block(qe, by_pair(kd_ref[0]), by_pair(vd_ref[0]), zeros, zeros, causal)
        older = [n_prefix]
        still = [jnp.int32(1)]

    outs = []
    for j in range(q_blocks):
        n, qe_j = older[j], entries(qe, j)

        def live(carry, n=n):
            return (carry[0] < (n >> 1)) & (carry[1] > 0)

        def body(carry, n=n, qe_j=qe_j):
            it, _, c, o = carry
            start = pl.multiple_of((n - 2 - 2 * it) * KEY_BLOCK, KEY_BLOCK)
            c, o = sweep(qe_j, prefix(kp_ref, start, 2 * KEY_BLOCK), prefix(vp_ref, start, 2 * KEY_BLOCK), c, o)
            return it + 1, alive(c), c, o

        _, still_j, c_j, o_j = lax.while_loop(live, body, (jnp.int32(0), still[j], entries(c, j), entries(o, j)))
        _, o_j = lax.cond(((n & 1) == 1) & (still_j > 0),
                          lambda c, o, qe_j=qe_j: sweep(qe_j, prefix(kp_ref, 0, KEY_BLOCK),
                                                        prefix(vp_ref, 0, KEY_BLOCK), c, o),
                          lambda c, o: (c, o), c_j, o_j)
        o_j = jnp.where(first, o_j[:, :tq], o_j[:, tq:])
        outs.append(jnp.concatenate([o_j[p] for p in range(pairs)], axis=1))
    o_ref[0] = jnp.concatenate(outs, axis=0)


def _sb_attention(q, kp, vp, own=None, *, tq):
    b, t, _ = q.shape
    tp = kp.shape[2]
    blk = lambda rows: pl.BlockSpec((1, rows, SB_WIDTH), lambda bi, i: (bi, i, 0))
    pre = pl.BlockSpec((1,) + kp.shape[1:], lambda bi, i: (bi, 0, 0))
    if own is None:
        assert tq == KEY_BLOCK and tp == t and t >= SB_WINDOW * KEY_BLOCK
        q_blocks = SB_QUERY_BLOCKS if t % (SB_QUERY_BLOCKS * tq) == 0 else 1
        in_specs, args = [blk(q_blocks * tq), pre, pre], (q, kp, vp)
    else:
        q_blocks = 1
        in_specs, args = [blk(tq), blk(KEY_BLOCK), blk(KEY_BLOCK), pre, pre], (q,) + tuple(own) + (kp, vp)
    return pl.pallas_call(
        functools.partial(_sb_kernel, tq=tq, q_blocks=q_blocks, own_in_prefix=own is None,
                          n_prefix=tp // KEY_BLOCK),
        grid=(b, t // (q_blocks * tq)),
        in_specs=in_specs,
        out_specs=blk(q_blocks * tq),
        out_shape=jax.ShapeDtypeStruct(q.shape, F32),
        compiler_params=pltpu.CompilerParams(dimension_semantics=("arbitrary",) * 2,
                                             vmem_limit_bytes=VMEM_LIMIT),
        name="stickbreak",
    )(*args)


def _rwkv_kernel(x_ref, l_ref, sx_ref, sl_ref, s0_ref, mux_ref, mul_ref, w0_ref, wd_ref, a0_ref, wa_ref,
                 wg_ref, kkw_ref, kaw_ref, rkw_ref, lnw_ref, lnb_ref,
                 o_ref, sout_ref,
                 px_s, pl_s, y_s, z_s, qe_s, yl_s, phi_s, psi_s,
                 *, nbat, tb, t_valid):
    t = pl.program_id(0)
    pairs = RW_WIDTH // LANES
    rows_all = nbat * tb
    pr = lax.broadcasted_iota(jnp.int32, (HEAD_DIM, LANES), 0)
    pc = lax.broadcasted_iota(jnp.int32, (HEAD_DIM, LANES), 1)
    to_low = jnp.where(pc == pr, 1.0, 0.0).astype(BF16)
    to_high = jnp.where(pc == pr + HEAD_DIM, 1.0, 0.0).astype(BF16)

    @pl.when(t == 0)
    def _():
        px_s[...] = sx_ref[...]
        pl_s[...] = sl_ref[...]
        for p in range(pairs):
            for bi in range(nbat):
                z_s[p * nbat + bi] = jnp.concatenate(
                    [_mm_exact_rhs(s0_ref[bi, 2 * p], to_low, 3),
                     _mm_exact_rhs(s0_ref[bi, 2 * p + 1], to_high, 3)], axis=0).T

    def shift(x_ref, carry_s, mu_ref):
        x = x_ref[...]
        prev = pltpu.roll(x.reshape(rows_all, x.shape[2]), 1, 0).reshape(x.shape)
        prev = jnp.where(lax.broadcasted_iota(jnp.int32, x.shape, 1) == 0, carry_s[...], prev)
        carry_s[...] = x[:, tb - 1:tb, :]
        return (x + (prev - x) * mu_ref[...]).reshape(rows_all, x.shape[2])

    xs = shift(x_ref, px_s, mux_ref)
    xr, xk, xv = xs[:, :RW_WIDTH], xs[:, RW_WIDTH:2 * RW_WIDTH], xs[:, 2 * RW_WIDTH:]
    xl = shift(l_ref, pl_s, mul_ref)

    hr = lax.broadcasted_iota(jnp.int32, (LANES, LANES), 0) // HEAD_DIM
    hc = lax.broadcasted_iota(jnp.int32, (LANES, LANES), 1) // HEAD_DIM
    head_ones = jnp.where(hr == hc, 1.0, 0.0).astype(BF16)
    head_sum = lambda x, parts=2: jnp.concatenate(
        [_mm_exact_rhs(x[:, p * LANES:(p + 1) * LANES], head_ones, parts) for p in range(pairs)], axis=1)

    lora = lambda x, w_ref, lo, hi: _dot(x.astype(BF16), w_ref[lo:hi, :].astype(BF16))
    log_w = -DECAY_SCALE * _sigmoid(w0_ref[...] + lora(jnp.tanh(xl[:, :LANES]), wd_ref, 0, LANES))
    a = _sigmoid(a0_ref[...] + lora(xl[:, :LANES], wa_ref, 0, LANES))
    kk = xk * kkw_ref[...]
    kk = kk * lax.rsqrt(jnp.maximum(head_sum(kk * kk, 1), 1e-24))
    k2 = xk * (1.0 + (a - 1.0) * kaw_ref[...])
    b = kk * a
    if t_valid < tb:
        valid = lax.broadcasted_iota(jnp.int32, (nbat, tb, RW_WIDTH), 1).reshape(rows_all, RW_WIDTH) < t_valid
        log_w = jnp.where(valid, log_w, 0.0)
        kk = jnp.where(valid, kk, 0.0)
        b = jnp.where(valid, b, 0.0)
        k2 = jnp.where(valid, k2, 0.0)
        xv = jnp.where(valid, xv, 0.0)
    nc = tb // CHUNK
    nb = pairs * nbat * nc
    e2 = 2 * CHUNK
    by_chunk = lambda x: jnp.stack(
        [x[:, p * LANES:(p + 1) * LANES] for p in range(pairs)]).reshape(nb, CHUNK, LANES)
    ti = lax.broadcasted_iota(jnp.int32, (nb, CHUNK, CHUNK), 1)
    tj = lax.broadcasted_iota(jnp.int32, (nb, CHUNK, CHUNK), 2)
    tri = jnp.where(tj <= ti, 1.0, 0.0).astype(BF16)
    lw3 = by_chunk(log_w)
    lw_hi, lw_lo = _split2(lw3)
    cum3 = _bdot(tri, lw_hi) + _bdot(tri, lw_lo)
    tot = cum3[:, CHUNK - 1:CHUNK, :]
    first = lax.broadcasted_iota(jnp.int32, (nb, CHUNK, LANES), 2) < HEAD_DIM
    expand = lambda x: jnp.concatenate([jnp.where(first, x, 0.0), jnp.where(first, 0.0, x)], axis=1)
    e_neg = jnp.exp(-cum3)
    e_tail = jnp.exp(tot - cum3)
    kk3, b3, k3, v3 = by_chunk(kk), by_chunk(b), by_chunk(k2), by_chunk(xv)
    kkt = kk3 * jnp.exp(cum3 - lw3)
    rt = by_chunk(xr) * jnp.exp(cum3)

    wr = lax.broadcasted_iota(jnp.int32, (nb, CHUNK, e2), 1)
    wc = lax.broadcasted_iota(jnp.int32, (nb, CHUNK, e2), 2)
    head0 = wc < CHUNK
    ws = jnp.where(head0, wc, wc - CHUNK)
    strict, incl, eye_w = ws < wr, ws <= wr, ws == wr
    diag2 = lambda x: jnp.concatenate([jnp.where(head0, x, 0.0), jnp.where(head0, 0.0, x)], axis=1)

    gram = _bdot_nt(jnp.concatenate([kkt, rt], axis=1).astype(BF16),
                    jnp.concatenate([expand(k3 * e_neg), expand(b3 * e_neg)], axis=1).astype(BF16))
    ak = jnp.where(strict, gram[:, :CHUNK, :e2], 0.0)
    ab = jnp.where(strict, gram[:, :CHUNK, e2:], 0.0)
    rk = jnp.where(incl, gram[:, CHUNK:, :e2], 0.0)
    rb = jnp.where(incl, gram[:, CHUNK:, e2:], 0.0)

    eye2 = jnp.where(eye_w, 1.0, 0.0)
    tinv = eye2 - ab
    ab_d = diag2(ab)
    pw = _bmm1(ab, ab_d)
    n_sq = CHUNK.bit_length() - 2
    for s in range(n_sq):
        if s + 1 < n_sq:
            both = _bmm1(pw, jnp.concatenate([diag2(tinv), diag2(pw)], axis=2))
            tinv = tinv + both[:, :, :e2]
            pw = both[:, :, e2:]
        else:
            tinv = tinv + _bmm1(pw, diag2(tinv))
    resid = eye2 - tinv - _bmm3(ab, diag2(tinv))
    tinv = tinv + _bmm1(tinv, diag2(resid))

    ve = expand(v3)
    akve = _bmm1(ak, ve)
    wu = _bmm1(tinv, jnp.concatenate([expand(kkt), expand(akve)], axis=2))
    w_, u_loc = wu[:, :, :LANES], wu[:, :, LANES:]
    yq = _bmm1(jnp.concatenate([rk, -rb], axis=2),
               jnp.concatenate([jnp.concatenate([ve, jnp.zeros_like(ve)], axis=2),
                                jnp.concatenate([expand(u_loc), expand(w_)], axis=2)], axis=1))
    per_pair = lambda x: x.reshape((pairs * nbat, nc) + x.shape[1:])
    qe_s[...] = per_pair(rt + yq[:, :, LANES:])
    yl_s[...] = per_pair(yq[:, :, :LANES])
    zp = _bmm1(jnp.swapaxes(jnp.concatenate([k3 * e_tail, -(b3 * e_tail)], axis=1), 1, 2),
               jnp.concatenate([jnp.concatenate([v3, jnp.zeros_like(v3)], axis=2),
                                jnp.concatenate([u_loc, w_], axis=2)], axis=1))
    zr = lax.broadcasted_iota(jnp.int32, (nb, HEAD_DIM, LANES), 1)
    zc = lax.broadcasted_iota(jnp.int32, (nb, HEAD_DIM, LANES), 2)
    zhead0 = zc < HEAD_DIM
    own_block = lambda x: jnp.where(zhead0, x[:, :HEAD_DIM], x[:, HEAD_DIM:])
    decay = jnp.where(jnp.where(zhead0, zc, zc - HEAD_DIM) == zr, jnp.exp(tot), 0.0)
    phi_s[...] = per_pair(decay + own_block(zp[:, :, LANES:]))
    psi_s[...] = per_pair(own_block(zp[:, :, :LANES]))
    shead0 = lax.broadcasted_iota(jnp.int32, (pairs * nbat, HEAD_DIM, LANES), 2) < HEAD_DIM

    def advance(c, _):
        rows = pl.ds(pl.multiple_of(c * CHUNK, CHUNK), CHUNK)
        z = z_s[...]
        y_s[:, rows, :] = _bmm1(qe_s[:, c], z) + yl_s[:, c]
        z_new = _bmm3(phi_s[:, c], z) + psi_s[:, c]
        z_s[...] = jnp.concatenate([jnp.where(shead0, z_new, 0.0), jnp.where(shead0, 0.0, z_new)], axis=1)
        return 0

    lax.fori_loop(0, nc, advance, 0, unroll=min(nc, 2))

    y = jnp.concatenate([y_s[p * nbat:(p + 1) * nbat].reshape(rows_all, LANES) for p in range(pairs)], axis=1)
    d = y - head_sum(y) * (1.0 / HEAD_DIM)
    var = head_sum(d * d) * (1.0 / HEAD_DIM)
    o = d * lax.rsqrt(var + GN_EPS) * lnw_ref[...] + lnb_ref[...]
    bonus = head_sum(xr * k2 * rkw_ref[...]) * xv
    gate = lora(_sigmoid(xl[:, LANES:]), wg_ref, LANES, LORA_PAD)
    o_ref[...] = ((o + bonus) * gate).reshape(nbat, tb, RW_WIDTH)

    @pl.when(t == pl.num_programs(0) - 1)
    def _():
        for p in range(pairs):
            for bi in range(nbat):
                s_pair = z_s[p * nbat + bi].T
                sout_ref[bi, 2 * p] = s_pair[:HEAD_DIM, :HEAD_DIM]
                sout_ref[bi, 2 * p + 1] = _mm_exact_rhs(s_pair[HEAD_DIM:], to_high, 3, transposed=True)


def _rwkv(rkv, lora, shift_rkv, shift_lora, z0, prm, *, tb, t_valid):
    b, t, _ = rkv.shape
    nt = t // tb
    nc = tb // CHUNK
    pairs = RW_WIDTH // LANES
    rows = lambda w: pl.BlockSpec((b, tb, w), lambda ti: (0, ti, 0))
    first = lambda w: pl.BlockSpec((b, 1, w), lambda ti: (0, 0, 0))
    state = pl.BlockSpec((b, 2 * pairs, HEAD_DIM, HEAD_DIM), lambda ti: (0, 0, 0, 0))
    vec = _const_spec((1, RW_WIDTH))
    mat = _const_spec((LORA_PAD, RW_WIDTH))
    return pl.pallas_call(
        functools.partial(_rwkv_kernel, nbat=b, tb=tb, t_valid=t_valid),
        grid=(nt,),
        in_specs=[rows(3 * RW_WIDTH), rows(LORA_PAD), first(3 * RW_WIDTH), first(LORA_PAD), state,
                  _const_spec((1, 3 * RW_WIDTH)), _const_spec((1, LORA_PAD)),
                  vec, mat, vec, mat, mat, vec, vec, vec, vec, vec],
        out_specs=[rows(RW_WIDTH), state],
        out_shape=[jax.ShapeDtypeStruct((b, t, RW_WIDTH), F32),
                   jax.ShapeDtypeStruct((b, 2 * pairs, HEAD_DIM, HEAD_DIM), F32)],
        scratch_shapes=[pltpu.VMEM((b, 1, 3 * RW_WIDTH), F32), pltpu.VMEM((b, 1, LORA_PAD), F32),
                        pltpu.VMEM((pairs * b, tb, LANES), F32), pltpu.VMEM((pairs * b, LANES, LANES), F32)]
        + [pltpu.VMEM((pairs * b, nc, CHUNK, LANES), F32)] * 2
        + [pltpu.VMEM((pairs * b, nc, HEAD_DIM, LANES), F32)] * 2,
        compiler_params=pltpu.CompilerParams(dimension_semantics=("arbitrary",),
                                             vmem_limit_bytes=VMEM_LIMIT),
        name="rwkv7",
    )(rkv, lora, shift_rkv, shift_lora, z0, prm["mu_rkv"], prm["mu_lora"],
      prm["w0"], prm["wd"], prm["a0"], prm["wa"], prm["wg"],
      prm["k_k"], prm["k_a"], prm["r_k"], prm["ln_w"], prm["ln_b"])


def _out_ffn_kernel(x_ref, sb_ref, rw_ref, woa_ref, wob_ref, gf_ref, wg_ref, wu_ref, wd_ref, gl_ref,
                    y_ref):
    x1 = x_ref[...] + _dot(sb_ref[...].astype(BF16), woa_ref[...]) \
        + _dot(rw_ref[...].astype(BF16), wob_ref[...])
    h2 = _rmsnorm(x1, gf_ref[...]).astype(BF16)
    gate = _dot(h2, wg_ref[...])
    act = gate * _sigmoid(gate) * _dot(h2, wu_ref[...])
    x2 = x1 + _dot(act.astype(BF16), wd_ref[...])
    y_ref[...] = _rmsnorm(x2, gl_ref[...])


def _out_ffn(x2d, o_sb, o_rw, woa, wob, gf, wg, wu, wd, gl, tm):
    m, d = x2d.shape
    row = lambda w: pl.BlockSpec((tm, w), lambda i: (i, 0))
    return pl.pallas_call(
        _out_ffn_kernel,
        grid=(m // tm,),
        in_specs=[row(d), row(SB_WIDTH), row(RW_WIDTH), _const_spec(woa.shape), _const_spec(wob.shape),
                  _const_spec((1, d)), _const_spec(wg.shape), _const_spec(wu.shape),
                  _const_spec(wd.shape), _const_spec((1, d))],
        out_specs=row(d),
        out_shape=jax.ShapeDtypeStruct((m, d), F32),
        compiler_params=pltpu.CompilerParams(dimension_semantics=("arbitrary",),
                                             vmem_limit_bytes=VMEM_LIMIT),
        name="out_ffn",
    )(x2d, o_sb, o_rw, woa, wob, gf, wg, wu, wd, gl)


def _pad_rows(x, rows):
    return jnp.pad(x, ((0, 0), (0, rows - x.shape[1]), (0, 0)))


def _pad_lanes(x, lanes):
    return jnp.pad(x, [(0, 0)] * (x.ndim - 1) + [(0, lanes - x.shape[-1])])


def _tile_sizes(b, t, feature_major_kv):
    rows = b * t
    return dict(tm_proj=math.gcd(t, PROJ_ROWS) if feature_major_kv else min(rows, PROJ_ROWS),
                tm_ffn=min(rows, FFN_ROWS), tq=min(t, KEY_BLOCK), tb=CHUNK * min(2, -(-t // CHUNK)))


def _layer(x, k_past, v_past, wkv0, shift0, w):
    b, t, d = x.shape
    tiles = _tile_sizes(b, t, feature_major_kv=k_past is None)
    tq, tb = tiles["tq"], tiles["tb"]
    x2d = x.reshape(b * t, d)
    to_feature_major = lambda a: jnp.transpose(a, (0, 2, 3, 1)).reshape(b, SB_WIDTH, a.shape[1])
    from_feature_major = lambda a: jnp.transpose(a.reshape(b, SB_HEADS, HEAD_DIM, a.shape[2]), (0, 3, 1, 2))
    if k_past is None:
        q, k, v, rkv, lora = _norm_proj(x, w["norm_mix_g"], w["w_q"], w["w_kv_t"], w["w_rkv"], w["w_lora"],
                                        tiles["tm_proj"], feature_major_kv=True)
        o_sb = _sb_attention(q.reshape(b, t, SB_WIDTH), k, v, tq=tq)
        k_heads, v_heads = from_feature_major(k), from_feature_major(v)
    else:
        q, k, v, rkv, lora = _norm_proj(x, w["norm_mix_g"], w["w_q"], w["w_kv"], w["w_rkv"], w["w_lora"],
                                        tiles["tm_proj"], feature_major_kv=False)
        q3, k3, v3 = (a.reshape(b, t, SB_WIDTH) for a in (q, k, v))
        o_sb = _sb_attention(q3, to_feature_major(k_past), to_feature_major(v_past),
                             (_pad_rows(k3, KEY_BLOCK), _pad_rows(v3, KEY_BLOCK)), tq=tq)
        k_heads, v_heads = (a.reshape(b, t, SB_HEADS, HEAD_DIM) for a in (k, v))

    rkv3 = rkv.reshape(b, t, 3 * RW_WIDTH)
    lora3 = lora.reshape(b, t, LORA_PAD)
    t_pad = -(-t // tb) * tb
    o_rw, wkv_out = _rwkv(_pad_rows(rkv3, t_pad), _pad_rows(lora3, t_pad),
                          shift0[..., :3 * RW_WIDTH], _pad_lanes(shift0[..., 3 * RW_WIDTH:], LORA_PAD),
                          wkv0, w, tb=tb, t_valid=min(t, tb))
    o_rw = o_rw[:, :t]
    shift_last = jnp.concatenate([rkv3[:, t - 1:, :], lora3[:, t - 1:, :LORA_WIDTH]], axis=-1)

    y = _out_ffn(x2d, o_sb.reshape(b * t, SB_WIDTH), o_rw.reshape(b * t, RW_WIDTH),
                 w["w_out_sb"], w["w_out_rw"], w["norm_ffn_g"], w["w_gate"], w["w_up"], w["w_down"],
                 w["norm_final_g"], tiles["tm_ffn"])
    return y.reshape(b, t, d), k_heads, v_heads, wkv_out, shift_last


def kernel(x_prompt, x_sample, cache_k, cache_v, state_wkv, state_shift, norm_mix_g, w_in, mu_shift, w0,
           w_decay_up, a0, w_aaa_up, w_gate_up, k_k, k_a, r_k, ln_x_w, ln_x_b, w_out, norm_ffn_g, w_gate,
           w_up, w_down, norm_final_g):
    assert w_in.shape[0] == 1, "single-layer trunk"
    l = 0
    rw3 = 3 * RW_WIDTH
    lora_rows = lambda m, lo, hi: jnp.pad(m, ((lo, LORA_PAD - hi), (0, 0)))
    w = {
        "norm_mix_g": norm_mix_g[l][None, :],
        "w_q": w_in[l][:, :SB_WIDTH].astype(BF16),
        "w_kv": w_in[l][:, SB_WIDTH:3 * SB_WIDTH].astype(BF16),
        "w_kv_t": w_in[l][:, SB_WIDTH:3 * SB_WIDTH].T.astype(BF16),
        "w_rkv": w_in[l][:, 3 * SB_WIDTH:3 * SB_WIDTH + rw3].astype(BF16),
        "w_lora": _pad_lanes(w_in[l][:, 3 * SB_WIDTH + rw3:], LORA_PAD).astype(BF16),
        "mu_rkv": mu_shift[l][None, :rw3],
        "mu_lora": _pad_lanes(mu_shift[l][None, rw3:], LORA_PAD),
        "w0": w0[l][None, :],
        "wd": lora_rows(w_decay_up[l], 0, LORA_DECAY),
        "a0": a0[l][None, :],
        "wa": lora_rows(w_aaa_up[l], LORA_DECAY, LORA_DECAY + LORA_AAA),
        "wg": lora_rows(w_gate_up[l], LORA_DECAY + LORA_AAA, LORA_WIDTH),
        "k_k": k_k[l][None, :],
        "k_a": k_a[l][None, :],
        "r_k": r_k[l].reshape(1, RW_WIDTH),
        "ln_w": ln_x_w[l][None, :],
        "ln_b": ln_x_b[l][None, :],
        "w_out_sb": w_out[l][:SB_WIDTH].astype(BF16),
        "w_out_rw": w_out[l][SB_WIDTH:].astype(BF16),
        "norm_ffn_g": norm_ffn_g[l][None, :],
        "w_gate": w_gate[l].astype(BF16),
        "w_up": w_up[l].astype(BF16),
        "w_down": w_down[l].astype(BF16),
        "norm_final_g": norm_final_g[None, :],
    }
    bp = x_prompt.shape[0]
    heads = RW_WIDTH // HEAD_DIM
    wkv_zero = jnp.zeros((bp, heads, HEAD_DIM, HEAD_DIM), x_prompt.dtype)
    shift_zero = jnp.zeros((bp, 1, rw3 + LORA_WIDTH), x_prompt.dtype)
    yp, k1, v1, s1, sh1 = _layer(x_prompt, None, None, wkv_zero, shift_zero, w)
    ys, k2, v2, s2, sh2 = _layer(x_sample, cache_k[l], cache_v[l], state_wkv[l], state_shift[l], w)
    return (yp, ys, k1[None], v1[None], s1[None], sh1[None], k2[None], v2[None], s2[None], sh2[None])
```

```python
import functools
import math

import jax
import jax.numpy as jnp
from jax import lax
from jax.experimental import pallas as pl
from jax.experimental.pallas import tpu as pltpu

F32 = jnp.float32
BF16 = jnp.bfloat16

LANES = 128
HEAD_DIM = 64
SB_WIDTH = 512
SB_HEADS = SB_WIDTH // HEAD_DIM
RW_WIDTH = 512
LORA_DECAY = 64
LORA_AAA = 64
LORA_GATE = 160
LORA_WIDTH = LORA_DECAY + LORA_AAA + LORA_GATE
LORA_PAD = 384
RMS_EPS = 1e-6
GN_EPS = 64e-5
DECAY_SCALE = math.exp(-0.5)
KEY_BLOCK = 128
SB_DEAD_LOG = -104.0
SB_WINDOW = 3
SB_QUERY_BLOCKS = 4
CHUNK = 64
INVERSE_RESID_MAX = 0.02
VMEM_LIMIT = 56 * 1024 * 1024
PROJ_ROWS = 1024
FFN_ROWS = 512


def _dot(a, b):
    return jnp.dot(a, b, preferred_element_type=F32)


def _split2(x):
    hi = x.astype(BF16)
    lo = (x - hi.astype(F32)).astype(BF16)
    return hi, lo


def _bdot(a, b):
    return jnp.einsum("bij,bjk->bik", a, b, preferred_element_type=F32)


def _bdot_nt(a, b):
    return jnp.einsum("bik,bjk->bij", a, b, preferred_element_type=F32)


def _bmm1(a, b):
    return _bdot(a.astype(BF16), b.astype(BF16))


def _bmm3(a, b):
    ah, al = _split2(a)
    bh, bl = _split2(b)
    return _bdot(ah, bh) + (_bdot(ah, bl) + _bdot(al, bh))


def _mm_exact_rhs(x, m, parts=2, transposed=False):
    contract = (((1,), (1 if transposed else 0,)), ((), ()))
    acc = None
    for _ in range(parts):
        part = x.astype(BF16)
        prod = lax.dot_general(part, m, contract, preferred_element_type=F32)
        acc = prod if acc is None else acc + prod
        x = x - part.astype(F32)
    return acc


def _sigmoid(x):
    return 0.5 * jnp.tanh(0.5 * x) + 0.5


def _rmsnorm(x, g):
    return x * lax.rsqrt(jnp.mean(x * x, axis=-1, keepdims=True) + RMS_EPS) * g


def _const_spec(shape):
    return pl.BlockSpec(shape, lambda *_: (0,) * len(shape), pipeline_mode=pl.Buffered(1))


def _norm_proj_kernel(x_ref, g_ref, wq_ref, wkv_ref, wrkv_ref, wl_ref,
                      q_ref, k_ref, v_ref, rkv_ref, l_ref, *, feature_major_kv):
    hb = _rmsnorm(x_ref[...], g_ref[...]).astype(BF16)
    q_ref[...] = _dot(hb, wq_ref[...])
    if feature_major_kv:
        kv = lax.dot_general(wkv_ref[...], hb, (((1,), (1,)), ((), ())), preferred_element_type=F32)
        k_ref[0] = kv[:SB_WIDTH]
        v_ref[0] = kv[SB_WIDTH:]
    else:
        kv = _dot(hb, wkv_ref[...])
        k_ref[...] = kv[:, :SB_WIDTH]
        v_ref[...] = kv[:, SB_WIDTH:]
    rkv_ref[...] = _dot(hb, wrkv_ref[...])
    l_ref[...] = _dot(hb, wl_ref[...])


def _norm_proj(x, g, wq, wkv, wrkv, wl, tm, feature_major_kv):
    b, t, d = x.shape
    m = b * t
    row = lambda w: pl.BlockSpec((tm, w), lambda i: (i, 0))
    if feature_major_kv:
        per_batch = t // tm
        kv_spec = pl.BlockSpec((1, SB_WIDTH, tm), lambda i: (i // per_batch, 0, i % per_batch))
        kv_shape = jax.ShapeDtypeStruct((b, SB_WIDTH, t), F32)
    else:
        kv_spec, kv_shape = row(SB_WIDTH), jax.ShapeDtypeStruct((m, SB_WIDTH), F32)
    return pl.pallas_call(
        functools.partial(_norm_proj_kernel, feature_major_kv=feature_major_kv),
        grid=(m // tm,),
        in_specs=[row(d), _const_spec((1, d)), _const_spec(wq.shape), _const_spec(wkv.shape),
                  _const_spec(wrkv.shape), _const_spec(wl.shape)],
        out_specs=[row(SB_WIDTH), kv_spec, kv_spec, row(3 * RW_WIDTH), row(LORA_PAD)],
        out_shape=[jax.ShapeDtypeStruct((m, SB_WIDTH), F32), kv_shape, kv_shape,
                   jax.ShapeDtypeStruct((m, 3 * RW_WIDTH), F32), jax.ShapeDtypeStruct((m, LORA_PAD), F32)],
        compiler_params=pltpu.CompilerParams(dimension_semantics=("arbitrary",),
                                             vmem_limit_bytes=VMEM_LIMIT),
        name="norm_proj",
    )(x.reshape(m, d), g, wq, wkv, wrkv, wl)


def _sb_kernel(q_ref, *refs, tq, q_blocks, own_in_prefix, n_prefix):
    if own_in_prefix:
        kp_ref, vp_ref, o_ref = refs
    else:
        kd_ref, vd_ref, kp_ref, vp_ref, o_ref = refs
    pairs = SB_WIDTH // LANES
    by_pair = lambda x: jnp.stack([x[:, p * LANES:(p + 1) * LANES] for p in range(pairs)])
    q_all = q_ref[0] * (1.0 / math.sqrt(HEAD_DIM))
    q = jnp.concatenate([by_pair(q_all[j * tq:(j + 1) * tq]) for j in range(q_blocks)], axis=0)
    first = lax.broadcasted_iota(jnp.int32, (pairs, tq, LANES), 2) < HEAD_DIM
    first_all = jnp.concatenate([first] * q_blocks, axis=0)
    qe = jnp.concatenate([jnp.where(first_all, q, 0.0), jnp.where(first_all, 0.0, q)], axis=1).astype(BF16)
    entries = lambda x, j: x[j * pairs:(j + 1) * pairs]

    row = lax.broadcasted_iota(jnp.int32, (pairs, 2 * tq, KEY_BLOCK), 1)
    col = lax.broadcasted_iota(jnp.int32, (pairs, 2 * tq, KEY_BLOCK), 2)
    causal = col < jnp.where(row >= tq, row - tq, row)

    jj = lax.broadcasted_iota(jnp.int32, (KEY_BLOCK, 2 * KEY_BLOCK), 0)
    ss = lax.broadcasted_iota(jnp.int32, (KEY_BLOCK, 2 * KEY_BLOCK), 1)
    suffix = jnp.where((jj > ss) | (ss >= KEY_BLOCK), 1.0, 0.0).astype(BF16)
    suffix2 = jnp.concatenate([suffix, suffix], axis=0)

    def prefix(ref, start, n_keys):
        return ref[0, :, pl.ds(start, n_keys)].reshape(pairs, LANES, n_keys)

    def suffix_sums(log1mb):
        n = log1mb.shape[0]
        hi_lo = jnp.concatenate(_split2(log1mb.reshape(n * 2 * tq, KEY_BLOCK)), axis=1)
        return _dot(hi_lo, suffix2).reshape(n, 2 * tq, 2 * KEY_BLOCK)

    def softplus(z):
        return jnp.maximum(z, 0.0) + jnp.log(1.0 + jnp.exp(-jnp.abs(z)))

    def own_block(qe, kb, vb, c, o, mask):
        z = _bdot_nt(qe, kb.astype(BF16))
        log1mb = jnp.where(mask, -softplus(z), 0.0)
        sums = suffix_sums(log1mb)
        w = jnp.where(mask, jnp.exp(z + log1mb + (sums[:, :, :KEY_BLOCK] + c)), 0.0)
        return c + sums[:, :, KEY_BLOCK:], o + _bdot(w.astype(BF16), vb.astype(BF16))

    def sweep(qe, kb, vb, c, o, mask=None):
        z = _bdot(qe, kb.astype(BF16))
        log1mb = -softplus(z)
        if mask is not None:
            log1mb = jnp.where(mask, log1mb, 0.0)
        n_blocks = kb.shape[2] // KEY_BLOCK
        after = [None] * n_blocks
        for j in reversed(range(n_blocks)):
            sums = suffix_sums(log1mb[:, :, j * KEY_BLOCK:(j + 1) * KEY_BLOCK])
            after[j] = sums[:, :, :KEY_BLOCK] + c
            c = c + sums[:, :, KEY_BLOCK:]
        w = jnp.exp(z + log1mb + jnp.concatenate(after, axis=2))
        if mask is not None:
            w = jnp.where(mask, w, 0.0)
        return c, o + _bdot_nt(w.astype(BF16), vb.astype(BF16))

    alive = lambda c: (jnp.max(c) >= SB_DEAD_LOG).astype(jnp.int32)
    zeros = jnp.zeros((q_blocks * pairs, 2 * tq, LANES), F32)
    if own_in_prefix:
        wrow = lax.broadcasted_iota(jnp.int32, (pairs, 2 * tq, SB_WINDOW * KEY_BLOCK), 1)
        wcol = lax.broadcasted_iota(jnp.int32, (pairs, 2 * tq, SB_WINDOW * KEY_BLOCK), 2)
        wrow = jnp.where(wrow >= tq, wrow - tq, wrow)
        older, keys, values, visible = [], [], [], []
        for j in range(q_blocks):
            i = pl.program_id(1) * q_blocks + j
            n = jnp.maximum(i - (SB_WINDOW - 1), 0)
            start = pl.multiple_of(n * KEY_BLOCK, KEY_BLOCK)
            older.append(n)
            keys.append(prefix(kp_ref, start, SB_WINDOW * KEY_BLOCK))
            values.append(prefix(vp_ref, start, SB_WINDOW * KEY_BLOCK))
            visible.append(start + wcol < i * tq + wrow)
        c, o = sweep(qe, jnp.concatenate(keys, axis=0), jnp.concatenate(values, axis=0), zeros, zeros,
                     jnp.concatenate(visible, axis=0))
        still = [alive(entries(c, j)) for j in range(q_blocks)]
    else:
        c, o = own_block(qe, by_pair(kd_ref[0]), by_pair(vd_ref[0]), zeros, zeros, causal)
        older = [n_prefix]
        still = [jnp.int32(1)]

    outs = []
    for j in range(q_blocks):
        n, qe_j = older[j], entries(qe, j)

        def live(carry, n=n):
            return (carry[0] < (n >> 1)) & (carry[1] > 0)

        def body(carry, n=n, qe_j=qe_j):
            it, _, c, o = carry
            start = pl.multiple_of((n - 2 - 2 * it) * KEY_BLOCK, KEY_BLOCK)
            c, o = sweep(qe_j, prefix(kp_ref, start, 2 * KEY_BLOCK), prefix(vp_ref, start, 2 * KEY_BLOCK), c, o)
            return it + 1, alive(c), c, o

        _, still_j, c_j, o_j = lax.while_loop(live, body, (jnp.int32(0), still[j], entries(c, j), entries(o, j)))
        _, o_j = lax.cond(((n & 1) == 1) & (still_j > 0),
                          lambda c, o, qe_j=qe_j: sweep(qe_j, prefix(kp_ref, 0, KEY_BLOCK),
                                                        prefix(vp_ref, 0, KEY_BLOCK), c, o),
                          lambda c, o: (c, o), c_j, o_j)
        o_j = jnp.where(first, o_j[:, :tq], o_j[:, tq:])
        outs.append(jnp.concatenate([o_j[p] for p in range(pairs)], axis=1))
    o_ref[0] = jnp.concatenate(outs, axis=0)


def _sb_attention(q, kp, vp, own=None, *, tq):
    b, t, _ = q.shape
    tp = kp.shape[2]
    blk = lambda rows: pl.BlockSpec((1, rows, SB_WIDTH), lambda bi, i: (bi, i, 0))
    pre = pl.BlockSpec((1,) + kp.shape[1:], lambda bi, i: (bi, 0, 0))
    if own is None:
        assert tq == KEY_BLOCK and tp == t and t >= SB_WINDOW * KEY_BLOCK
        q_blocks = SB_QUERY_BLOCKS if t % (SB_QUERY_BLOCKS * tq) == 0 else 1
        in_specs, args = [blk(q_blocks * tq), pre, pre], (q, kp, vp)
    else:
        q_blocks = 1
        in_specs, args = [blk(tq), blk(KEY_BLOCK), blk(KEY_BLOCK), pre, pre], (q,) + tuple(own) + (kp, vp)
    return pl.pallas_call(
        functools.partial(_sb_kernel, tq=tq, q_blocks=q_blocks, own_in_prefix=own is None,
                          n_prefix=tp // KEY_BLOCK),
        grid=(b, t // (q_blocks * tq)),
        in_specs=in_specs,
        out_specs=blk(q_blocks * tq),
        out_shape=jax.ShapeDtypeStruct(q.shape, F32),
        compiler_params=pltpu.CompilerParams(dimension_semantics=("arbitrary",) * 2,
                                             vmem_limit_bytes=VMEM_LIMIT),
        name="stickbreak",
    )(*args)


def _rwkv_kernel(x_ref, l_ref, sx_ref, sl_ref, s0_ref, mux_ref, mul_ref, w0_ref, wd_ref, a0_ref, wa_ref,
                 wg_ref, kkw_ref, kaw_ref, rkw_ref, lnw_ref, lnb_ref,
                 o_ref, sout_ref,
                 px_s, pl_s, y_s, z_s, qe_s, yl_s, phi_s, psi_s,
                 *, nbat, tb, t_valid):
    t = pl.program_id(0)
    pairs = RW_WIDTH // LANES
    rows_all = nbat * tb
    pr = lax.broadcasted_iota(jnp.int32, (HEAD_DIM, LANES), 0)
    pc = lax.broadcasted_iota(jnp.int32, (HEAD_DIM, LANES), 1)
    to_low = jnp.where(pc == pr, 1.0, 0.0).astype(BF16)
    to_high = jnp.where(pc == pr + HEAD_DIM, 1.0, 0.0).astype(BF16)

    @pl.when(t == 0)
    def _():
        px_s[...] = sx_ref[...]
        pl_s[...] = sl_ref[...]
        for p in range(pairs):
            for bi in range(nbat):
                z_s[p * nbat + bi] = jnp.concatenate(
                    [_mm_exact_rhs(s0_ref[bi, 2 * p], to_low, 3),
                     _mm_exact_rhs(s0_ref[bi, 2 * p + 1], to_high, 3)], axis=0).T

    def shift(x_ref, carry_s, mu_ref):
        x = x_ref[...]
        prev = pltpu.roll(x.reshape(rows_all, x.shape[2]), 1, 0).reshape(x.shape)
        prev = jnp.where(lax.broadcasted_iota(jnp.int32, x.shape, 1) == 0, carry_s[...], prev)
        carry_s[...] = x[:, tb - 1:tb, :]
        return (x + (prev - x) * mu_ref[...]).reshape(rows_all, x.shape[2])

    xs = shift(x_ref, px_s, mux_ref)
    xr, xk, xv = xs[:, :RW_WIDTH], xs[:, RW_WIDTH:2 * RW_WIDTH], xs[:, 2 * RW_WIDTH:]
    xl = shift(l_ref, pl_s, mul_ref)

    hr = lax.broadcasted_iota(jnp.int32, (LANES, LANES), 0) // HEAD_DIM
    hc = lax.broadcasted_iota(jnp.int32, (LANES, LANES), 1) // HEAD_DIM
    head_ones = jnp.where(hr == hc, 1.0, 0.0).astype(BF16)
    head_sum = lambda x, parts=2: jnp.concatenate(
        [_mm_exact_rhs(x[:, p * LANES:(p + 1) * LANES], head_ones, parts) for p in range(pairs)], axis=1)

    lora = lambda x, w_ref, lo, hi: _dot(x.astype(BF16), w_ref[lo:hi, :].astype(BF16))
    log_w = -DECAY_SCALE * _sigmoid(w0_ref[...] + lora(jnp.tanh(xl[:, :LANES]), wd_ref, 0, LANES))
    a = _sigmoid(a0_ref[...] + lora(xl[:, :LANES], wa_ref, 0, LANES))
    kk = xk * kkw_ref[...]
    kk = kk * lax.rsqrt(jnp.maximum(head_sum(kk * kk, 1), 1e-24))
    k2 = xk * (1.0 + (a - 1.0) * kaw_ref[...])
    b = kk * a
    if t_valid < tb:
        valid = lax.broadcasted_iota(jnp.int32, (nbat, tb, RW_WIDTH), 1).reshape(rows_all, RW_WIDTH) < t_valid
        log_w = jnp.where(valid, log_w, 0.0)
        kk = jnp.where(valid, kk, 0.0)
        b = jnp.where(valid, b, 0.0)
        k2 = jnp.where(valid, k2, 0.0)
        xv = jnp.where(valid, xv, 0.0)
    nc = tb // CHUNK
    nb = pairs * nbat * nc
    e2 = 2 * CHUNK
    by_chunk = lambda x: jnp.stack(
        [x[:, p * LANES:(p + 1) * LANES] for p in range(pairs)]).reshape(nb, CHUNK, LANES)
    ti = lax.broadcasted_iota(jnp.int32, (nb, CHUNK, CHUNK), 1)
    tj = lax.broadcasted_iota(jnp.int32, (nb, CHUNK, CHUNK), 2)
    tri = jnp.where(tj <= ti, 1.0, 0.0).astype(BF16)
    lw3 = by_chunk(log_w)
    lw_hi, lw_lo = _split2(lw3)
    cum3 = _bdot(tri, lw_hi) + _bdot(tri, lw_lo)
    tot = cum3[:, CHUNK - 1:CHUNK, :]
    first = lax.broadcasted_iota(jnp.int32, (nb, CHUNK, LANES), 2) < HEAD_DIM
    expand = lambda x: jnp.concatenate([jnp.where(first, x, 0.0), jnp.where(first, 0.0, x)], axis=1)
    e_neg = jnp.exp(-cum3)
    e_tail = jnp.exp(tot - cum3)
    kk3, b3, k3, v3 = by_chunk(kk), by_chunk(b), by_chunk(k2), by_chunk(xv)
    kkt = kk3 * jnp.exp(cum3 - lw3)
    rt = by_chunk(xr) * jnp.exp(cum3)

    wr = lax.broadcasted_iota(jnp.int32, (nb, CHUNK, e2), 1)
    wc = lax.broadcasted_iota(jnp.int32, (nb, CHUNK, e2), 2)
    head0 = wc < CHUNK
    ws = jnp.where(head0, wc, wc - CHUNK)
    strict, incl, eye_w = ws < wr, ws <= wr, ws == wr
    diag2 = lambda x: jnp.concatenate([jnp.where(head0, x, 0.0), jnp.where(head0, 0.0, x)], axis=1)

    gram = _bdot_nt(jnp.concatenate([kkt, rt], axis=1).astype(BF16),
                    jnp.concatenate([expand(k3 * e_neg), expand(b3 * e_neg)], axis=1).astype(BF16))
    ak = jnp.where(strict, gram[:, :CHUNK, :e2], 0.0)
    ab = jnp.where(strict, gram[:, :CHUNK, e2:], 0.0)
    rk = jnp.where(incl, gram[:, CHUNK:, :e2], 0.0)
    rb = jnp.where(incl, gram[:, CHUNK:, e2:], 0.0)

    eye2 = jnp.where(eye_w, 1.0, 0.0)
    ab_d = diag2(ab)

    def inverse(product):
        tinv = eye2 - ab
        pw = product(ab, ab_d)
        n_sq = CHUNK.bit_length() - 2
        for s in range(n_sq):
            if s + 1 < n_sq:
                both = product(pw, jnp.concatenate([diag2(tinv), diag2(pw)], axis=2))
                tinv = tinv + both[:, :, :e2]
                pw = both[:, :, e2:]
            else:
                tinv = tinv + product(pw, diag2(tinv))
        return tinv, eye2 - tinv - _bmm3(ab, diag2(tinv))

    ve = expand(v3)
    akve = _bmm1(ak, ve)
    per_pair = lambda x: x.reshape((pairs * nbat, nc) + x.shape[1:])
    zr = lax.broadcasted_iota(jnp.int32, (nb, HEAD_DIM, LANES), 1)
    zc = lax.broadcasted_iota(jnp.int32, (nb, HEAD_DIM, LANES), 2)
    zhead0 = zc < HEAD_DIM
    own_block = lambda x: jnp.where(zhead0, x[:, :HEAD_DIM], x[:, HEAD_DIM:])
    decay = jnp.where(jnp.where(zhead0, zc, zc - HEAD_DIM) == zr, jnp.exp(tot), 0.0)

    def build_operators(product):
        tinv, resid = inverse(product)
        row_ok = jnp.sum(jnp.abs(resid), axis=2, keepdims=True) <= INVERSE_RESID_MAX
        tinv = tinv + _bmm1(tinv, diag2(resid))
        wu = _bmm1(tinv, jnp.concatenate([expand(kkt), expand(akve)], axis=2))
        w_, u_loc = wu[:, :, :LANES], wu[:, :, LANES:]
        yq = _bmm1(jnp.concatenate([rk, -rb], axis=2),
                   jnp.concatenate([jnp.concatenate([ve, jnp.zeros_like(ve)], axis=2),
                                    jnp.concatenate([expand(u_loc), expand(w_)], axis=2)], axis=1))
        qe_s[...] = per_pair(rt + yq[:, :, LANES:])
        yl_s[...] = per_pair(yq[:, :, :LANES])
        zp = _bmm1(jnp.swapaxes(jnp.concatenate([k3 * e_tail, -(b3 * e_tail)], axis=1), 1, 2),
                   jnp.concatenate([jnp.concatenate([v3, jnp.zeros_like(v3)], axis=2),
                                    jnp.concatenate([u_loc, w_], axis=2)], axis=1))
        phi_s[...] = per_pair(decay + own_block(zp[:, :, LANES:]))
        psi_s[...] = per_pair(own_block(zp[:, :, :LANES]))
        return jnp.min(row_ok.astype(jnp.int32)) == 1

    resid_ok = build_operators(_bmm1)

    @pl.when(jnp.logical_not(resid_ok))
    def _():
        build_operators(_bmm3)

    shead0 = lax.broadcasted_iota(jnp.int32, (pairs * nbat, HEAD_DIM, LANES), 2) < HEAD_DIM

    def advance(c, _):
        rows = pl.ds(pl.multiple_of(c * CHUNK, CHUNK), CHUNK)
        z = z_s[...]
        y_s[:, rows, :] = _bmm1(qe_s[:, c], z) + yl_s[:, c]
        z_new = _bmm3(phi_s[:, c], z) + psi_s[:, c]
        z_s[...] = jnp.concatenate([jnp.where(shead0, z_new, 0.0), jnp.where(shead0, 0.0, z_new)], axis=1)
        return 0

    lax.fori_loop(0, nc, advance, 0, unroll=min(nc, 2))

    y = jnp.concatenate([y_s[p * nbat:(p + 1) * nbat].reshape(rows_all, LANES) for p in range(pairs)], axis=1)
    d = y - head_sum(y) * (1.0 / HEAD_DIM)
    var = head_sum(d * d) * (1.0 / HEAD_DIM)
    o = d * lax.rsqrt(var + GN_EPS) * lnw_ref[...] + lnb_ref[...]
    bonus = head_sum(xr * k2 * rkw_ref[...]) * xv
    gate = lora(_sigmoid(xl[:, LANES:]), wg_ref, LANES, LORA_PAD)
    o_ref[...] = ((o + bonus) * gate).reshape(nbat, tb, RW_WIDTH)

    @pl.when(t == pl.num_programs(0) - 1)
    def _():
        for p in range(pairs):
            for bi in range(nbat):
                s_pair = z_s[p * nbat + bi].T
                sout_ref[bi, 2 * p] = s_pair[:HEAD_DIM, :HEAD_DIM]
                sout_ref[bi, 2 * p + 1] = _mm_exact_rhs(s_pair[HEAD_DIM:], to_high, 3, transposed=True)


def _rwkv(rkv, lora, shift_rkv, shift_lora, z0, prm, *, tb, t_valid):
    b, t, _ = rkv.shape
    nt = t // tb
    nc = tb // CHUNK
    pairs = RW_WIDTH // LANES
    rows = lambda w: pl.BlockSpec((b, tb, w), lambda ti: (0, ti, 0))
    first = lambda w: pl.BlockSpec((b, 1, w), lambda ti: (0, 0, 0))
    state = pl.BlockSpec((b, 2 * pairs, HEAD_DIM, HEAD_DIM), lambda ti: (0, 0, 0, 0))
    vec = _const_spec((1, RW_WIDTH))
    mat = _const_spec((LORA_PAD, RW_WIDTH))
    return pl.pallas_call(
        functools.partial(_rwkv_kernel, nbat=b, tb=tb, t_valid=t_valid),
        grid=(nt,),
        in_specs=[rows(3 * RW_WIDTH), rows(LORA_PAD), first(3 * RW_WIDTH), first(LORA_PAD), state,
                  _const_spec((1, 3 * RW_WIDTH)), _const_spec((1, LORA_PAD)),
                  vec, mat, vec, mat, mat, vec, vec, vec, vec, vec],
        out_specs=[rows(RW_WIDTH), state],
        out_shape=[jax.ShapeDtypeStruct((b, t, RW_WIDTH), F32),
                   jax.ShapeDtypeStruct((b, 2 * pairs, HEAD_DIM, HEAD_DIM), F32)],
        scratch_shapes=[pltpu.VMEM((b, 1, 3 * RW_WIDTH), F32), pltpu.VMEM((b, 1, LORA_PAD), F32),
                        pltpu.VMEM((pairs * b, tb, LANES), F32), pltpu.VMEM((pairs * b, LANES, LANES), F32)]
        + [pltpu.VMEM((pairs * b, nc, CHUNK, LANES), F32)] * 2
        + [pltpu.VMEM((pairs * b, nc, HEAD_DIM, LANES), F32)] * 2,
        compiler_params=pltpu.CompilerParams(dimension_semantics=("arbitrary",),
                                             vmem_limit_bytes=VMEM_LIMIT),
        name="rwkv7",
    )(rkv, lora, shift_rkv, shift_lora, z0, prm["mu_rkv"], prm["mu_lora"],
      prm["w0"], prm["wd"], prm["a0"], prm["wa"], prm["wg"],
      prm["k_k"], prm["k_a"], prm["r_k"], prm["ln_w"], prm["ln_b"])


def _out_ffn_kernel(x_ref, sb_ref, rw_ref, woa_ref, wob_ref, gf_ref, wg_ref, wu_ref, wd_ref, gl_ref,
                    y_ref):
    x1 = x_ref[...] + _dot(sb_ref[...].astype(BF16), woa_ref[...]) \
        + _dot(rw_ref[...].astype(BF16), wob_ref[...])
    h2 = _rmsnorm(x1, gf_ref[...]).astype(BF16)
    gate = _dot(h2, wg_ref[...])
    act = gate * _sigmoid(gate) * _dot(h2, wu_ref[...])
    x2 = x1 + _dot(act.astype(BF16), wd_ref[...])
    y_ref[...] = _rmsnorm(x2, gl_ref[...])


def _out_ffn(x2d, o_sb, o_rw, woa, wob, gf, wg, wu, wd, gl, tm):
    m, d = x2d.shape
    row = lambda w: pl.BlockSpec((tm, w), lambda i: (i, 0))
    return pl.pallas_call(
        _out_ffn_kernel,
        grid=(m // tm,),
        in_specs=[row(d), row(SB_WIDTH), row(RW_WIDTH), _const_spec(woa.shape), _const_spec(wob.shape),
                  _const_spec((1, d)), _const_spec(wg.shape), _const_spec(wu.shape),
                  _const_spec(wd.shape), _const_spec((1, d))],
        out_specs=row(d),
        out_shape=jax.ShapeDtypeStruct((m, d), F32),
        compiler_params=pltpu.CompilerParams(dimension_semantics=("arbitrary",),
                                             vmem_limit_bytes=VMEM_LIMIT),
        name="out_ffn",
    )(x2d, o_sb, o_rw, woa, wob, gf, wg, wu, wd, gl)


def _pad_rows(x, rows):
    return jnp.pad(x, ((0, 0), (0, rows - x.shape[1]), (0, 0)))


def _pad_lanes(x, lanes):
    return jnp.pad(x, [(0, 0)] * (x.ndim - 1) + [(0, lanes - x.shape[-1])])


def _tile_sizes(b, t, feature_major_kv):
    rows = b * t
    return dict(tm_proj=math.gcd(t, PROJ_ROWS) if feature_major_kv else min(rows, PROJ_ROWS),
                tm_ffn=min(rows, FFN_ROWS), tq=min(t, KEY_BLOCK), tb=CHUNK * min(2, -(-t // CHUNK)))


def _layer(x, k_past, v_past, wkv0, shift0, w):
    b, t, d = x.shape
    tiles = _tile_sizes(b, t, feature_major_kv=k_past is None)
    tq, tb = tiles["tq"], tiles["tb"]
    x2d = x.reshape(b * t, d)
    to_feature_major = lambda a: jnp.transpose(a, (0, 2, 3, 1)).reshape(b, SB_WIDTH, a.shape[1])
    from_feature_major = lambda a: jnp.transpose(a.reshape(b, SB_HEADS, HEAD_DIM, a.shape[2]), (0, 3, 1, 2))
    if k_past is None:
        q, k, v, rkv, lora = _norm_proj(x, w["norm_mix_g"], w["w_q"], w["w_kv_t"], w["w_rkv"], w["w_lora"],
                                        tiles["tm_proj"], feature_major_kv=True)
        o_sb = _sb_attention(q.reshape(b, t, SB_WIDTH), k, v, tq=tq)
        k_heads, v_heads = from_feature_major(k), from_feature_major(v)
    else:
        q, k, v, rkv, lora = _norm_proj(x, w["norm_mix_g"], w["w_q"], w["w_kv"], w["w_rkv"], w["w_lora"],
                                        tiles["tm_proj"], feature_major_kv=False)
        q3, k3, v3 = (a.reshape(b, t, SB_WIDTH) for a in (q, k, v))
        o_sb = _sb_attention(q3, to_feature_major(k_past), to_feature_major(v_past),
                             (_pad_rows(k3, KEY_BLOCK), _pad_rows(v3, KEY_BLOCK)), tq=tq)
        k_heads, v_heads = (a.reshape(b, t, SB_HEADS, HEAD_DIM) for a in (k, v))

    rkv3 = rkv.reshape(b, t, 3 * RW_WIDTH)
    lora3 = lora.reshape(b, t, LORA_PAD)
    t_pad = -(-t // tb) * tb
    o_rw, wkv_out = _rwkv(_pad_rows(rkv3, t_pad), _pad_rows(lora3, t_pad),
                          shift0[..., :3 * RW_WIDTH], _pad_lanes(shift0[..., 3 * RW_WIDTH:], LORA_PAD),
                          wkv0, w, tb=tb, t_valid=min(t, tb))
    o_rw = o_rw[:, :t]
    shift_last = jnp.concatenate([rkv3[:, t - 1:, :], lora3[:, t - 1:, :LORA_WIDTH]], axis=-1)

    y = _out_ffn(x2d, o_sb.reshape(b * t, SB_WIDTH), o_rw.reshape(b * t, RW_WIDTH),
                 w["w_out_sb"], w["w_out_rw"], w["norm_ffn_g"], w["w_gate"], w["w_up"], w["w_down"],
                 w["norm_final_g"], tiles["tm_ffn"])
    return y.reshape(b, t, d), k_heads, v_heads, wkv_out, shift_last


def kernel(x_prompt, x_sample, cache_k, cache_v, state_wkv, state_shift, norm_mix_g, w_in, mu_shift, w0,
           w_decay_up, a0, w_aaa_up, w_gate_up, k_k, k_a, r_k, ln_x_w, ln_x_b, w_out, norm_ffn_g, w_gate,
           w_up, w_down, norm_final_g):
    assert w_in.shape[0] == 1, "single-layer trunk"
    l = 0
    rw3 = 3 * RW_WIDTH
    lora_rows = lambda m, lo, hi: jnp.pad(m, ((lo, LORA_PAD - hi), (0, 0)))
    w = {
        "norm_mix_g": norm_mix_g[l][None, :],
        "w_q": w_in[l][:, :SB_WIDTH].astype(BF16),
        "w_kv": w_in[l][:, SB_WIDTH:3 * SB_WIDTH].astype(BF16),
        "w_kv_t": w_in[l][:, SB_WIDTH:3 * SB_WIDTH].T.astype(BF16),
        "w_rkv": w_in[l][:, 3 * SB_WIDTH:3 * SB_WIDTH + rw3].astype(BF16),
        "w_lora": _pad_lanes(w_in[l][:, 3 * SB_WIDTH + rw3:], LORA_PAD).astype(BF16),
        "mu_rkv": mu_shift[l][None, :rw3],
        "mu_lora": _pad_lanes(mu_shift[l][None, rw3:], LORA_PAD),
        "w0": w0[l][None, :],
        "wd": lora_rows(w_decay_up[l], 0, LORA_DECAY),
        "a0": a0[l][None, :],
        "wa": lora_rows(w_aaa_up[l], LORA_DECAY, LORA_DECAY + LORA_AAA),
        "wg": lora_rows(w_gate_up[l], LORA_DECAY + LORA_AAA, LORA_WIDTH),
        "k_k": k_k[l][None, :],
        "k_a": k_a[l][None, :],
        "r_k": r_k[l].reshape(1, RW_WIDTH),
        "ln_w": ln_x_w[l][None, :],
        "ln_b": ln_x_b[l][None, :],
        "w_out_sb": w_out[l][:SB_WIDTH].astype(BF16),
        "w_out_rw": w_out[l][SB_WIDTH:].astype(BF16),
        "norm_ffn_g": norm_ffn_g[l][None, :],
        "w_gate": w_gate[l].astype(BF16),
        "w_up": w_up[l].astype(BF16),
        "w_down": w_down[l].astype(BF16),
        "norm_final_g": norm_final_g[None, :],
    }
    bp = x_prompt.shape[0]
    heads = RW_WIDTH // HEAD_DIM
    wkv_zero = jnp.zeros((bp, heads, HEAD_DIM, HEAD_DIM), x_prompt.dtype)
    shift_zero = jnp.zeros((bp, 1, rw3 + LORA_WIDTH), x_prompt.dtype)
    yp, k1, v1, s1, sh1 = _layer(x_prompt, None, None, wkv_zero, shift_zero, w)
    ys, k2, v2, s2, sh2 = _layer(x_sample, cache_k[l], cache_v[l], state_wkv[l], state_shift[l], w)
    return (yp, ys, k1[None], v1[None], s1[None], sh1[None], k2[None], v2[None], s2[None], sh2[None])
```

```python
import functools
import math

import jax
import jax.numpy as jnp
from jax import lax
from jax.experimental import pallas as pl
from jax.experimental.pallas import tpu as pltpu

F32 = jnp.float32
BF16 = jnp.bfloat16

LANES = 128
SUBLANES = 8
HEAD_DIM = 64
SB_WIDTH = 512
SB_HEADS = SB_WIDTH // HEAD_DIM
RW_WIDTH = 512
LORA_DECAY = 64
LORA_AAA = 64
LORA_GATE = 160
LORA_WIDTH = LORA_DECAY + LORA_AAA + LORA_GATE
LORA_PAD = 384
RMS_EPS = 1e-6
GN_EPS = 64e-5
DECAY_SCALE = math.exp(-0.5)
KEY_BLOCK = 128
SB_DEAD_LOG = -104.0
SB_WINDOW = 3
SB_QUERY_BLOCKS = 4
CHUNK = 64
INVERSE_RESID_MAX = 0.02
VMEM_LIMIT = 56 * 1024 * 1024
PROJ_ROWS = 1024
FFN_ROWS = 512


def _dot(a, b):
    return jnp.dot(a, b, preferred_element_type=F32)


def _split2(x):
    hi = x.astype(BF16)
    lo = (x - hi.astype(F32)).astype(BF16)
    return hi, lo


def _bdot(a, b):
    return jnp.einsum("bij,bjk->bik", a, b, preferred_element_type=F32)


def _bdot_nt(a, b):
    return jnp.einsum("bik,bjk->bij", a, b, preferred_element_type=F32)


def _bmm1(a, b):
    return _bdot(a.astype(BF16), b.astype(BF16))


def _bmm3(a, b):
    ah, al = _split2(a)
    bh, bl = _split2(b)
    return _bdot(ah, bh) + (_bdot(ah, bl) + _bdot(al, bh))


def _mm_exact_rhs(x, m, parts=2, transposed=False):
    contract = (((1,), (1 if transposed else 0,)), ((), ()))
    acc = None
    for _ in range(parts):
        part = x.astype(BF16)
        prod = lax.dot_general(part, m, contract, preferred_element_type=F32)
        acc = prod if acc is None else acc + prod
        x = x - part.astype(F32)
    return acc


def _sigmoid(x):
    return 0.5 * jnp.tanh(0.5 * x) + 0.5


def _rmsnorm(x, g):
    return x * lax.rsqrt(jnp.mean(x * x, axis=-1, keepdims=True) + RMS_EPS) * g


def _const_spec(shape):
    return pl.BlockSpec(shape, lambda *_: (0,) * len(shape), pipeline_mode=pl.Buffered(1))


def _norm_proj_kernel(x_ref, g_ref, wq_ref, wkv_ref, wrkv_ref, wl_ref,
                      q_ref, k_ref, v_ref, rkv_ref, l_ref, *, feature_major_kv):
    hb = _rmsnorm(x_ref[...], g_ref[...]).astype(BF16)
    q_ref[...] = _dot(hb, wq_ref[...])
    if feature_major_kv:
        kv = lax.dot_general(wkv_ref[...], hb, (((1,), (1,)), ((), ())), preferred_element_type=F32)
        k_ref[0] = kv[:SB_WIDTH]
        v_ref[0] = kv[SB_WIDTH:]
    else:
        kv = _dot(hb, wkv_ref[...])
        k_ref[...] = kv[:, :SB_WIDTH]
        v_ref[...] = kv[:, SB_WIDTH:]
    rkv_ref[...] = _dot(hb, wrkv_ref[...])
    l_ref[...] = _dot(hb, wl_ref[...])


def _norm_proj(x, g, wq, wkv, wrkv, wl, tm, feature_major_kv):
    b, t, d = x.shape
    m = b * t
    row = lambda w: pl.BlockSpec((tm, w), lambda i: (i, 0))
    if feature_major_kv:
        per_batch = t // tm
        kv_spec = pl.BlockSpec((1, SB_WIDTH, tm), lambda i: (i // per_batch, 0, i % per_batch))
        kv_shape = jax.ShapeDtypeStruct((b, SB_WIDTH, t), F32)
    else:
        kv_spec, kv_shape = row(SB_WIDTH), jax.ShapeDtypeStruct((m, SB_WIDTH), F32)
    return pl.pallas_call(
        functools.partial(_norm_proj_kernel, feature_major_kv=feature_major_kv),
        grid=(m // tm,),
        in_specs=[row(d), _const_spec((1, d)), _const_spec(wq.shape), _const_spec(wkv.shape),
                  _const_spec(wrkv.shape), _const_spec(wl.shape)],
        out_specs=[row(SB_WIDTH), kv_spec, kv_spec, row(3 * RW_WIDTH), row(LORA_PAD)],
        out_shape=[jax.ShapeDtypeStruct((m, SB_WIDTH), F32), kv_shape, kv_shape,
                   jax.ShapeDtypeStruct((m, 3 * RW_WIDTH), F32), jax.ShapeDtypeStruct((m, LORA_PAD), F32)],
        compiler_params=pltpu.CompilerParams(dimension_semantics=("arbitrary",),
                                             vmem_limit_bytes=VMEM_LIMIT),
        name="norm_proj",
    )(x.reshape(m, d), g, wq, wkv, wrkv, wl)


def _sb_kernel(q_ref, *refs, tq, q_blocks, own_in_prefix, n_prefix):
    if own_in_prefix:
        kp_ref, vp_ref, o_ref = refs
    else:
        kd_ref, vd_ref, kp_ref, vp_ref, o_ref = refs
    pairs = SB_WIDTH // LANES
    by_pair = lambda x: jnp.stack([x[:, p * LANES:(p + 1) * LANES] for p in range(pairs)])
    q_all = q_ref[0] * (1.0 / math.sqrt(HEAD_DIM))
    q = jnp.concatenate([by_pair(q_all[j * tq:(j + 1) * tq]) for j in range(q_blocks)], axis=0)
    first = lax.broadcasted_iota(jnp.int32, (pairs, tq, LANES), 2) < HEAD_DIM
    first_all = jnp.concatenate([first] * q_blocks, axis=0)
    qe = jnp.concatenate([jnp.where(first_all, q, 0.0), jnp.where(first_all, 0.0, q)], axis=1).astype(BF16)
    entries = lambda x, j: x[j * pairs:(j + 1) * pairs]

    row = lax.broadcasted_iota(jnp.int32, (pairs, 2 * tq, KEY_BLOCK), 1)
    col = lax.broadcasted_iota(jnp.int32, (pairs, 2 * tq, KEY_BLOCK), 2)
    causal = col < jnp.where(row >= tq, row - tq, row)

    jj = lax.broadcasted_iota(jnp.int32, (KEY_BLOCK, 2 * KEY_BLOCK), 0)
    ss = lax.broadcasted_iota(jnp.int32, (KEY_BLOCK, 2 * KEY_BLOCK), 1)
    suffix = jnp.where((jj > ss) | (ss >= KEY_BLOCK), 1.0, 0.0).astype(BF16)
    suffix2 = jnp.concatenate([suffix, suffix], axis=0)

    def prefix(ref, start, n_keys):
        return ref[0, :, pl.ds(start, n_keys)].reshape(pairs, LANES, n_keys)

    def suffix_sums(log1mb):
        n = log1mb.shape[0]
        hi_lo = jnp.concatenate(_split2(log1mb.reshape(n * 2 * tq, KEY_BLOCK)), axis=1)
        return _dot(hi_lo, suffix2).reshape(n, 2 * tq, 2 * KEY_BLOCK)

    def softplus(z):
        return jnp.maximum(z, 0.0) + jnp.log(1.0 + jnp.exp(-jnp.abs(z)))

    def own_block(qe, kb, vb, c, o, mask):
        z = _bdot_nt(qe, kb.astype(BF16))
        log1mb = jnp.where(mask, -softplus(z), 0.0)
        sums = suffix_sums(log1mb)
        w = jnp.where(mask, jnp.exp(z + log1mb + (sums[:, :, :KEY_BLOCK] + c)), 0.0)
        return c + sums[:, :, KEY_BLOCK:], o + _bdot(w.astype(BF16), vb.astype(BF16))

    def sweep(qe, kb, vb, c, o, mask=None):
        z = _bdot(qe, kb.astype(BF16))
        log1mb = -softplus(z)
        if mask is not None:
            log1mb = jnp.where(mask, log1mb, 0.0)
        n_blocks = kb.shape[2] // KEY_BLOCK
        after = [None] * n_blocks
        for j in reversed(range(n_blocks)):
            sums = suffix_sums(log1mb[:, :, j * KEY_BLOCK:(j + 1) * KEY_BLOCK])
            after[j] = sums[:, :, :KEY_BLOCK] + c
            c = c + sums[:, :, KEY_BLOCK:]
        w = jnp.exp(z + log1mb + jnp.concatenate(after, axis=2))
        if mask is not None:
            w = jnp.where(mask, w, 0.0)
        return c, o + _bdot_nt(w.astype(BF16), vb.astype(BF16))

    alive = lambda c: (jnp.max(c) >= SB_DEAD_LOG).astype(jnp.int32)
    zeros = jnp.zeros((q_blocks * pairs, 2 * tq, LANES), F32)
    if own_in_prefix:
        wrow = lax.broadcasted_iota(jnp.int32, (pairs, 2 * tq, SB_WINDOW * KEY_BLOCK), 1)
        wcol = lax.broadcasted_iota(jnp.int32, (pairs, 2 * tq, SB_WINDOW * KEY_BLOCK), 2)
        wrow = jnp.where(wrow >= tq, wrow - tq, wrow)
        older, keys, values, visible = [], [], [], []
        for j in range(q_blocks):
            i = pl.program_id(1) * q_blocks + j
            n = jnp.maximum(i - (SB_WINDOW - 1), 0)
            start = pl.multiple_of(n * KEY_BLOCK, KEY_BLOCK)
            older.append(n)
            keys.append(prefix(kp_ref, start, SB_WINDOW * KEY_BLOCK))
            values.append(prefix(vp_ref, start, SB_WINDOW * KEY_BLOCK))
            visible.append(start + wcol < i * tq + wrow)
        c, o = sweep(qe, jnp.concatenate(keys, axis=0), jnp.concatenate(values, axis=0), zeros, zeros,
                     jnp.concatenate(visible, axis=0))
        still = [alive(entries(c, j)) for j in range(q_blocks)]
    else:
        c, o = own_block(qe, by_pair(kd_ref[0]), by_pair(vd_ref[0]), zeros, zeros, causal)
        older = [n_prefix]
        still = [jnp.int32(1)]

    outs = []
    for j in range(q_blocks):
        n, qe_j = older[j], entries(qe, j)

        def live(carry, n=n):
            return (carry[0] < (n >> 1)) & (carry[1] > 0)

        def body(carry, n=n, qe_j=qe_j):
            it, _, c, o = carry
            start = pl.multiple_of((n - 2 - 2 * it) * KEY_BLOCK, KEY_BLOCK)
            c, o = sweep(qe_j, prefix(kp_ref, start, 2 * KEY_BLOCK), prefix(vp_ref, start, 2 * KEY_BLOCK), c, o)
            return it + 1, alive(c), c, o

        _, still_j, c_j, o_j = lax.while_loop(live, body, (jnp.int32(0), still[j], entries(c, j), entries(o, j)))
        _, o_j = lax.cond(((n & 1) == 1) & (still_j > 0),
                          lambda c, o, qe_j=qe_j: sweep(qe_j, prefix(kp_ref, 0, KEY_BLOCK),
                                                        prefix(vp_ref, 0, KEY_BLOCK), c, o),
                          lambda c, o: (c, o), c_j, o_j)
        o_j = jnp.where(first, o_j[:, :tq], o_j[:, tq:])
        outs.append(jnp.concatenate([o_j[p] for p in range(pairs)], axis=1))
    o_ref[0] = jnp.concatenate(outs, axis=0)


def _sb_attention(q, kp, vp, own=None, *, tq):
    b, t, _ = q.shape
    tp = kp.shape[2]
    blk = lambda rows: pl.BlockSpec((1, rows, SB_WIDTH), lambda bi, i: (bi, i, 0))
    pre = pl.BlockSpec((1,) + kp.shape[1:], lambda bi, i: (bi, 0, 0))
    if own is None:
        assert tq == KEY_BLOCK and tp == t and t >= SB_WINDOW * KEY_BLOCK
        q_blocks = SB_QUERY_BLOCKS if t % (SB_QUERY_BLOCKS * tq) == 0 else 1
        in_specs, args = [blk(q_blocks * tq), pre, pre], (q, kp, vp)
    else:
        q_blocks = 1
        in_specs, args = [blk(tq), blk(KEY_BLOCK), blk(KEY_BLOCK), pre, pre], (q,) + tuple(own) + (kp, vp)
    return pl.pallas_call(
        functools.partial(_sb_kernel, tq=tq, q_blocks=q_blocks, own_in_prefix=own is None,
                          n_prefix=tp // KEY_BLOCK),
        grid=(b, t // (q_blocks * tq)),
        in_specs=in_specs,
        out_specs=blk(q_blocks * tq),
        out_shape=jax.ShapeDtypeStruct(q.shape, F32),
        compiler_params=pltpu.CompilerParams(dimension_semantics=("arbitrary",) * 2,
                                             vmem_limit_bytes=VMEM_LIMIT),
        name="stickbreak",
    )(*args)


def _rwkv_kernel(x_ref, l_ref, sx_ref, sl_ref, s0_ref, mux_ref, mul_ref, w0_ref, wd_ref, a0_ref, wa_ref,
                 wg_ref, kkw_ref, kaw_ref, rkw_ref, lnw_ref, lnb_ref,
                 o_ref, sout_ref, ok_ref,
                 px_s, pl_s, y_s, z_s, qe_s, yl_s, phi_s, psi_s,
                 *, nbat, tb, t_valid, robust):
    t = pl.program_id(0)
    pairs = RW_WIDTH // LANES
    rows_all = nbat * tb
    pr = lax.broadcasted_iota(jnp.int32, (HEAD_DIM, LANES), 0)
    pc = lax.broadcasted_iota(jnp.int32, (HEAD_DIM, LANES), 1)
    to_low = jnp.where(pc == pr, 1.0, 0.0).astype(BF16)
    to_high = jnp.where(pc == pr + HEAD_DIM, 1.0, 0.0).astype(BF16)

    @pl.when(t == 0)
    def _():
        px_s[...] = sx_ref[...]
        pl_s[...] = sl_ref[...]
        for p in range(pairs):
            for bi in range(nbat):
                z_s[p * nbat + bi] = jnp.concatenate(
                    [_mm_exact_rhs(s0_ref[bi, 2 * p], to_low, 3),
                     _mm_exact_rhs(s0_ref[bi, 2 * p + 1], to_high, 3)], axis=0).T

    def shift(x_ref, carry_s, mu_ref):
        x = x_ref[...]
        prev = pltpu.roll(x.reshape(rows_all, x.shape[2]), 1, 0).reshape(x.shape)
        prev = jnp.where(lax.broadcasted_iota(jnp.int32, x.shape, 1) == 0, carry_s[...], prev)
        carry_s[...] = x[:, tb - 1:tb, :]
        return (x + (prev - x) * mu_ref[...]).reshape(rows_all, x.shape[2])

    xs = shift(x_ref, px_s, mux_ref)
    xr, xk, xv = xs[:, :RW_WIDTH], xs[:, RW_WIDTH:2 * RW_WIDTH], xs[:, 2 * RW_WIDTH:]
    xl = shift(l_ref, pl_s, mul_ref)

    hr = lax.broadcasted_iota(jnp.int32, (LANES, LANES), 0) // HEAD_DIM
    hc = lax.broadcasted_iota(jnp.int32, (LANES, LANES), 1) // HEAD_DIM
    head_ones = jnp.where(hr == hc, 1.0, 0.0).astype(BF16)
    head_sum = lambda x, parts=2: jnp.concatenate(
        [_mm_exact_rhs(x[:, p * LANES:(p + 1) * LANES], head_ones, parts) for p in range(pairs)], axis=1)

    lora = lambda x, w_ref, lo, hi: _dot(x.astype(BF16), w_ref[lo:hi, :].astype(BF16))
    log_w = -DECAY_SCALE * _sigmoid(w0_ref[...] + lora(jnp.tanh(xl[:, :LANES]), wd_ref, 0, LANES))
    a = _sigmoid(a0_ref[...] + lora(xl[:, :LANES], wa_ref, 0, LANES))
    kk = xk * kkw_ref[...]
    kk = kk * lax.rsqrt(jnp.maximum(head_sum(kk * kk, 1), 1e-24))
    k2 = xk * (1.0 + (a - 1.0) * kaw_ref[...])
    b = kk * a
    if t_valid < tb:
        valid = lax.broadcasted_iota(jnp.int32, (nbat, tb, RW_WIDTH), 1).reshape(rows_all, RW_WIDTH) < t_valid
        log_w = jnp.where(valid, log_w, 0.0)
        kk = jnp.where(valid, kk, 0.0)
        b = jnp.where(valid, b, 0.0)
        k2 = jnp.where(valid, k2, 0.0)
        xv = jnp.where(valid, xv, 0.0)
    nc = tb // CHUNK
    nb = pairs * nbat * nc
    e2 = 2 * CHUNK
    by_chunk = lambda x: jnp.stack(
        [x[:, p * LANES:(p + 1) * LANES] for p in range(pairs)]).reshape(nb, CHUNK, LANES)
    ti = lax.broadcasted_iota(jnp.int32, (nb, CHUNK, CHUNK), 1)
    tj = lax.broadcasted_iota(jnp.int32, (nb, CHUNK, CHUNK), 2)
    tri = jnp.where(tj <= ti, 1.0, 0.0).astype(BF16)
    lw3 = by_chunk(log_w)
    lw_hi, lw_lo = _split2(lw3)
    cum3 = _bdot(tri, lw_hi) + _bdot(tri, lw_lo)
    tot = cum3[:, CHUNK - 1:CHUNK, :]
    first = lax.broadcasted_iota(jnp.int32, (nb, CHUNK, LANES), 2) < HEAD_DIM
    expand = lambda x: jnp.concatenate([jnp.where(first, x, 0.0), jnp.where(first, 0.0, x)], axis=1)
    e_neg = jnp.exp(-cum3)
    e_tail = jnp.exp(tot - cum3)
    kk3, b3, k3, v3 = by_chunk(kk), by_chunk(b), by_chunk(k2), by_chunk(xv)
    kkt = kk3 * jnp.exp(cum3 - lw3)
    rt = by_chunk(xr) * jnp.exp(cum3)

    wr = lax.broadcasted_iota(jnp.int32, (nb, CHUNK, e2), 1)
    wc = lax.broadcasted_iota(jnp.int32, (nb, CHUNK, e2), 2)
    head0 = wc < CHUNK
    ws = jnp.where(head0, wc, wc - CHUNK)
    strict, incl, eye_w = ws < wr, ws <= wr, ws == wr
    diag2 = lambda x: jnp.concatenate([jnp.where(head0, x, 0.0), jnp.where(head0, 0.0, x)], axis=1)

    gram = _bdot_nt(jnp.concatenate([kkt, rt], axis=1).astype(BF16),
                    jnp.concatenate([expand(k3 * e_neg), expand(b3 * e_neg)], axis=1).astype(BF16))
    ak = jnp.where(strict, gram[:, :CHUNK, :e2], 0.0)
    ab = jnp.where(strict, gram[:, :CHUNK, e2:], 0.0)
    rk = jnp.where(incl, gram[:, CHUNK:, :e2], 0.0)
    rb = jnp.where(incl, gram[:, CHUNK:, e2:], 0.0)

    eye2 = jnp.where(eye_w, 1.0, 0.0)
    ab_d = diag2(ab)

    def inverse(product):
        tinv = eye2 - ab
        pw = product(ab, ab_d)
        n_sq = CHUNK.bit_length() - 2
        for s in range(n_sq):
            if s + 1 < n_sq:
                both = product(pw, jnp.concatenate([diag2(tinv), diag2(pw)], axis=2))
                tinv = tinv + both[:, :, :e2]
                pw = both[:, :, e2:]
            else:
                tinv = tinv + product(pw, diag2(tinv))
        return tinv, eye2 - tinv - _bmm3(ab, diag2(tinv))

    ve = expand(v3)
    akve = _bmm1(ak, ve)
    per_pair = lambda x: x.reshape((pairs * nbat, nc) + x.shape[1:])
    zr = lax.broadcasted_iota(jnp.int32, (nb, HEAD_DIM, LANES), 1)
    zc = lax.broadcasted_iota(jnp.int32, (nb, HEAD_DIM, LANES), 2)
    zhead0 = zc < HEAD_DIM
    own_block = lambda x: jnp.where(zhead0, x[:, :HEAD_DIM], x[:, HEAD_DIM:])
    decay = jnp.where(jnp.where(zhead0, zc, zc - HEAD_DIM) == zr, jnp.exp(tot), 0.0)

    def build_operators(product):
        tinv, resid = inverse(product)
        row_ok = jnp.sum(jnp.abs(resid), axis=2, keepdims=True) <= INVERSE_RESID_MAX
        tinv = tinv + _bmm1(tinv, diag2(resid))
        wu = _bmm1(tinv, jnp.concatenate([expand(kkt), expand(akve)], axis=2))
        w_, u_loc = wu[:, :, :LANES], wu[:, :, LANES:]
        yq = _bmm1(jnp.concatenate([rk, -rb], axis=2),
                   jnp.concatenate([jnp.concatenate([ve, jnp.zeros_like(ve)], axis=2),
                                    jnp.concatenate([expand(u_loc), expand(w_)], axis=2)], axis=1))
        qe_s[...] = per_pair(rt + yq[:, :, LANES:])
        yl_s[...] = per_pair(yq[:, :, :LANES])
        zp = _bmm1(jnp.swapaxes(jnp.concatenate([k3 * e_tail, -(b3 * e_tail)], axis=1), 1, 2),
                   jnp.concatenate([jnp.concatenate([v3, jnp.zeros_like(v3)], axis=2),
                                    jnp.concatenate([u_loc, w_], axis=2)], axis=1))
        phi_s[...] = per_pair(decay + own_block(zp[:, :, LANES:]))
        psi_s[...] = per_pair(own_block(zp[:, :, :LANES]))
        return jnp.min(row_ok.astype(jnp.int32))

    step_ok = build_operators(_bmm3 if robust else _bmm1)

    @pl.when(t == 0)
    def _():
        ok_ref[...] = jnp.ones(ok_ref.shape, jnp.int32)

    ok_ref[...] = jnp.minimum(ok_ref[...], step_ok)

    shead0 = lax.broadcasted_iota(jnp.int32, (pairs * nbat, HEAD_DIM, LANES), 2) < HEAD_DIM

    def advance(c, _):
        rows = pl.ds(pl.multiple_of(c * CHUNK, CHUNK), CHUNK)
        z = z_s[...]
        y_s[:, rows, :] = _bmm1(qe_s[:, c], z) + yl_s[:, c]
        z_new = _bmm3(phi_s[:, c], z) + psi_s[:, c]
        z_s[...] = jnp.concatenate([jnp.where(shead0, z_new, 0.0), jnp.where(shead0, 0.0, z_new)], axis=1)
        return 0

    lax.fori_loop(0, nc, advance, 0, unroll=min(nc, 2))

    y = jnp.concatenate([y_s[p * nbat:(p + 1) * nbat].reshape(rows_all, LANES) for p in range(pairs)], axis=1)
    d = y - head_sum(y) * (1.0 / HEAD_DIM)
    var = head_sum(d * d) * (1.0 / HEAD_DIM)
    o = d * lax.rsqrt(var + GN_EPS) * lnw_ref[...] + lnb_ref[...]
    bonus = head_sum(xr * k2 * rkw_ref[...]) * xv
    gate = lora(_sigmoid(xl[:, LANES:]), wg_ref, LANES, LORA_PAD)
    o_ref[...] = ((o + bonus) * gate).reshape(nbat, tb, RW_WIDTH)

    @pl.when(t == pl.num_programs(0) - 1)
    def _():
        for p in range(pairs):
            for bi in range(nbat):
                s_pair = z_s[p * nbat + bi].T
                sout_ref[bi, 2 * p] = s_pair[:HEAD_DIM, :HEAD_DIM]
                sout_ref[bi, 2 * p + 1] = _mm_exact_rhs(s_pair[HEAD_DIM:], to_high, 3, transposed=True)


def _rwkv(rkv, lora, shift_rkv, shift_lora, z0, prm, *, tb, t_valid, robust=False):
    b, t, _ = rkv.shape
    nt = t // tb
    nc = tb // CHUNK
    pairs = RW_WIDTH // LANES
    rows = lambda w: pl.BlockSpec((b, tb, w), lambda ti: (0, ti, 0))
    first = lambda w: pl.BlockSpec((b, 1, w), lambda ti: (0, 0, 0))
    state = pl.BlockSpec((b, 2 * pairs, HEAD_DIM, HEAD_DIM), lambda ti: (0, 0, 0, 0))
    vec = _const_spec((1, RW_WIDTH))
    mat = _const_spec((LORA_PAD, RW_WIDTH))
    return pl.pallas_call(
        functools.partial(_rwkv_kernel, nbat=b, tb=tb, t_valid=t_valid, robust=robust),
        grid=(nt,),
        in_specs=[rows(3 * RW_WIDTH), rows(LORA_PAD), first(3 * RW_WIDTH), first(LORA_PAD), state,
                  _const_spec((1, 3 * RW_WIDTH)), _const_spec((1, LORA_PAD)),
                  vec, mat, vec, mat, mat, vec, vec, vec, vec, vec],
        out_specs=[rows(RW_WIDTH), state, pl.BlockSpec((SUBLANES, LANES), lambda ti: (0, 0))],
        out_shape=[jax.ShapeDtypeStruct((b, t, RW_WIDTH), F32),
                   jax.ShapeDtypeStruct((b, 2 * pairs, HEAD_DIM, HEAD_DIM), F32),
                   jax.ShapeDtypeStruct((SUBLANES, LANES), jnp.int32)],
        scratch_shapes=[pltpu.VMEM((b, 1, 3 * RW_WIDTH), F32), pltpu.VMEM((b, 1, LORA_PAD), F32),
                        pltpu.VMEM((pairs * b, tb, LANES), F32), pltpu.VMEM((pairs * b, LANES, LANES), F32)]
        + [pltpu.VMEM((pairs * b, nc, CHUNK, LANES), F32)] * 2
        + [pltpu.VMEM((pairs * b, nc, HEAD_DIM, LANES), F32)] * 2,
        compiler_params=pltpu.CompilerParams(dimension_semantics=("arbitrary",),
                                             vmem_limit_bytes=VMEM_LIMIT),
        name="rwkv7",
    )(rkv, lora, shift_rkv, shift_lora, z0, prm["mu_rkv"], prm["mu_lora"],
      prm["w0"], prm["wd"], prm["a0"], prm["wa"], prm["wg"],
      prm["k_k"], prm["k_a"], prm["r_k"], prm["ln_w"], prm["ln_b"])


def _out_ffn_kernel(x_ref, sb_ref, rw_ref, woa_ref, wob_ref, gf_ref, wg_ref, wu_ref, wd_ref, gl_ref,
                    y_ref):
    x1 = x_ref[...] + _dot(sb_ref[...].astype(BF16), woa_ref[...]) \
        + _dot(rw_ref[...].astype(BF16), wob_ref[...])
    h2 = _rmsnorm(x1, gf_ref[...]).astype(BF16)
    gate = _dot(h2, wg_ref[...])
    act = gate * _sigmoid(gate) * _dot(h2, wu_ref[...])
    x2 = x1 + _dot(act.astype(BF16), wd_ref[...])
    y_ref[...] = _rmsnorm(x2, gl_ref[...])


def _out_ffn(x2d, o_sb, o_rw, woa, wob, gf, wg, wu, wd, gl, tm):
    m, d = x2d.shape
    row = lambda w: pl.BlockSpec((tm, w), lambda i: (i, 0))
    return pl.pallas_call(
        _out_ffn_kernel,
        grid=(m // tm,),
        in_specs=[row(d), row(SB_WIDTH), row(RW_WIDTH), _const_spec(woa.shape), _const_spec(wob.shape),
                  _const_spec((1, d)), _const_spec(wg.shape), _const_spec(wu.shape),
                  _const_spec(wd.shape), _const_spec((1, d))],
        out_specs=row(d),
        out_shape=jax.ShapeDtypeStruct((m, d), F32),
        compiler_params=pltpu.CompilerParams(dimension_semantics=("arbitrary",),
                                             vmem_limit_bytes=VMEM_LIMIT),
        name="out_ffn",
    )(x2d, o_sb, o_rw, woa, wob, gf, wg, wu, wd, gl)


def _pad_rows(x, rows):
    return jnp.pad(x, ((0, 0), (0, rows - x.shape[1]), (0, 0)))


def _pad_lanes(x, lanes):
    return jnp.pad(x, [(0, 0)] * (x.ndim - 1) + [(0, lanes - x.shape[-1])])


def _tile_sizes(b, t, feature_major_kv):
    rows = b * t
    return dict(tm_proj=math.gcd(t, PROJ_ROWS) if feature_major_kv else min(rows, PROJ_ROWS),
                tm_ffn=min(rows, FFN_ROWS), tq=min(t, KEY_BLOCK), tb=CHUNK * min(2, -(-t // CHUNK)))


def _layer(x, k_past, v_past, wkv0, shift0, w):
    b, t, d = x.shape
    tiles = _tile_sizes(b, t, feature_major_kv=k_past is None)
    tq, tb = tiles["tq"], tiles["tb"]
    x2d = x.reshape(b * t, d)
    to_feature_major = lambda a: jnp.transpose(a, (0, 2, 3, 1)).reshape(b, SB_WIDTH, a.shape[1])
    from_feature_major = lambda a: jnp.transpose(a.reshape(b, SB_HEADS, HEAD_DIM, a.shape[2]), (0, 3, 1, 2))
    if k_past is None:
        q, k, v, rkv, lora = _norm_proj(x, w["norm_mix_g"], w["w_q"], w["w_kv_t"], w["w_rkv"], w["w_lora"],
                                        tiles["tm_proj"], feature_major_kv=True)
        o_sb = _sb_attention(q.reshape(b, t, SB_WIDTH), k, v, tq=tq)
        k_heads, v_heads = from_feature_major(k), from_feature_major(v)
    else:
        q, k, v, rkv, lora = _norm_proj(x, w["norm_mix_g"], w["w_q"], w["w_kv"], w["w_rkv"], w["w_lora"],
                                        tiles["tm_proj"], feature_major_kv=False)
        q3, k3, v3 = (a.reshape(b, t, SB_WIDTH) for a in (q, k, v))
        o_sb = _sb_attention(q3, to_feature_major(k_past), to_feature_major(v_past),
                             (_pad_rows(k3, KEY_BLOCK), _pad_rows(v3, KEY_BLOCK)), tq=tq)
        k_heads, v_heads = (a.reshape(b, t, SB_HEADS, HEAD_DIM) for a in (k, v))

    rkv3 = rkv.reshape(b, t, 3 * RW_WIDTH)
    lora3 = lora.reshape(b, t, LORA_PAD)
    t_pad = -(-t // tb) * tb
    rwkv_args = (_pad_rows(rkv3, t_pad), _pad_rows(lora3, t_pad), shift0[..., :3 * RW_WIDTH],
                 _pad_lanes(shift0[..., 3 * RW_WIDTH:], LORA_PAD), wkv0, w)
    o_rw, wkv_out, inverse_ok = _rwkv(*rwkv_args, tb=tb, t_valid=min(t, tb))
    o_rw, wkv_out = lax.cond(inverse_ok[0, 0] == 1, lambda: (o_rw, wkv_out),
                             lambda: _rwkv(*rwkv_args, tb=tb, t_valid=min(t, tb), robust=True)[:2])
    o_rw = o_rw[:, :t]
    shift_last = jnp.concatenate([rkv3[:, t - 1:, :], lora3[:, t - 1:, :LORA_WIDTH]], axis=-1)

    y = _out_ffn(x2d, o_sb.reshape(b * t, SB_WIDTH), o_rw.reshape(b * t, RW_WIDTH),
                 w["w_out_sb"], w["w_out_rw"], w["norm_ffn_g"], w["w_gate"], w["w_up"], w["w_down"],
                 w["norm_final_g"], tiles["tm_ffn"])
    return y.reshape(b, t, d), k_heads, v_heads, wkv_out, shift_last


def kernel(x_prompt, x_sample, cache_k, cache_v, state_wkv, state_shift, norm_mix_g, w_in, mu_shift, w0,
           w_decay_up, a0, w_aaa_up, w_gate_up, k_k, k_a, r_k, ln_x_w, ln_x_b, w_out, norm_ffn_g, w_gate,
           w_up, w_down, norm_final_g):
    assert w_in.shape[0] == 1, "single-layer trunk"
    l = 0
    rw3 = 3 * RW_WIDTH
    lora_rows = lambda m, lo, hi: jnp.pad(m, ((lo, LORA_PAD - hi), (0, 0)))
    w = {
        "norm_mix_g": norm_mix_g[l][None, :],
        "w_q": w_in[l][:, :SB_WIDTH].astype(BF16),
        "w_kv": w_in[l][:, SB_WIDTH:3 * SB_WIDTH].astype(BF16),
        "w_kv_t": w_in[l][:, SB_WIDTH:3 * SB_WIDTH].T.astype(BF16),
        "w_rkv": w_in[l][:, 3 * SB_WIDTH:3 * SB_WIDTH + rw3].astype(BF16),
        "w_lora": _pad_lanes(w_in[l][:, 3 * SB_WIDTH + rw3:], LORA_PAD).astype(BF16),
        "mu_rkv": mu_shift[l][None, :rw3],
        "mu_lora": _pad_lanes(mu_shift[l][None, rw3:], LORA_PAD),
        "w0": w0[l][None, :],
        "wd": lora_rows(w_decay_up[l], 0, LORA_DECAY),
        "a0": a0[l][None, :],
        "wa": lora_rows(w_aaa_up[l], LORA_DECAY, LORA_DECAY + LORA_AAA),
        "wg": lora_rows(w_gate_up[l], LORA_DECAY + LORA_AAA, LORA_WIDTH),
        "k_k": k_k[l][None, :],
        "k_a": k_a[l][None, :],
        "r_k": r_k[l].reshape(1, RW_WIDTH),
        "ln_w": ln_x_w[l][None, :],
        "ln_b": ln_x_b[l][None, :],
        "w_out_sb": w_out[l][:SB_WIDTH].astype(BF16),
        "w_out_rw": w_out[l][SB_WIDTH:].astype(BF16),
        "norm_ffn_g": norm_ffn_g[l][None, :],
        "w_gate": w_gate[l].astype(BF16),
        "w_up": w_up[l].astype(BF16),
        "w_down": w_down[l].astype(BF16),
        "norm_final_g": norm_final_g[None, :],
    }
    bp = x_prompt.shape[0]
    heads = RW_WIDTH // HEAD_DIM
    wkv_zero = jnp.zeros((bp, heads, HEAD_DIM, HEAD_DIM), x_prompt.dtype)
    shift_zero = jnp.zeros((bp, 1, rw3 + LORA_WIDTH), x_prompt.dtype)
    yp, k1, v1, s1, sh1 = _layer(x_prompt, None, None, wkv_zero, shift_zero, w)
    ys, k2, v2, s2, sh2 = _layer(x_sample, cache_k[l], cache_v[l], state_wkv[l], state_shift[l], w)
    return (yp, ys, k1[None], v1[None], s1[None], sh1[None], k2[None], v2[None], s2[None], sh2[None])
```

```python
import functools
import math

import jax
import jax.numpy as jnp
from jax import lax
from jax.experimental import pallas as pl
from jax.experimental.pallas import tpu as pltpu

F32 = jnp.float32
BF16 = jnp.bfloat16

LANES = 128
SUBLANES = 8
HEAD_DIM = 64
SB_WIDTH = 512
SB_HEADS = SB_WIDTH // HEAD_DIM
RW_WIDTH = 512
LORA_DECAY = 64
LORA_AAA = 64
LORA_GATE = 160
LORA_WIDTH = LORA_DECAY + LORA_AAA + LORA_GATE
LORA_PAD = 384
RMS_EPS = 1e-6
GN_EPS = 64e-5
DECAY_SCALE = math.exp(-0.5)
KEY_BLOCK = 128
SB_DEAD_LOG = -104.0
SB_WINDOW = 3
SB_QUERY_BLOCKS = 4
CHUNK = 64
INVERSE_RESID_MAX = 0.02
VMEM_LIMIT = 56 * 1024 * 1024
PROJ_ROWS = 1024
FFN_ROWS = 512


def _dot(a, b):
    return jnp.dot(a, b, preferred_element_type=F32)


def _split2(x):
    hi = x.astype(BF16)
    lo = (x - hi.astype(F32)).astype(BF16)
    return hi, lo


def _bdot(a, b):
    return jnp.einsum("bij,bjk->bik", a, b, preferred_element_type=F32)


def _bdot_nt(a, b):
    return jnp.einsum("bik,bjk->bij", a, b, preferred_element_type=F32)


def _bmm1(a, b):
    return _bdot(a.astype(BF16), b.astype(BF16))


def _bmm3(a, b):
    ah, al = _split2(a)
    bh, bl = _split2(b)
    return _bdot(ah, bh) + (_bdot(ah, bl) + _bdot(al, bh))


def _mm_exact_rhs(x, m, parts=2, transposed=False):
    contract = (((1,), (1 if transposed else 0,)), ((), ()))
    acc = None
    for _ in range(parts):
        part = x.astype(BF16)
        prod = lax.dot_general(part, m, contract, preferred_element_type=F32)
        acc = prod if acc is None else acc + prod
        x = x - part.astype(F32)
    return acc


def _sigmoid(x):
    return 0.5 * jnp.tanh(0.5 * x) + 0.5


def _rmsnorm(x, g):
    return x * lax.rsqrt(jnp.mean(x * x, axis=-1, keepdims=True) + RMS_EPS) * g


def _const_spec(shape):
    return pl.BlockSpec(shape, lambda *_: (0,) * len(shape), pipeline_mode=pl.Buffered(1))


def _norm_proj_kernel(x_ref, g_ref, wq_ref, wkv_ref, wrkv_ref, wl_ref,
                      q_ref, k_ref, v_ref, rkv_ref, l_ref, *, feature_major_kv):
    hb = _rmsnorm(x_ref[...], g_ref[...]).astype(BF16)
    q_ref[...] = _dot(hb, wq_ref[...])
    if feature_major_kv:
        kv = lax.dot_general(wkv_ref[...], hb, (((1,), (1,)), ((), ())), preferred_element_type=F32)
        k_ref[0] = kv[:SB_WIDTH]
        v_ref[0] = kv[SB_WIDTH:]
    else:
        kv = _dot(hb, wkv_ref[...])
        k_ref[...] = kv[:, :SB_WIDTH]
        v_ref[...] = kv[:, SB_WIDTH:]
    rkv_ref[...] = _dot(hb, wrkv_ref[...])
    l_ref[...] = _dot(hb, wl_ref[...])


def _norm_proj(x, g, wq, wkv, wrkv, wl, tm, feature_major_kv):
    b, t, d = x.shape
    m = b * t
    row = lambda w: pl.BlockSpec((tm, w), lambda i: (i, 0))
    if feature_major_kv:
        per_batch = t // tm
        kv_spec = pl.BlockSpec((1, SB_WIDTH, tm), lambda i: (i // per_batch, 0, i % per_batch))
        kv_shape = jax.ShapeDtypeStruct((b, SB_WIDTH, t), F32)
    else:
        kv_spec, kv_shape = row(SB_WIDTH), jax.ShapeDtypeStruct((m, SB_WIDTH), F32)
    return pl.pallas_call(
        functools.partial(_norm_proj_kernel, feature_major_kv=feature_major_kv),
        grid=(m // tm,),
        in_specs=[row(d), _const_spec((1, d)), _const_spec(wq.shape), _const_spec(wkv.shape),
                  _const_spec(wrkv.shape), _const_spec(wl.shape)],
        out_specs=[row(SB_WIDTH), kv_spec, kv_spec, row(3 * RW_WIDTH), row(LORA_PAD)],
        out_shape=[jax.ShapeDtypeStruct((m, SB_WIDTH), F32), kv_shape, kv_shape,
                   jax.ShapeDtypeStruct((m, 3 * RW_WIDTH), F32), jax.ShapeDtypeStruct((m, LORA_PAD), F32)],
        compiler_params=pltpu.CompilerParams(dimension_semantics=("arbitrary",),
                                             vmem_limit_bytes=VMEM_LIMIT),
        name="norm_proj",
    )(x.reshape(m, d), g, wq, wkv, wrkv, wl)


def _sb_kernel(q_ref, *refs, tq, q_blocks, own_in_prefix, n_prefix):
    if own_in_prefix:
        kp_ref, vp_ref, o_ref = refs
    else:
        kd_ref, vd_ref, kp_ref, vp_ref, o_ref = refs
    pairs = SB_WIDTH // LANES
    by_pair = lambda x: jnp.stack([x[:, p * LANES:(p + 1) * LANES] for p in range(pairs)])
    q_all = q_ref[0] * (1.0 / math.sqrt(HEAD_DIM))
    q = jnp.concatenate([by_pair(q_all[j * tq:(j + 1) * tq]) for j in range(q_blocks)], axis=0)
    first = lax.broadcasted_iota(jnp.int32, (pairs, tq, LANES), 2) < HEAD_DIM
    first_all = jnp.concatenate([first] * q_blocks, axis=0)
    qe = jnp.concatenate([jnp.where(first_all, q, 0.0), jnp.where(first_all, 0.0, q)], axis=1).astype(BF16)
    entries = lambda x, j: x[j * pairs:(j + 1) * pairs]

    row = lax.broadcasted_iota(jnp.int32, (pairs, 2 * tq, KEY_BLOCK), 1)
    col = lax.broadcasted_iota(jnp.int32, (pairs, 2 * tq, KEY_BLOCK), 2)
    causal = col < jnp.where(row >= tq, row - tq, row)

    jj = lax.broadcasted_iota(jnp.int32, (KEY_BLOCK, 2 * KEY_BLOCK), 0)
    ss = lax.broadcasted_iota(jnp.int32, (KEY_BLOCK, 2 * KEY_BLOCK), 1)
    suffix = jnp.where((jj > ss) | (ss >= KEY_BLOCK), 1.0, 0.0).astype(BF16)
    suffix2 = jnp.concatenate([suffix, suffix], axis=0)

    def prefix(ref, start, n_keys):
        return ref[0, :, pl.ds(start, n_keys)].reshape(pairs, LANES, n_keys)

    def suffix_sums(log1mb):
        n = log1mb.shape[0]
        hi_lo = jnp.concatenate(_split2(log1mb.reshape(n * 2 * tq, KEY_BLOCK)), axis=1)
        return _dot(hi_lo, suffix2).reshape(n, 2 * tq, 2 * KEY_BLOCK)

    def softplus(z):
        return jnp.maximum(z, 0.0) + jnp.log(1.0 + jnp.exp(-jnp.abs(z)))

    def own_block(qe, kb, vb, c, o, mask):
        z = _bdot_nt(qe, kb.astype(BF16))
        log1mb = jnp.where(mask, -softplus(z), 0.0)
        sums = suffix_sums(log1mb)
        w = jnp.where(mask, jnp.exp(z + log1mb + (sums[:, :, :KEY_BLOCK] + c)), 0.0)
        return c + sums[:, :, KEY_BLOCK:], o + _bdot(w.astype(BF16), vb.astype(BF16))

    def sweep(qe, kb, vb, c, o, mask=None):
        z = _bdot(qe, kb.astype(BF16))
        log1mb = -softplus(z)
        if mask is not None:
            log1mb = jnp.where(mask, log1mb, 0.0)
        n_blocks = kb.shape[2] // KEY_BLOCK
        after = [None] * n_blocks
        for j in reversed(range(n_blocks)):
            sums = suffix_sums(log1mb[:, :, j * KEY_BLOCK:(j + 1) * KEY_BLOCK])
            after[j] = sums[:, :, :KEY_BLOCK] + c
            c = c + sums[:, :, KEY_BLOCK:]
        w = jnp.exp(z + log1mb + jnp.concatenate(after, axis=2))
        if mask is not None:
            w = jnp.where(mask, w, 0.0)
        return c, o + _bdot_nt(w.astype(BF16), vb.astype(BF16))

    alive = lambda c: (jnp.max(c) >= SB_DEAD_LOG).astype(jnp.int32)
    zeros = jnp.zeros((q_blocks * pairs, 2 * tq, LANES), F32)
    if own_in_prefix:
        wrow = lax.broadcasted_iota(jnp.int32, (pairs, 2 * tq, SB_WINDOW * KEY_BLOCK), 1)
        wcol = lax.broadcasted_iota(jnp.int32, (pairs, 2 * tq, SB_WINDOW * KEY_BLOCK), 2)
        wrow = jnp.where(wrow >= tq, wrow - tq, wrow)
        older, keys, values, visible = [], [], [], []
        for j in range(q_blocks):
            i = pl.program_id(1) * q_blocks + j
            n = jnp.maximum(i - (SB_WINDOW - 1), 0)
            start = pl.multiple_of(n * KEY_BLOCK, KEY_BLOCK)
            older.append(n)
            keys.append(prefix(kp_ref, start, SB_WINDOW * KEY_BLOCK))
            values.append(prefix(vp_ref, start, SB_WINDOW * KEY_BLOCK))
            visible.append(start + wcol < i * tq + wrow)
        c, o = sweep(qe, jnp.concatenate(keys, axis=0), jnp.concatenate(values, axis=0), zeros, zeros,
                     jnp.concatenate(visible, axis=0))
        still = [alive(entries(c, j)) for j in range(q_blocks)]
    else:
        c, o = own_block(qe, by_pair(kd_ref[0]), by_pair(vd_ref[0]), zeros, zeros, causal)
        older = [n_prefix]
        still = [jnp.int32(1)]

    outs = []
    for j in range(q_blocks):
        n, qe_j = older[j], entries(qe, j)

        def live(carry, n=n):
            return (carry[0] < (n >> 1)) & (carry[1] > 0)

        def body(carry, n=n, qe_j=qe_j):
            it, _, c, o = carry
            start = pl.multiple_of((n - 2 - 2 * it) * KEY_BLOCK, KEY_BLOCK)
            c, o = sweep(qe_j, prefix(kp_ref, start, 2 * KEY_BLOCK), prefix(vp_ref, start, 2 * KEY_BLOCK), c, o)
            return it + 1, alive(c), c, o

        _, still_j, c_j, o_j = lax.while_loop(live, body, (jnp.int32(0), still[j], entries(c, j), entries(o, j)))
        _, o_j = lax.cond(((n & 1) == 1) & (still_j > 0),
                          lambda c, o, qe_j=qe_j: sweep(qe_j, prefix(kp_ref, 0, KEY_BLOCK),
                                                        prefix(vp_ref, 0, KEY_BLOCK), c, o),
                          lambda c, o: (c, o), c_j, o_j)
        o_j = jnp.where(first, o_j[:, :tq], o_j[:, tq:])
        outs.append(jnp.concatenate([o_j[p] for p in range(pairs)], axis=1))
    o_ref[0] = jnp.concatenate(outs, axis=0)


def _sb_attention(q, kp, vp, own=None, *, tq):
    b, t, _ = q.shape
    tp = kp.shape[2]
    blk = lambda rows: pl.BlockSpec((1, rows, SB_WIDTH), lambda bi, i: (bi, i, 0))
    pre = pl.BlockSpec((1,) + kp.shape[1:], lambda bi, i: (bi, 0, 0))
    if own is None:
        assert tq == KEY_BLOCK and tp == t and t >= SB_WINDOW * KEY_BLOCK
        q_blocks = SB_QUERY_BLOCKS if t % (SB_QUERY_BLOCKS * tq) == 0 else 1
        in_specs, args = [blk(q_blocks * tq), pre, pre], (q, kp, vp)
    else:
        q_blocks = 1
        in_specs, args = [blk(tq), blk(KEY_BLOCK), blk(KEY_BLOCK), pre, pre], (q,) + tuple(own) + (kp, vp)
    return pl.pallas_call(
        functools.partial(_sb_kernel, tq=tq, q_blocks=q_blocks, own_in_prefix=own is None,
                          n_prefix=tp // KEY_BLOCK),
        grid=(b, t // (q_blocks * tq)),
        in_specs=in_specs,
        out_specs=blk(q_blocks * tq),
        out_shape=jax.ShapeDtypeStruct(q.shape, F32),
        compiler_params=pltpu.CompilerParams(dimension_semantics=("arbitrary",) * 2,
                                             vmem_limit_bytes=VMEM_LIMIT),
        name="stickbreak",
    )(*args)


def _rwkv_kernel(x_ref, l_ref, sx_ref, sl_ref, s0_ref, mux_ref, mul_ref, w0_ref, wd_ref, a0_ref, wa_ref,
                 wg_ref, kkw_ref, kaw_ref, rkw_ref, lnw_ref, lnb_ref,
                 o_ref, sout_ref, ok_ref,
                 px_s, pl_s, y_s, z_s, qe_s, yl_s, phi_s, psi_s,
                 *, nbat, tb, t_valid, robust):
    t = pl.program_id(0)
    pairs = RW_WIDTH // LANES
    rows_all = nbat * tb
    pr = lax.broadcasted_iota(jnp.int32, (HEAD_DIM, LANES), 0)
    pc = lax.broadcasted_iota(jnp.int32, (HEAD_DIM, LANES), 1)
    to_low = jnp.where(pc == pr, 1.0, 0.0).astype(BF16)
    to_high = jnp.where(pc == pr + HEAD_DIM, 1.0, 0.0).astype(BF16)

    @pl.when(t == 0)
    def _():
        px_s[...] = sx_ref[...]
        pl_s[...] = sl_ref[...]
        for p in range(pairs):
            for bi in range(nbat):
                z_s[p * nbat + bi] = jnp.concatenate(
                    [_mm_exact_rhs(s0_ref[bi, 2 * p], to_low, 3),
                     _mm_exact_rhs(s0_ref[bi, 2 * p + 1], to_high, 3)], axis=0).T

    def shift(x_ref, carry_s, mu_ref):
        x = x_ref[...]
        prev = pltpu.roll(x.reshape(rows_all, x.shape[2]), 1, 0).reshape(x.shape)
        prev = jnp.where(lax.broadcasted_iota(jnp.int32, x.shape, 1) == 0, carry_s[...], prev)
        carry_s[...] = x[:, tb - 1:tb, :]
        return (x + (prev - x) * mu_ref[...]).reshape(rows_all, x.shape[2])

    xs = shift(x_ref, px_s, mux_ref)
    xr, xk, xv = xs[:, :RW_WIDTH], xs[:, RW_WIDTH:2 * RW_WIDTH], xs[:, 2 * RW_WIDTH:]
    xl = shift(l_ref, pl_s, mul_ref)

    hr = lax.broadcasted_iota(jnp.int32, (LANES, LANES), 0) // HEAD_DIM
    hc = lax.broadcasted_iota(jnp.int32, (LANES, LANES), 1) // HEAD_DIM
    head_ones = jnp.where(hr == hc, 1.0, 0.0).astype(BF16)
    head_sum = lambda x, parts=2: jnp.concatenate(
        [_mm_exact_rhs(x[:, p * LANES:(p + 1) * LANES], head_ones, parts) for p in range(pairs)], axis=1)

    lora = lambda x, w_ref, lo, hi: _dot(x.astype(BF16), w_ref[lo:hi, :].astype(BF16))
    log_w = -DECAY_SCALE * _sigmoid(w0_ref[...] + lora(jnp.tanh(xl[:, :LANES]), wd_ref, 0, LANES))
    a = _sigmoid(a0_ref[...] + lora(xl[:, :LANES], wa_ref, 0, LANES))
    kk = xk * kkw_ref[...]
    kk = kk * lax.rsqrt(jnp.maximum(head_sum(kk * kk, 1), 1e-24))
    k2 = xk * (1.0 + (a - 1.0) * kaw_ref[...])
    b = kk * a
    if t_valid < tb:
        valid = lax.broadcasted_iota(jnp.int32, (nbat, tb, RW_WIDTH), 1).reshape(rows_all, RW_WIDTH) < t_valid
        log_w = jnp.where(valid, log_w, 0.0)
        kk = jnp.where(valid, kk, 0.0)
        b = jnp.where(valid, b, 0.0)
        k2 = jnp.where(valid, k2, 0.0)
        xv = jnp.where(valid, xv, 0.0)
    nc = tb // CHUNK
    nb = pairs * nbat * nc
    e2 = 2 * CHUNK
    by_chunk = lambda x: jnp.stack(
        [x[:, p * LANES:(p + 1) * LANES] for p in range(pairs)]).reshape(nb, CHUNK, LANES)
    ti = lax.broadcasted_iota(jnp.int32, (nb, CHUNK, CHUNK), 1)
    tj = lax.broadcasted_iota(jnp.int32, (nb, CHUNK, CHUNK), 2)
    tri = jnp.where(tj <= ti, 1.0, 0.0).astype(BF16)
    lw3 = by_chunk(log_w)
    lw_hi, lw_lo = _split2(lw3)
    cum3 = _bdot(tri, lw_hi) + _bdot(tri, lw_lo)
    tot = cum3[:, CHUNK - 1:CHUNK, :]
    first = lax.broadcasted_iota(jnp.int32, (nb, CHUNK, LANES), 2) < HEAD_DIM
    expand = lambda x: jnp.concatenate([jnp.where(first, x, 0.0), jnp.where(first, 0.0, x)], axis=1)
    e_neg = jnp.exp(-cum3)
    e_tail = jnp.exp(tot - cum3)
    kk3, b3, k3, v3 = by_chunk(kk), by_chunk(b), by_chunk(k2), by_chunk(xv)
    kkt = kk3 * jnp.exp(cum3 - lw3)
    rt = by_chunk(xr) * jnp.exp(cum3)

    wr = lax.broadcasted_iota(jnp.int32, (nb, CHUNK, e2), 1)
    wc = lax.broadcasted_iota(jnp.int32, (nb, CHUNK, e2), 2)
    head0 = wc < CHUNK
    ws = jnp.where(head0, wc, wc - CHUNK)
    strict, incl, eye_w = ws < wr, ws <= wr, ws == wr
    diag2 = lambda x: jnp.concatenate([jnp.where(head0, x, 0.0), jnp.where(head0, 0.0, x)], axis=1)

    gram = _bdot_nt(jnp.concatenate([kkt, rt], axis=1).astype(BF16),
                    jnp.concatenate([expand(k3 * e_neg), expand(b3 * e_neg)], axis=1).astype(BF16))
    ak = jnp.where(strict, gram[:, :CHUNK, :e2], 0.0)
    ab = jnp.where(strict, gram[:, :CHUNK, e2:], 0.0)
    rk = jnp.where(incl, gram[:, CHUNK:, :e2], 0.0)
    rb = jnp.where(incl, gram[:, CHUNK:, e2:], 0.0)

    eye2 = jnp.where(eye_w, 1.0, 0.0)
    ab_d = diag2(ab)

    def inverse(product):
        tinv = eye2 - ab
        pw = product(ab, ab_d)
        n_sq = CHUNK.bit_length() - 2
        for s in range(n_sq):
            if s + 1 < n_sq:
                both = product(pw, jnp.concatenate([diag2(tinv), diag2(pw)], axis=2))
                tinv = tinv + both[:, :, :e2]
                pw = both[:, :, e2:]
            else:
                tinv = tinv + product(pw, diag2(tinv))
        return tinv, eye2 - tinv - _bmm3(ab, diag2(tinv))

    ve = expand(v3)
    akve = _bmm1(ak, ve)
    per_pair = lambda x: x.reshape((pairs * nbat, nc) + x.shape[1:])
    zr = lax.broadcasted_iota(jnp.int32, (nb, HEAD_DIM, LANES), 1)
    zc = lax.broadcasted_iota(jnp.int32, (nb, HEAD_DIM, LANES), 2)
    zhead0 = zc < HEAD_DIM
    own_block = lambda x: jnp.where(zhead0, x[:, :HEAD_DIM], x[:, HEAD_DIM:])
    decay = jnp.where(jnp.where(zhead0, zc, zc - HEAD_DIM) == zr, jnp.exp(tot), 0.0)

    def build_operators(product):
        tinv, resid = inverse(product)
        row_ok = jnp.sum(jnp.abs(resid), axis=2, keepdims=True) <= INVERSE_RESID_MAX
        tinv = tinv + _bmm1(tinv, diag2(resid))
        wu = _bmm1(tinv, jnp.concatenate([expand(kkt), expand(akve)], axis=2))
        w_, u_loc = wu[:, :, :LANES], wu[:, :, LANES:]
        yq = _bmm1(jnp.concatenate([rk, -rb], axis=2),
                   jnp.concatenate([jnp.concatenate([ve, jnp.zeros_like(ve)], axis=2),
                                    jnp.concatenate([expand(u_loc), expand(w_)], axis=2)], axis=1))
        qe_s[...] = per_pair(rt + yq[:, :, LANES:])
        yl_s[...] = per_pair(yq[:, :, :LANES])
        zp = _bmm1(jnp.swapaxes(jnp.concatenate([k3 * e_tail, -(b3 * e_tail)], axis=1), 1, 2),
                   jnp.concatenate([jnp.concatenate([v3, jnp.zeros_like(v3)], axis=2),
                                    jnp.concatenate([u_loc, w_], axis=2)], axis=1))
        phi_s[...] = per_pair(decay + own_block(zp[:, :, LANES:]))
        psi_s[...] = per_pair(own_block(zp[:, :, :LANES]))
        return jnp.min(row_ok.astype(jnp.int32))

    step_ok = build_operators(_bmm3 if robust else _bmm1)

    @pl.when(t == 0)
    def _():
        ok_ref[...] = jnp.ones(ok_ref.shape, jnp.int32)

    ok_ref[...] = jnp.minimum(ok_ref[...], step_ok)

    shead0 = lax.broadcasted_iota(jnp.int32, (pairs * nbat, HEAD_DIM, LANES), 2) < HEAD_DIM

    def advance(c, _):
        rows = pl.ds(pl.multiple_of(c * CHUNK, CHUNK), CHUNK)
        z = z_s[...]
        y_s[:, rows, :] = _bmm1(qe_s[:, c], z) + yl_s[:, c]
        z_new = _bmm3(phi_s[:, c], z) + psi_s[:, c]
        z_s[...] = jnp.concatenate([jnp.where(shead0, z_new, 0.0), jnp.where(shead0, 0.0, z_new)], axis=1)
        return 0

    lax.fori_loop(0, nc, advance, 0, unroll=min(nc, 2))

    y = jnp.concatenate([y_s[p * nbat:(p + 1) * nbat].reshape(rows_all, LANES) for p in range(pairs)], axis=1)
    d = y - head_sum(y) * (1.0 / HEAD_DIM)
    var = head_sum(d * d) * (1.0 / HEAD_DIM)
    o = d * lax.rsqrt(var + GN_EPS) * lnw_ref[...] + lnb_ref[...]
    bonus = head_sum(xr * k2 * rkw_ref[...]) * xv
    gate = lora(_sigmoid(xl[:, LANES:]), wg_ref, LANES, LORA_PAD)
    o_ref[...] = ((o + bonus) * gate).reshape(nbat, tb, RW_WIDTH)

    @pl.when(t == pl.num_programs(0) - 1)
    def _():
        for p in range(pairs):
            for bi in range(nbat):
                s_pair = z_s[p * nbat + bi].T
                sout_ref[bi, 2 * p] = s_pair[:HEAD_DIM, :HEAD_DIM]
                sout_ref[bi, 2 * p + 1] = _mm_exact_rhs(s_pair[HEAD_DIM:], to_high, 3, transposed=True)


def _rwkv(rkv, lora, shift_rkv, shift_lora, z0, prm, *, tb, t_valid, robust=False):
    b, t, _ = rkv.shape
    nt = t // tb
    nc = tb // CHUNK
    pairs = RW_WIDTH // LANES
    rows = lambda w: pl.BlockSpec((b, tb, w), lambda ti: (0, ti, 0))
    first = lambda w: pl.BlockSpec((b, 1, w), lambda ti: (0, 0, 0))
    state = pl.BlockSpec((b, 2 * pairs, HEAD_DIM, HEAD_DIM), lambda ti: (0, 0, 0, 0))
    vec = _const_spec((1, RW_WIDTH))
    mat = _const_spec((LORA_PAD, RW_WIDTH))
    return pl.pallas_call(
        functools.partial(_rwkv_kernel, nbat=b, tb=tb, t_valid=t_valid, robust=robust),
        grid=(nt,),
        in_specs=[rows(3 * RW_WIDTH), rows(LORA_PAD), first(3 * RW_WIDTH), first(LORA_PAD), state,
                  _const_spec((1, 3 * RW_WIDTH)), _const_spec((1, LORA_PAD)),
                  vec, mat, vec, mat, mat, vec, vec, vec, vec, vec],
        out_specs=[rows(RW_WIDTH), state, pl.BlockSpec((SUBLANES, LANES), lambda ti: (0, 0))],
        out_shape=[jax.ShapeDtypeStruct((b, t, RW_WIDTH), F32),
                   jax.ShapeDtypeStruct((b, 2 * pairs, HEAD_DIM, HEAD_DIM), F32),
                   jax.ShapeDtypeStruct((SUBLANES, LANES), jnp.int32)],
        scratch_shapes=[pltpu.VMEM((b, 1, 3 * RW_WIDTH), F32), pltpu.VMEM((b, 1, LORA_PAD), F32),
                        pltpu.VMEM((pairs * b, tb, LANES), F32), pltpu.VMEM((pairs * b, LANES, LANES), F32)]
        + [pltpu.VMEM((pairs * b, nc, CHUNK, LANES), F32)] * 2
        + [pltpu.VMEM((pairs * b, nc, HEAD_DIM, LANES), F32)] * 2,
        compiler_params=pltpu.CompilerParams(dimension_semantics=("arbitrary",),
                                             vmem_limit_bytes=VMEM_LIMIT),
        name="rwkv7",
    )(rkv, lora, shift_rkv, shift_lora, z0, prm["mu_rkv"], prm["mu_lora"],
      prm["w0"], prm["wd"], prm["a0"], prm["wa"], prm["wg"],
      prm["k_k"], prm["k_a"], prm["r_k"], prm["ln_w"], prm["ln_b"])


def _out_ffn_kernel(x_ref, sb_ref, rw_ref, woa_ref, wob_ref, gf_ref, wg_ref, wu_ref, wd_ref, gl_ref,
                    y_ref):
    x1 = x_ref[...] + _dot(sb_ref[...].astype(BF16), woa_ref[...]) \
        + _dot(rw_ref[...].astype(BF16), wob_ref[...])
    h2 = _rmsnorm(x1, gf_ref[...]).astype(BF16)
    gate = _dot(h2, wg_ref[...])
    act = gate * _sigmoid(gate) * _dot(h2, wu_ref[...])
    x2 = x1 + _dot(act.astype(BF16), wd_ref[...])
    y_ref[...] = _rmsnorm(x2, gl_ref[...])


def _out_ffn(x2d, o_sb, o_rw, woa, wob, gf, wg, wu, wd, gl, tm):
    m, d = x2d.shape
    row = lambda w: pl.BlockSpec((tm, w), lambda i: (i, 0))
    return pl.pallas_call(
        _out_ffn_kernel,
        grid=(m // tm,),
        in_specs=[row(d), row(SB_WIDTH), row(RW_WIDTH), _const_spec(woa.shape), _const_spec(wob.shape),
                  _const_spec((1, d)), _const_spec(wg.shape), _const_spec(wu.shape),
                  _const_spec(wd.shape), _const_spec((1, d))],
        out_specs=row(d),
        out_shape=jax.ShapeDtypeStruct((m, d), F32),
        compiler_params=pltpu.CompilerParams(dimension_semantics=("arbitrary",),
                                             vmem_limit_bytes=VMEM_LIMIT),
        name="out_ffn",
    )(x2d, o_sb, o_rw, woa, wob, gf, wg, wu, wd, gl)


def _pad_rows(x, rows):
    return jnp.pad(x, ((0, 0), (0, rows - x.shape[1]), (0, 0)))


def _pad_lanes(x, lanes):
    return jnp.pad(x, [(0, 0)] * (x.ndim - 1) + [(0, lanes - x.shape[-1])])


def _tile_sizes(b, t, feature_major_kv):
    rows = b * t
    return dict(tm_proj=math.gcd(t, PROJ_ROWS) if feature_major_kv else min(rows, PROJ_ROWS),
                tm_ffn=min(rows, FFN_ROWS), tq=min(t, KEY_BLOCK), tb=CHUNK * min(2, -(-t // CHUNK)))


def _layer(x, k_past, v_past, wkv0, shift0, w):
    b, t, d = x.shape
    tiles = _tile_sizes(b, t, feature_major_kv=k_past is None)
    tq, tb = tiles["tq"], tiles["tb"]
    x2d = x.reshape(b * t, d)
    to_feature_major = lambda a: jnp.transpose(a, (0, 2, 3, 1)).reshape(b, SB_WIDTH, a.shape[1])
    from_feature_major = lambda a: jnp.transpose(a.reshape(b, SB_HEADS, HEAD_DIM, a.shape[2]), (0, 3, 1, 2))
    if k_past is None:
        q, k, v, rkv, lora = _norm_proj(x, w["norm_mix_g"], w["w_q"], w["w_kv_t"], w["w_rkv"], w["w_lora"],
                                        tiles["tm_proj"], feature_major_kv=True)
        o_sb = _sb_attention(q.reshape(b, t, SB_WIDTH), k, v, tq=tq)
        k_heads, v_heads = from_feature_major(k), from_feature_major(v)
    else:
        q, k, v, rkv, lora = _norm_proj(x, w["norm_mix_g"], w["w_q"], w["w_kv"], w["w_rkv"], w["w_lora"],
                                        tiles["tm_proj"], feature_major_kv=False)
        q3, k3, v3 = (a.reshape(b, t, SB_WIDTH) for a in (q, k, v))
        o_sb = _sb_attention(q3, to_feature_major(k_past), to_feature_major(v_past),
                             (_pad_rows(k3, KEY_BLOCK), _pad_rows(v3, KEY_BLOCK)), tq=tq)
        k_heads, v_heads = (a.reshape(b, t, SB_HEADS, HEAD_DIM) for a in (k, v))

    rkv3 = rkv.reshape(b, t, 3 * RW_WIDTH)
    lora3 = lora.reshape(b, t, LORA_PAD)
    t_pad = -(-t // tb) * tb
    rwkv_args = (_pad_rows(rkv3, t_pad), _pad_rows(lora3, t_pad), shift0[..., :3 * RW_WIDTH],
                 _pad_lanes(shift0[..., 3 * RW_WIDTH:], LORA_PAD), wkv0, w)
    o_rw, wkv_out, inverse_ok = _rwkv(*rwkv_args, tb=tb, t_valid=min(t, tb))
    o_rw, wkv_out = lax.cond(inverse_ok[0, 0] == 1, lambda: (o_rw, wkv_out),
                             lambda: _rwkv(*rwkv_args, tb=tb, t_valid=min(t, tb), robust=True)[:2])
    o_rw = o_rw[:, :t]
    shift_last = jnp.concatenate([rkv3[:, t - 1:, :], lora3[:, t - 1:, :LORA_WIDTH]], axis=-1)

    y = _out_ffn(x2d, o_sb.reshape(b * t, SB_WIDTH), o_rw.reshape(b * t, RW_WIDTH),
                 w["w_out_sb"], w["w_out_rw"], w["norm_ffn_g"], w["w_gate"], w["w_up"], w["w_down"],
                 w["norm_final_g"], tiles["tm_ffn"])
    return y.reshape(b, t, d), k_heads, v_heads, wkv_out, shift_last


def kernel(x_prompt, x_sample, cache_k, cache_v, state_wkv, state_shift, norm_mix_g, w_in, mu_shift, w0,
           w_decay_up, a0, w_aaa_up, w_gate_up, k_k, k_a, r_k, ln_x_w, ln_x_b, w_out, norm_ffn_g, w_gate,
           w_up, w_down, norm_final_g):
    assert w_in.shape[0] == 1, "single-layer trunk"
    l = 0
    rw3 = 3 * RW_WIDTH
    lora_rows = lambda m, lo, hi: jnp.pad(m, ((lo, LORA_PAD - hi), (0, 0)))
    w = {
        "norm_mix_g": norm_mix_g[l][None, :],
        "w_q": w_in[l][:, :SB_WIDTH].astype(BF16),
        "w_kv": w_in[l][:, SB_WIDTH:3 * SB_WIDTH].astype(BF16),
        "w_kv_t": w_in[l][:, SB_WIDTH:3 * SB_WIDTH].T.astype(BF16),
        "w_rkv": w_in[l][:, 3 * SB_WIDTH:3 * SB_WIDTH + rw3].astype(BF16),
        "w_lora": _pad_lanes(w_in[l][:, 3 * SB_WIDTH + rw3:], LORA_PAD).astype(BF16),
        "mu_rkv": mu_shift[l][None, :rw3],
        "mu_lora": _pad_lanes(mu_shift[l][None, rw3:], LORA_PAD),
        "w0": w0[l][None, :],
        "wd": lora_rows(w_decay_up[l], 0, LORA_DECAY),
        "a0": a0[l][None, :],
        "wa": lora_rows(w_aaa_up[l], LORA_DECAY, LORA_DECAY + LORA_AAA),
        "wg": lora_rows(w_gate_up[l], LORA_DECAY + LORA_AAA, LORA_WIDTH),
        "k_k": k_k[l][None, :],
        "k_a": k_a[l][None, :],
        "r_k": r_k[l].reshape(1, RW_WIDTH),
        "ln_w": ln_x_w[l][None, :],
        "ln_b": ln_x_b[l][None, :],
        "w_out_sb": w_out[l][:SB_WIDTH].astype(BF16),
        "w_out_rw": w_out[l][SB_WIDTH:].astype(BF16),
        "norm_ffn_g": norm_ffn_g[l][None, :],
        "w_gate": w_gate[l].astype(BF16),
        "w_up": w_up[l].astype(BF16),
        "w_down": w_down[l].astype(BF16),
        "norm_final_g": norm_final_g[None, :],
    }
    bp = x_prompt.shape[0]
    heads = RW_WIDTH // HEAD_DIM
    wkv_zero = jnp.zeros((bp, heads, HEAD_DIM, HEAD_DIM), x_prompt.dtype)
    shift_zero = jnp.zeros((bp, 1, rw3 + LORA_WIDTH), x_prompt.dtype)
    ys, k2, v2, s2, sh2 = _layer(x_sample, cache_k[l], cache_v[l], state_wkv[l], state_shift[l], w)
    yp, k1, v1, s1, sh1 = _layer(x_prompt, None, None, wkv_zero, shift_zero, w)
    return (yp, ys, k1[None], v1[None], s1[None], sh1[None], k2[None], v2[None], s2[None], sh2[None])
```

```python
import functools
import math

import jax
import jax.numpy as jnp
from jax import lax
from jax.experimental import pallas as pl
from jax.experimental.pallas import tpu as pltpu

F32 = jnp.float32
BF16 = jnp.bfloat16

LANES = 128
SUBLANES = 8
HEAD_DIM = 64
SB_WIDTH = 512
SB_HEADS = SB_WIDTH // HEAD_DIM
RW_WIDTH = 512
LORA_DECAY = 64
LORA_AAA = 64
LORA_GATE = 160
LORA_WIDTH = LORA_DECAY + LORA_AAA + LORA_GATE
LORA_PAD = 384
RMS_EPS = 1e-6
GN_EPS = 64e-5
DECAY_SCALE = math.exp(-0.5)
KEY_BLOCK = 128
SB_DEAD_LOG = -104.0
SB_WINDOW = 3
SB_QUERY_BLOCKS = 4
CHUNK = 64
INVERSE_RESID_MAX = 0.02
VMEM_LIMIT = 56 * 1024 * 1024
PROJ_ROWS = 1024
FFN_ROWS = 512


def _dot(a, b):
    return jnp.dot(a, b, preferred_element_type=F32)


def _split2(x):
    hi = x.astype(BF16)
    lo = (x - hi.astype(F32)).astype(BF16)
    return hi, lo


def _bdot(a, b):
    return jnp.einsum("bij,bjk->bik", a, b, preferred_element_type=F32)


def _bdot_nt(a, b):
    return jnp.einsum("bik,bjk->bij", a, b, preferred_element_type=F32)


def _bmm1(a, b):
    return _bdot(a.astype(BF16), b.astype(BF16))


def _bmm3(a, b):
    ah, al = _split2(a)
    bh, bl = _split2(b)
    return _bdot(ah, bh) + (_bdot(ah, bl) + _bdot(al, bh))


def _mm_exact_rhs(x, m, parts=2, transposed=False):
    contract = (((1,), (1 if transposed else 0,)), ((), ()))
    acc = None
    for _ in range(parts):
        part = x.astype(BF16)
        prod = lax.dot_general(part, m, contract, preferred_element_type=F32)
        acc = prod if acc is None else acc + prod
        x = x - part.astype(F32)
    return acc


def _sigmoid(x):
    return 0.5 * jnp.tanh(0.5 * x) + 0.5


def _rmsnorm(x, g):
    return x * lax.rsqrt(jnp.mean(x * x, axis=-1, keepdims=True) + RMS_EPS) * g


def _const_spec(shape):
    return pl.BlockSpec(shape, lambda *_: (0,) * len(shape), pipeline_mode=pl.Buffered(1))


def _norm_proj_kernel(x_ref, g_ref, wq_ref, wkv_ref, wrkv_ref, wl_ref,
                      q_ref, k_ref, v_ref, rkv_ref, l_ref, *, feature_major_kv):
    hb = _rmsnorm(x_ref[...], g_ref[...]).astype(BF16)
    q_ref[...] = _dot(hb, wq_ref[...])
    if feature_major_kv:
        kv = lax.dot_general(wkv_ref[...], hb, (((1,), (1,)), ((), ())), preferred_element_type=F32)
        k_ref[0] = kv[:SB_WIDTH]
        v_ref[0] = kv[SB_WIDTH:]
    else:
        kv = _dot(hb, wkv_ref[...])
        k_ref[...] = kv[:, :SB_WIDTH]
        v_ref[...] = kv[:, SB_WIDTH:]
    rkv_ref[...] = _dot(hb, wrkv_ref[...])
    l_ref[...] = _dot(hb, wl_ref[...])


def _norm_proj(x, g, wq, wkv, wrkv, wl, tm, feature_major_kv):
    b, t, d = x.shape
    m = b * t
    row = lambda w: pl.BlockSpec((tm, w), lambda i: (i, 0))
    if feature_major_kv:
        per_batch = t // tm
        kv_spec = pl.BlockSpec((1, SB_WIDTH, tm), lambda i: (i // per_batch, 0, i % per_batch))
        kv_shape = jax.ShapeDtypeStruct((b, SB_WIDTH, t), F32)
    else:
        kv_spec, kv_shape = row(SB_WIDTH), jax.ShapeDtypeStruct((m, SB_WIDTH), F32)
    return pl.pallas_call(
        functools.partial(_norm_proj_kernel, feature_major_kv=feature_major_kv),
        grid=(m // tm,),
        in_specs=[row(d), _const_spec((1, d)), _const_spec(wq.shape), _const_spec(wkv.shape),
                  _const_spec(wrkv.shape), _const_spec(wl.shape)],
        out_specs=[row(SB_WIDTH), kv_spec, kv_spec, row(3 * RW_WIDTH), row(LORA_PAD)],
        out_shape=[jax.ShapeDtypeStruct((m, SB_WIDTH), F32), kv_shape, kv_shape,
                   jax.ShapeDtypeStruct((m, 3 * RW_WIDTH), F32), jax.ShapeDtypeStruct((m, LORA_PAD), F32)],
        compiler_params=pltpu.CompilerParams(dimension_semantics=("arbitrary",),
                                             vmem_limit_bytes=VMEM_LIMIT),
        name="norm_proj",
    )(x.reshape(m, d), g, wq, wkv, wrkv, wl)


def _sb_kernel(q_ref, *refs, tq, q_blocks, own_in_prefix, n_prefix):
    if own_in_prefix:
        kp_ref, vp_ref, o_ref = refs
    else:
        kd_ref, vd_ref, kp_ref, vp_ref, o_ref = refs
    pairs = SB_WIDTH // LANES
    by_pair = lambda x: jnp.stack([x[:, p * LANES:(p + 1) * LANES] for p in range(pairs)])
    q_all = q_ref[0] * (1.0 / math.sqrt(HEAD_DIM))
    q = jnp.concatenate([by_pair(q_all[j * tq:(j + 1) * tq]) for j in range(q_blocks)], axis=0)
    first = lax.broadcasted_iota(jnp.int32, (pairs, tq, LANES), 2) < HEAD_DIM
    first_all = jnp.concatenate([first] * q_blocks, axis=0)
    qe = jnp.concatenate([jnp.where(first_all, q, 0.0), jnp.where(first_all, 0.0, q)], axis=1).astype(BF16)
    entries = lambda x, j: x[j * pairs:(j + 1) * pairs]

    row = lax.broadcasted_iota(jnp.int32, (pairs, 2 * tq, KEY_BLOCK), 1)
    col = lax.broadcasted_iota(jnp.int32, (pairs, 2 * tq, KEY_BLOCK), 2)
    causal = col < jnp.where(row >= tq, row - tq, row)

    jj = lax.broadcasted_iota(jnp.int32, (KEY_BLOCK, 2 * KEY_BLOCK), 0)
    ss = lax.broadcasted_iota(jnp.int32, (KEY_BLOCK, 2 * KEY_BLOCK), 1)
    suffix = jnp.where((jj > ss) | (ss >= KEY_BLOCK), 1.0, 0.0).astype(BF16)
    suffix2 = jnp.concatenate([suffix, suffix], axis=0)

    def prefix(ref, start, n_keys):
        return ref[0, :, pl.ds(start, n_keys)].reshape(pairs, LANES, n_keys)

    def suffix_sums(log1mb):
        n = log1mb.shape[0]
        hi_lo = jnp.concatenate(_split2(log1mb.reshape(n * 2 * tq, KEY_BLOCK)), axis=1)
        return _dot(hi_lo, suffix2).reshape(n, 2 * tq, 2 * KEY_BLOCK)

    def softplus(z):
        return jnp.maximum(z, 0.0) + jnp.log(1.0 + jnp.exp(-jnp.abs(z)))

    def own_block(qe, kb, vb, c, o, mask):
        z = _bdot_nt(qe, kb.astype(BF16))
        log1mb = jnp.where(mask, -softplus(z), 0.0)
        sums = suffix_sums(log1mb)
        w = jnp.where(mask, jnp.exp(z + log1mb + (sums[:, :, :KEY_BLOCK] + c)), 0.0)
        return c + sums[:, :, KEY_BLOCK:], o + _bdot(w.astype(BF16), vb.astype(BF16))

    def sweep(qe, kb, vb, c, o, mask=None):
        z = _bdot(qe, kb.astype(BF16))
        log1mb = -softplus(z)
        if mask is not None:
            log1mb = jnp.where(mask, log1mb, 0.0)
        n_blocks = kb.shape[2] // KEY_BLOCK
        after = [None] * n_blocks
        for j in reversed(range(n_blocks)):
            sums = suffix_sums(log1mb[:, :, j * KEY_BLOCK:(j + 1) * KEY_BLOCK])
            after[j] = sums[:, :, :KEY_BLOCK] + c
            c = c + sums[:, :, KEY_BLOCK:]
        w = jnp.exp(z + log1mb + jnp.concatenate(after, axis=2))
        if mask is not None:
            w = jnp.where(mask, w, 0.0)
        return c, o + _bdot_nt(w.astype(BF16), vb.astype(BF16))

    alive = lambda c: (jnp.max(c) >= SB_DEAD_LOG).astype(jnp.int32)
    zeros = jnp.zeros((q_blocks * pairs, 2 * tq, LANES), F32)
    if own_in_prefix:
        wrow = lax.broadcasted_iota(jnp.int32, (pairs, 2 * tq, SB_WINDOW * KEY_BLOCK), 1)
        wcol = lax.broadcasted_iota(jnp.int32, (pairs, 2 * tq, SB_WINDOW * KEY_BLOCK), 2)
        wrow = jnp.where(wrow >= tq, wrow - tq, wrow)
        older, keys, values, visible = [], [], [], []
        for j in range(q_blocks):
            i = pl.program_id(1) * q_blocks + j
            n = jnp.maximum(i - (SB_WINDOW - 1), 0)
            start = pl.multiple_of(n * KEY_BLOCK, KEY_BLOCK)
            older.append(n)
            keys.append(prefix(kp_ref, start, SB_WINDOW * KEY_BLOCK))
            values.append(prefix(vp_ref, start, SB_WINDOW * KEY_BLOCK))
            visible.append(start + wcol < i * tq + wrow)
        c, o = sweep(qe, jnp.concatenate(keys, axis=0), jnp.concatenate(values, axis=0), zeros, zeros,
                     jnp.concatenate(visible, axis=0))
        still = [alive(entries(c, j)) for j in range(q_blocks)]
    else:
        c, o = own_block(qe, by_pair(kd_ref[0]), by_pair(vd_ref[0]), zeros, zeros, causal)
        older = [n_prefix]
        still = [jnp.int32(1)]

    outs = []
    for j in range(q_blocks):
        n, qe_j = older[j], entries(qe, j)

        def live(carry, n=n):
            return (carry[0] < (n >> 1)) & (carry[1] > 0)

        def body(carry, n=n, qe_j=qe_j):
            it, _, c, o = carry
            start = pl.multiple_of((n - 2 - 2 * it) * KEY_BLOCK, KEY_BLOCK)
            c, o = sweep(qe_j, prefix(kp_ref, start, 2 * KEY_BLOCK), prefix(vp_ref, start, 2 * KEY_BLOCK), c, o)
            return it + 1, alive(c), c, o

        _, still_j, c_j, o_j = lax.while_loop(live, body, (jnp.int32(0), still[j], entries(c, j), entries(o, j)))
        _, o_j = lax.cond(((n & 1) == 1) & (still_j > 0),
                          lambda c, o, qe_j=qe_j: sweep(qe_j, prefix(kp_ref, 0, KEY_BLOCK),
                                                        prefix(vp_ref, 0, KEY_BLOCK), c, o),
                          lambda c, o: (c, o), c_j, o_j)
        o_j = jnp.where(first, o_j[:, :tq], o_j[:, tq:])
        outs.append(jnp.concatenate([o_j[p] for p in range(pairs)], axis=1))
    o_ref[0] = jnp.concatenate(outs, axis=0)


def _sb_attention(q, kp, vp, own=None, *, tq):
    b, t, _ = q.shape
    tp = kp.shape[2]
    blk = lambda rows: pl.BlockSpec((1, rows, SB_WIDTH), lambda bi, i: (bi, i, 0))
    pre = pl.BlockSpec((1,) + kp.shape[1:], lambda bi, i: (bi, 0, 0))
    if own is None:
        assert tq == KEY_BLOCK and tp == t and t >= SB_WINDOW * KEY_BLOCK
        q_blocks = SB_QUERY_BLOCKS if t % (SB_QUERY_BLOCKS * tq) == 0 else 1
        in_specs, args = [blk(q_blocks * tq), pre, pre], (q, kp, vp)
    else:
        q_blocks = 1
        in_specs, args = [blk(tq), blk(KEY_BLOCK), blk(KEY_BLOCK), pre, pre], (q,) + tuple(own) + (kp, vp)
    return pl.pallas_call(
        functools.partial(_sb_kernel, tq=tq, q_blocks=q_blocks, own_in_prefix=own is None,
                          n_prefix=tp // KEY_BLOCK),
        grid=(b, t // (q_blocks * tq)),
        in_specs=in_specs,
        out_specs=blk(q_blocks * tq),
        out_shape=jax.ShapeDtypeStruct(q.shape, F32),
        compiler_params=pltpu.CompilerParams(dimension_semantics=("arbitrary",) * 2,
                                             vmem_limit_bytes=VMEM_LIMIT),
        name="stickbreak",
    )(*args)


def _rwkv_kernel(x_ref, l_ref, sx_ref, sl_ref, s0_ref, mux_ref, mul_ref, w0_ref, wd_ref, a0_ref, wa_ref,
                 wg_ref, kkw_ref, kaw_ref, rkw_ref, lnw_ref, lnb_ref,
                 o_ref, sout_ref, ok_ref,
                 px_s, pl_s, y_s, z_s, qe_s, yl_s, phi_s, psi_s,
                 *, nbat, tb, t_valid, robust):
    t = pl.program_id(0)
    pairs = RW_WIDTH // LANES
    rows_all = nbat * tb
    pr = lax.broadcasted_iota(jnp.int32, (HEAD_DIM, LANES), 0)
    pc = lax.broadcasted_iota(jnp.int32, (HEAD_DIM, LANES), 1)
    to_low = jnp.where(pc == pr, 1.0, 0.0).astype(BF16)
    to_high = jnp.where(pc == pr + HEAD_DIM, 1.0, 0.0).astype(BF16)

    @pl.when(t == 0)
    def _():
        px_s[...] = sx_ref[...]
        pl_s[...] = sl_ref[...]
        for p in range(pairs):
            for bi in range(nbat):
                z_s[p * nbat + bi] = jnp.concatenate(
                    [_mm_exact_rhs(s0_ref[bi, 2 * p], to_low, 3),
                     _mm_exact_rhs(s0_ref[bi, 2 * p + 1], to_high, 3)], axis=0).T

    def shift(x_ref, carry_s, mu_ref):
        x = x_ref[...]
        prev = pltpu.roll(x.reshape(rows_all, x.shape[2]), 1, 0).reshape(x.shape)
        prev = jnp.where(lax.broadcasted_iota(jnp.int32, x.shape, 1) == 0, carry_s[...], prev)
        carry_s[...] = x[:, tb - 1:tb, :]
        return (x + (prev - x) * mu_ref[...]).reshape(rows_all, x.shape[2])

    xs = shift(x_ref, px_s, mux_ref)
    xr, xk, xv = xs[:, :RW_WIDTH], xs[:, RW_WIDTH:2 * RW_WIDTH], xs[:, 2 * RW_WIDTH:]
    xl = shift(l_ref, pl_s, mul_ref)

    hr = lax.broadcasted_iota(jnp.int32, (LANES, LANES), 0) // HEAD_DIM
    hc = lax.broadcasted_iota(jnp.int32, (LANES, LANES), 1) // HEAD_DIM
    head_ones = jnp.where(hr == hc, 1.0, 0.0).astype(BF16)
    head_sum = lambda x, parts=2: jnp.concatenate(
        [_mm_exact_rhs(x[:, p * LANES:(p + 1) * LANES], head_ones, parts) for p in range(pairs)], axis=1)

    lora = lambda x, w_ref, lo, hi: _dot(x.astype(BF16), w_ref[lo:hi, :].astype(BF16))
    log_w = -DECAY_SCALE * _sigmoid(w0_ref[...] + lora(jnp.tanh(xl[:, :LANES]), wd_ref, 0, LANES))
    a = _sigmoid(a0_ref[...] + lora(xl[:, :LANES], wa_ref, 0, LANES))
    kk = xk * kkw_ref[...]
    kk = kk * lax.rsqrt(jnp.maximum(head_sum(kk * kk, 1), 1e-24))
    k2 = xk * (1.0 + (a - 1.0) * kaw_ref[...])
    b = kk * a
    if t_valid < tb:
        valid = lax.broadcasted_iota(jnp.int32, (nbat, tb, RW_WIDTH), 1).reshape(rows_all, RW_WIDTH) < t_valid
        log_w = jnp.where(valid, log_w, 0.0)
        kk = jnp.where(valid, kk, 0.0)
        b = jnp.where(valid, b, 0.0)
        k2 = jnp.where(valid, k2, 0.0)
        xv = jnp.where(valid, xv, 0.0)
    nc = tb // CHUNK
    nb = pairs * nbat * nc
    e2 = 2 * CHUNK
    by_chunk = lambda x: jnp.stack(
        [x[:, p * LANES:(p + 1) * LANES] for p in range(pairs)]).reshape(nb, CHUNK, LANES)
    ti = lax.broadcasted_iota(jnp.int32, (nb, CHUNK, CHUNK), 1)
    tj = lax.broadcasted_iota(jnp.int32, (nb, CHUNK, CHUNK), 2)
    tri = jnp.where(tj <= ti, 1.0, 0.0).astype(BF16)
    lw3 = by_chunk(log_w)
    lw_hi, lw_lo = _split2(lw3)
    cum3 = _bdot(tri, lw_hi) + _bdot(tri, lw_lo)
    tot = cum3[:, CHUNK - 1:CHUNK, :]
    first = lax.broadcasted_iota(jnp.int32, (nb, CHUNK, LANES), 2) < HEAD_DIM
    expand = lambda x: jnp.concatenate([jnp.where(first, x, 0.0), jnp.where(first, 0.0, x)], axis=1)
    e_neg = jnp.exp(-cum3)
    e_tail = jnp.exp(tot - cum3)
    kk3, b3, k3, v3 = by_chunk(kk), by_chunk(b), by_chunk(k2), by_chunk(xv)
    kkt = kk3 * jnp.exp(cum3 - lw3)
    rt = by_chunk(xr) * jnp.exp(cum3)

    wr = lax.broadcasted_iota(jnp.int32, (nb, CHUNK, e2), 1)
    wc = lax.broadcasted_iota(jnp.int32, (nb, CHUNK, e2), 2)
    head0 = wc < CHUNK
    ws = jnp.where(head0, wc, wc - CHUNK)
    strict, incl, eye_w = ws < wr, ws <= wr, ws == wr
    diag2 = lambda x: jnp.concatenate([jnp.where(head0, x, 0.0), jnp.where(head0, 0.0, x)], axis=1)

    gram = _bdot_nt(jnp.concatenate([kkt, rt], axis=1).astype(BF16),
                    jnp.concatenate([expand(k3 * e_neg), expand(b3 * e_neg)], axis=1).astype(BF16))
    ak = jnp.where(strict, gram[:, :CHUNK, :e2], 0.0)
    ab = jnp.where(strict, gram[:, :CHUNK, e2:], 0.0)
    rk = jnp.where(incl, gram[:, CHUNK:, :e2], 0.0)
    rb = jnp.where(incl, gram[:, CHUNK:, e2:], 0.0)

    eye2 = jnp.where(eye_w, 1.0, 0.0)
    ab_d = diag2(ab)

    def inverse(product):
        tinv = eye2 - ab
        pw = product(ab, ab_d)
        n_sq = CHUNK.bit_length() - 2
        for s in range(n_sq):
            if s + 1 < n_sq:
                both = product(pw, jnp.concatenate([diag2(tinv), diag2(pw)], axis=2))
                tinv = tinv + both[:, :, :e2]
                pw = both[:, :, e2:]
            else:
                tinv = tinv + product(pw, diag2(tinv))
        return tinv, eye2 - tinv - _bmm3(ab, diag2(tinv))

    ve = expand(v3)
    akve = _bmm1(ak, ve)
    per_pair = lambda x: x.reshape((pairs * nbat, nc) + x.shape[1:])
    zr = lax.broadcasted_iota(jnp.int32, (nb, HEAD_DIM, LANES), 1)
    zc = lax.broadcasted_iota(jnp.int32, (nb, HEAD_DIM, LANES), 2)
    zhead0 = zc < HEAD_DIM
    own_block = lambda x: jnp.where(zhead0, x[:, :HEAD_DIM], x[:, HEAD_DIM:])
    decay = jnp.where(jnp.where(zhead0, zc, zc - HEAD_DIM) == zr, jnp.exp(tot), 0.0)

    def build_operators(product):
        tinv, resid = inverse(product)
        row_ok = jnp.sum(jnp.abs(resid), axis=2, keepdims=True) <= INVERSE_RESID_MAX
        tinv = tinv + _bmm1(tinv, diag2(resid))
        wu = _bmm1(tinv, jnp.concatenate([expand(kkt), expand(akve)], axis=2))
        w_, u_loc = wu[:, :, :LANES], wu[:, :, LANES:]
        yq = _bmm1(jnp.concatenate([rk, -rb], axis=2),
                   jnp.concatenate([jnp.concatenate([ve, jnp.zeros_like(ve)], axis=2),
                                    jnp.concatenate([expand(u_loc), expand(w_)], axis=2)], axis=1))
        qe_s[...] = per_pair(rt + yq[:, :, LANES:])
        yl_s[...] = per_pair(yq[:, :, :LANES])
        zp = _bmm1(jnp.swapaxes(jnp.concatenate([k3 * e_tail, -(b3 * e_tail)], axis=1), 1, 2),
                   jnp.concatenate([jnp.concatenate([v3, jnp.zeros_like(v3)], axis=2),
                                    jnp.concatenate([u_loc, w_], axis=2)], axis=1))
        phi_s[...] = per_pair(decay + own_block(zp[:, :, LANES:]))
        psi_s[...] = per_pair(own_block(zp[:, :, :LANES]))
        return jnp.min(row_ok.astype(jnp.int32))

    step_ok = build_operators(_bmm3 if robust else _bmm1)

    @pl.when(t == 0)
    def _():
        ok_ref[...] = jnp.ones(ok_ref.shape, jnp.int32)

    ok_ref[...] = jnp.minimum(ok_ref[...], step_ok)

    shead0 = lax.broadcasted_iota(jnp.int32, (pairs * nbat, HEAD_DIM, LANES), 2) < HEAD_DIM

    def advance(c, _):
        rows = pl.ds(pl.multiple_of(c * CHUNK, CHUNK), CHUNK)
        z = z_s[...]
        y_s[:, rows, :] = _bmm1(qe_s[:, c], z) + yl_s[:, c]
        z_new = _bmm3(phi_s[:, c], z) + psi_s[:, c]
        z_s[...] = jnp.concatenate([jnp.where(shead0, z_new, 0.0), jnp.where(shead0, 0.0, z_new)], axis=1)
        return 0

    lax.fori_loop(0, nc, advance, 0, unroll=min(nc, 2))

    y = jnp.concatenate([y_s[p * nbat:(p + 1) * nbat].reshape(rows_all, LANES) for p in range(pairs)], axis=1)
    d = y - head_sum(y) * (1.0 / HEAD_DIM)
    var = head_sum(d * d) * (1.0 / HEAD_DIM)
    o = d * lax.rsqrt(var + GN_EPS) * lnw_ref[...] + lnb_ref[...]
    bonus = head_sum(xr * k2 * rkw_ref[...]) * xv
    gate = lora(_sigmoid(xl[:, LANES:]), wg_ref, LANES, LORA_PAD)
    o_ref[...] = ((o + bonus) * gate).reshape(nbat, tb, RW_WIDTH)

    @pl.when(t == pl.num_programs(0) - 1)
    def _():
        for p in range(pairs):
            for bi in range(nbat):
                s_pair = z_s[p * nbat + bi].T
                sout_ref[bi, 2 * p] = s_pair[:HEAD_DIM, :HEAD_DIM]
                sout_ref[bi, 2 * p + 1] = _mm_exact_rhs(s_pair[HEAD_DIM:], to_high, 3, transposed=True)


def _rwkv(rkv, lora, shift_rkv, shift_lora, z0, prm, *, tb, t_valid, robust=False):
    b, t, _ = rkv.shape
    nt = t // tb
    nc = tb // CHUNK
    pairs = RW_WIDTH // LANES
    rows = lambda w: pl.BlockSpec((b, tb, w), lambda ti: (0, ti, 0))
    first = lambda w: pl.BlockSpec((b, 1, w), lambda ti: (0, 0, 0))
    state = pl.BlockSpec((b, 2 * pairs, HEAD_DIM, HEAD_DIM), lambda ti: (0, 0, 0, 0))
    vec = _const_spec((1, RW_WIDTH))
    mat = _const_spec((LORA_PAD, RW_WIDTH))
    return pl.pallas_call(
        functools.partial(_rwkv_kernel, nbat=b, tb=tb, t_valid=t_valid, robust=robust),
        grid=(nt,),
        in_specs=[rows(3 * RW_WIDTH), rows(LORA_PAD), first(3 * RW_WIDTH), first(LORA_PAD), state,
                  _const_spec((1, 3 * RW_WIDTH)), _const_spec((1, LORA_PAD)),
                  vec, mat, vec, mat, mat, vec, vec, vec, vec, vec],
        out_specs=[rows(RW_WIDTH), state, pl.BlockSpec((SUBLANES, LANES), lambda ti: (0, 0))],
        out_shape=[jax.ShapeDtypeStruct((b, t, RW_WIDTH), F32),
                   jax.ShapeDtypeStruct((b, 2 * pairs, HEAD_DIM, HEAD_DIM), F32),
                   jax.ShapeDtypeStruct((SUBLANES, LANES), jnp.int32)],
        scratch_shapes=[pltpu.VMEM((b, 1, 3 * RW_WIDTH), F32), pltpu.VMEM((b, 1, LORA_PAD), F32),
                        pltpu.VMEM((pairs * b, tb, LANES), F32), pltpu.VMEM((pairs * b, LANES, LANES), F32)]
        + [pltpu.VMEM((pairs * b, nc, CHUNK, LANES), F32)] * 2
        + [pltpu.VMEM((pairs * b, nc, HEAD_DIM, LANES), F32)] * 2,
        compiler_params=pltpu.CompilerParams(dimension_semantics=("arbitrary",),
                                             vmem_limit_bytes=VMEM_LIMIT),
        name="rwkv7",
    )(rkv, lora, shift_rkv, shift_lora, z0, prm["mu_rkv"], prm["mu_lora"],
      prm["w0"], prm["wd"], prm["a0"], prm["wa"], prm["wg"],
      prm["k_k"], prm["k_a"], prm["r_k"], prm["ln_w"], prm["ln_b"])


def _out_ffn_kernel(x_ref, sb_ref, rw_ref, woa_ref, wob_ref, gf_ref, wg_ref, wu_ref, wd_ref, gl_ref,
                    y_ref):
    x1 = x_ref[...] + _dot(sb_ref[...].astype(BF16), woa_ref[...]) \
        + _dot(rw_ref[...].astype(BF16), wob_ref[...])
    h2 = _rmsnorm(x1, gf_ref[...]).astype(BF16)
    gate = _dot(h2, wg_ref[...])
    act = gate * _sigmoid(gate) * _dot(h2, wu_ref[...])
    x2 = x1 + _dot(act.astype(BF16), wd_ref[...])
    y_ref[...] = _rmsnorm(x2, gl_ref[...])


def _out_ffn(x2d, o_sb, o_rw, woa, wob, gf, wg, wu, wd, gl, tm):
    m, d = x2d.shape
    row = lambda w: pl.BlockSpec((tm, w), lambda i: (i, 0))
    return pl.pallas_call(
        _out_ffn_kernel,
        grid=(m // tm,),
        in_specs=[row(d), row(SB_WIDTH), row(RW_WIDTH), _const_spec(woa.shape), _const_spec(wob.shape),
                  _const_spec((1, d)), _const_spec(wg.shape), _const_spec(wu.shape),
                  _const_spec(wd.shape), _const_spec((1, d))],
        out_specs=row(d),
        out_shape=jax.ShapeDtypeStruct((m, d), F32),
        compiler_params=pltpu.CompilerParams(dimension_semantics=("arbitrary",),
                                             vmem_limit_bytes=VMEM_LIMIT),
        name="out_ffn",
    )(x2d, o_sb, o_rw, woa, wob, gf, wg, wu, wd, gl)


def _pad_rows(x, rows):
    return jnp.pad(x, ((0, 0), (0, rows - x.shape[1]), (0, 0)))


def _pad_lanes(x, lanes):
    return jnp.pad(x, [(0, 0)] * (x.ndim - 1) + [(0, lanes - x.shape[-1])])


def _tile_sizes(b, t, feature_major_kv):
    rows = b * t
    return dict(tm_proj=math.gcd(t, PROJ_ROWS) if feature_major_kv else min(rows, PROJ_ROWS),
                tm_ffn=min(rows, FFN_ROWS), tq=min(t, KEY_BLOCK), tb=CHUNK * min(2, -(-t // CHUNK)))


def _layer(x, k_past, v_past, wkv0, shift0, w):
    b, t, d = x.shape
    tiles = _tile_sizes(b, t, feature_major_kv=k_past is None)
    tq, tb = tiles["tq"], tiles["tb"]
    x2d = x.reshape(b * t, d)
    to_feature_major = lambda a: jnp.transpose(a, (0, 2, 3, 1)).reshape(b, SB_WIDTH, a.shape[1])
    from_feature_major = lambda a: jnp.transpose(a.reshape(b, SB_HEADS, HEAD_DIM, a.shape[2]), (0, 3, 1, 2))
    if k_past is None:
        q, k, v, rkv, lora = _norm_proj(x, w["norm_mix_g"], w["w_q"], w["w_kv_t"], w["w_rkv"], w["w_lora"],
                                        tiles["tm_proj"], feature_major_kv=True)
        o_sb = _sb_attention(q.reshape(b, t, SB_WIDTH), k, v, tq=tq)
        k_heads, v_heads = from_feature_major(k), from_feature_major(v)
    else:
        q, k, v, rkv, lora = _norm_proj(x, w["norm_mix_g"], w["w_q"], w["w_kv"], w["w_rkv"], w["w_lora"],
                                        tiles["tm_proj"], feature_major_kv=False)
        q3, k3, v3 = (a.reshape(b, t, SB_WIDTH) for a in (q, k, v))
        o_sb = _sb_attention(q3, to_feature_major(k_past), to_feature_major(v_past),
                             (_pad_rows(k3, KEY_BLOCK), _pad_rows(v3, KEY_BLOCK)), tq=tq)
        k_heads, v_heads = (a.reshape(b, t, SB_HEADS, HEAD_DIM) for a in (k, v))

    rkv3 = rkv.reshape(b, t, 3 * RW_WIDTH)
    lora3 = lora.reshape(b, t, LORA_PAD)
    t_pad = -(-t // tb) * tb
    rwkv_args = (_pad_rows(rkv3, t_pad), _pad_rows(lora3, t_pad), shift0[..., :3 * RW_WIDTH],
                 _pad_lanes(shift0[..., 3 * RW_WIDTH:], LORA_PAD), wkv0, w)
    if t_pad == tb:
        o_rw, wkv_out, _ = _rwkv(*rwkv_args, tb=tb, t_valid=min(t, tb), robust=True)
    else:
        o_rw, wkv_out, inverse_ok = _rwkv(*rwkv_args, tb=tb, t_valid=min(t, tb))
        o_rw, wkv_out = lax.cond(inverse_ok[0, 0] == 1, lambda: (o_rw, wkv_out),
                                 lambda: _rwkv(*rwkv_args, tb=tb, t_valid=min(t, tb), robust=True)[:2])
    o_rw = o_rw[:, :t]
    shift_last = jnp.concatenate([rkv3[:, t - 1:, :], lora3[:, t - 1:, :LORA_WIDTH]], axis=-1)

    y = _out_ffn(x2d, o_sb.reshape(b * t, SB_WIDTH), o_rw.reshape(b * t, RW_WIDTH),
                 w["w_out_sb"], w["w_out_rw"], w["norm_ffn_g"], w["w_gate"], w["w_up"], w["w_down"],
                 w["norm_final_g"], tiles["tm_ffn"])
    return y.reshape(b, t, d), k_heads, v_heads, wkv_out, shift_last


def kernel(x_prompt, x_sample, cache_k, cache_v, state_wkv, state_shift, norm_mix_g, w_in, mu_shift, w0,
           w_decay_up, a0, w_aaa_up, w_gate_up, k_k, k_a, r_k, ln_x_w, ln_x_b, w_out, norm_ffn_g, w_gate,
           w_up, w_down, norm_final_g):
    assert w_in.shape[0] == 1, "single-layer trunk"
    l = 0
    rw3 = 3 * RW_WIDTH
    lora_rows = lambda m, lo, hi: jnp.pad(m, ((lo, LORA_PAD - hi), (0, 0)))
    w = {
        "norm_mix_g": norm_mix_g[l][None, :],
        "w_q": w_in[l][:, :SB_WIDTH].astype(BF16),
        "w_kv": w_in[l][:, SB_WIDTH:3 * SB_WIDTH].astype(BF16),
        "w_kv_t": w_in[l][:, SB_WIDTH:3 * SB_WIDTH].T.astype(BF16),
        "w_rkv": w_in[l][:, 3 * SB_WIDTH:3 * SB_WIDTH + rw3].astype(BF16),
        "w_lora": _pad_lanes(w_in[l][:, 3 * SB_WIDTH + rw3:], LORA_PAD).astype(BF16),
        "mu_rkv": mu_shift[l][None, :rw3],
        "mu_lora": _pad_lanes(mu_shift[l][None, rw3:], LORA_PAD),
        "w0": w0[l][None, :],
        "wd": lora_rows(w_decay_up[l], 0, LORA_DECAY),
        "a0": a0[l][None, :],
        "wa": lora_rows(w_aaa_up[l], LORA_DECAY, LORA_DECAY + LORA_AAA),
        "wg": lora_rows(w_gate_up[l], LORA_DECAY + LORA_AAA, LORA_WIDTH),
        "k_k": k_k[l][None, :],
        "k_a": k_a[l][None, :],
        "r_k": r_k[l].reshape(1, RW_WIDTH),
        "ln_w": ln_x_w[l][None, :],
        "ln_b": ln_x_b[l][None, :],
        "w_out_sb": w_out[l][:SB_WIDTH].astype(BF16),
        "w_out_rw": w_out[l][SB_WIDTH:].astype(BF16),
        "norm_ffn_g": norm_ffn_g[l][None, :],
        "w_gate": w_gate[l].astype(BF16),
        "w_up": w_up[l].astype(BF16),
        "w_down": w_down[l].astype(BF16),
        "norm_final_g": norm_final_g[None, :],
    }
    bp = x_prompt.shape[0]
    heads = RW_WIDTH // HEAD_DIM
    wkv_zero = jnp.zeros((bp, heads, HEAD_DIM, HEAD_DIM), x_prompt.dtype)
    shift_zero = jnp.zeros((bp, 1, rw3 + LORA_WIDTH), x_prompt.dtype)
    yp, k1, v1, s1, sh1 = _layer(x_prompt, None, None, wkv_zero, shift_zero, w)
    ys, k2, v2, s2, sh2 = _layer(x_sample, cache_k[l], cache_v[l], state_wkv[l], state_shift[l], w)
    return (yp, ys, k1[None], v1[None], s1[None], sh1[None], k2[None], v2[None], s2[None], sh2[None])
```
